```python
import math
import jax, jax.numpy as jnp
from jax import lax
import numpy as np

D_MODEL = 1024
BATCH = 8
SEQ = 4096
DEPTH = 1

PLE_DIM = 256
NORM_EPS = 1e-6

MLA_HEADS = 8
MLA_Q_RANK = 256
MLA_KV_RANK = 128
MLA_NOPE_DIM = 64
MLA_ROPE_DIM = 32
MLA_V_DIM = 64
MLA_OUT = MLA_HEADS * MLA_V_DIM
ROPE_THETA = 10000.0
Q_BLOCK = 128

SSD_HEADS = 8
SSD_HEAD_DIM = 64
SSD_GROUPS = 2
SSD_HEADS_PER_GROUP = SSD_HEADS // SSD_GROUPS
SSD_STATE = 64
SSD_CONV = 5
SSD_CHUNK = 128
SSD_INNER = SSD_HEADS * SSD_HEAD_DIM
SSD_XBC = SSD_INNER + 2 * SSD_GROUPS * SSD_STATE
SSD_DIRECTIONS = 2

D_MIX = MLA_OUT + SSD_INNER
IN_PROJ_COLS = MLA_Q_RANK + MLA_KV_RANK + MLA_ROPE_DIM + SSD_INNER + SSD_XBC + SSD_DIRECTIONS * SSD_HEADS

N_EXPERT_GROUPS = 4
EXPERTS_PER_GROUP = 8
N_EXPERTS = N_EXPERT_GROUPS * EXPERTS_PER_GROUP
TOP_K_IN_GROUP = 2
D_EXPERT = 256

kernel_name = "hybrid_mla_ssd_hmoe_ple_encoder"


def rmsnorm(x, w):
    xf = x.astype(jnp.float32)
    xf = xf * lax.rsqrt(jnp.mean(xf * xf, axis=-1, keepdims=True) + NORM_EPS)
    return xf.astype(x.dtype) * w


def rotary(t, cos, sin):
    half = t.shape[-1] // 2
    t1, t2 = t[..., :half], t[..., half:]
    return jnp.concatenate([t1 * cos - t2 * sin, t2 * cos + t1 * sin], axis=-1).astype(t.dtype)


def rotary_tables(positions):
    inv_freq = 1.0 / (ROPE_THETA ** (jnp.arange(0, MLA_ROPE_DIM, 2, dtype=jnp.float32) / MLA_ROPE_DIM))
    ang = positions.astype(jnp.float32)[..., None] * inv_freq
    return jnp.cos(ang), jnp.sin(ang)


def mla_bidirectional_attention(q_nope, q_rope, k_nope, k_rope, v):
    b, s, h, _ = q_nope.shape
    nb = s // Q_BLOCK
    scale = (MLA_NOPE_DIM + MLA_ROPE_DIM) ** -0.5

    def to_blocks(t):
        return jnp.moveaxis(t.reshape(b, nb, Q_BLOCK, *t.shape[2:]), 1, 0)

    def block(qs):
        qn, qr = qs
        sc = (jnp.einsum("bqhd,bkhd->bhqk", qn, k_nope)
              + jnp.einsum("bqhr,bkr->bhqk", qr, k_rope))
        pr = jax.nn.softmax(sc.astype(jnp.float32) * scale, axis=-1).astype(v.dtype)
        return jnp.einsum("bhqk,bkhd->bqhd", pr, v)

    out = lax.map(block, (to_blocks(q_nope), to_blocks(q_rope)))
    return jnp.moveaxis(out, 0, 1).reshape(b, s, h * MLA_V_DIM)


def centred_depthwise_conv(x, w, bias):
    c = x.shape[-1]
    out = lax.conv_general_dilated(
        x, w[:, None, :].astype(x.dtype), window_strides=(1,),
        padding=[(SSD_CONV // 2, SSD_CONV // 2)],
        dimension_numbers=("NWC", "WIO", "NWC"), feature_group_count=c)
    return out + bias


def ssd_chunked_scan(xh, dt, a, bm, cm):
    b, s, g, r, p = xh.shape
    n = bm.shape[-1]
    c = s // SSD_CHUNK
    X = (xh * dt[..., None]).reshape(b, c, SSD_CHUNK, g, r, p)
    A = (dt * a).reshape(b, c, SSD_CHUNK, g, r).transpose(0, 3, 4, 1, 2)
    Bc = bm.reshape(b, c, SSD_CHUNK, g, n)
    Cc = cm.reshape(b, c, SSD_CHUNK, g, n)
    a_cs = jnp.cumsum(A, axis=-1)
    tri = jnp.tril(jnp.ones((SSD_CHUNK, SSD_CHUNK), dtype=bool))
    seg = a_cs[..., :, None] - a_cs[..., None, :]
    Lmat = jnp.exp(jnp.where(tri, seg, -jnp.inf))
    y_diag = jnp.einsum("bclgn,bcsgn,bgrcls,bcsgrp->bclgrp", Cc, Bc, Lmat, X)
    decay_states = jnp.exp(a_cs[..., -1:] - a_cs)
    states = jnp.einsum("bclgn,bgrcl,bclgrp->bcgrpn", Bc, decay_states, X)
    chunk_decay = jnp.exp(a_cs[..., -1])

    def step(hstate, inp):
        st, dc = inp
        return hstate * dc[..., None, None] + st, hstate

    _, h_prev = lax.scan(step, jnp.zeros_like(states[:, 0]),
                         (jnp.moveaxis(states, 1, 0), jnp.moveaxis(chunk_decay, -1, 0)))
    h_prev = jnp.moveaxis(h_prev, 0, 1)
    y_off = jnp.einsum("bclgn,bcgrpn,bgrcl->bclgrp", Cc, h_prev, jnp.exp(a_cs))
    return (y_diag + y_off).reshape(b, s, g, r, p)


def hybrid_token_mixer(h, cos, sin, w_in, q_norm_w, w_uq, kv_norm_w, w_ukv, attn_out_norm_w,
                       conv_w, conv_b, dt_bias, a_log, ssd_d, ssd_norm_w, w_o):
    b, s, _ = h.shape
    proj = h @ w_in
    o1 = MLA_Q_RANK
    o2 = o1 + MLA_KV_RANK
    o3 = o2 + MLA_ROPE_DIM
    o4 = o3 + SSD_INNER
    o5 = o4 + SSD_XBC
    c_q, c_kv, k_rope, z, xbc, dt_raw = jnp.split(proj, [o1, o2, o3, o4, o5], axis=-1)

    q = (rmsnorm(c_q, q_norm_w) @ w_uq).reshape(b, s, MLA_HEADS, MLA_NOPE_DIM + MLA_ROPE_DIM)
    q_nope = q[..., :MLA_NOPE_DIM]
    q_rope = rotary(q[..., MLA_NOPE_DIM:], cos[:, :, None, :], sin[:, :, None, :])
    kv = (rmsnorm(c_kv, kv_norm_w) @ w_ukv).reshape(b, s, MLA_HEADS, MLA_NOPE_DIM + MLA_V_DIM)
    k_nope, v = kv[..., :MLA_NOPE_DIM], kv[..., MLA_NOPE_DIM:]
    k_rope = rotary(k_rope, cos, sin)
    attn = mla_bidirectional_attention(q_nope, q_rope, k_nope, k_rope, v)
    attn = rmsnorm(attn, attn_out_norm_w)

    xbc = jax.nn.silu(centred_depthwise_conv(xbc, conv_w, conv_b))
    xs, bm, cm = jnp.split(xbc, [SSD_INNER, SSD_INNER + SSD_GROUPS * SSD_STATE], axis=-1)
    xh = xs.reshape(b, s, SSD_GROUPS, SSD_HEADS_PER_GROUP, SSD_HEAD_DIM)
    bm = bm.reshape(b, s, SSD_GROUPS, SSD_STATE)
    cm = cm.reshape(b, s, SSD_GROUPS, SSD_STATE)
    dt = jax.nn.softplus(
        (dt_raw.reshape(b, s, SSD_DIRECTIONS, SSD_GROUPS, SSD_HEADS_PER_GROUP)
         + dt_bias.reshape(SSD_DIRECTIONS, SSD_GROUPS, SSD_HEADS_PER_GROUP)).astype(jnp.float32))
    a = -jnp.exp(a_log.astype(jnp.float32)).reshape(SSD_DIRECTIONS, SSD_GROUPS, SSD_HEADS_PER_GROUP)
    y_fwd = ssd_chunked_scan(xh, dt[:, :, 0], a[0], bm, cm)
    y_bwd = jnp.flip(ssd_chunked_scan(jnp.flip(xh, 1), jnp.flip(dt[:, :, 1], 1), a[1],
                                      jnp.flip(bm, 1), jnp.flip(cm, 1)), 1)
    y = y_fwd + y_bwd + xh * ssd_d.reshape(SSD_GROUPS, SSD_HEADS_PER_GROUP)[..., None]
    y = y.reshape(b, s, SSD_INNER).astype(h.dtype)
    y = rmsnorm(y * jax.nn.silu(z), ssd_norm_w)

    return jnp.concatenate([attn, y], axis=-1) @ w_o


def hierarchical_moe(h, w_router_group, b_router_group, w_router_expert, b_router_expert,
                     w_exp_gate, w_exp_up, w_exp_down):
    b, s, _ = h.shape
    group_logits = (h @ w_router_group).astype(jnp.float32) + b_router_group
    group_prob = jax.nn.softmax(group_logits, axis=-1)
    g_idx = jnp.argmax(group_logits, axis=-1)
    g_weight = jnp.max(group_prob, axis=-1, keepdims=True)
    exp_logits = ((h @ w_router_expert).astype(jnp.float32) + b_router_expert).reshape(
        b, s, N_EXPERT_GROUPS, EXPERTS_PER_GROUP)
    sel_logits = jnp.einsum("bsge,bsg->bse", exp_logits,
                            jax.nn.one_hot(g_idx, N_EXPERT_GROUPS, dtype=jnp.float32))
    top_v, top_i = lax.top_k(sel_logits, TOP_K_IN_GROUP)
    w_k = jax.nn.softmax(top_v, axis=-1) * g_weight
    e_idx = g_idx[..., None] * EXPERTS_PER_GROUP + top_i
    combine = jnp.einsum("bsk,bske->bse", w_k,
                         jax.nn.one_hot(e_idx, N_EXPERTS, dtype=jnp.float32)).astype(h.dtype)
    y = jnp.zeros_like(h)
    for e in range(N_EXPERTS):
        he = jax.nn.silu(h @ w_exp_gate[e]) * (h @ w_exp_up[e])
        y = y + combine[..., e:e + 1] * (he @ w_exp_down[e])
    return y


def setup_inputs(seed: int = 0) -> dict:
    key = jax.random.key(seed)
    ks = jax.random.split(key, 40)
    f32 = jnp.float32

    def nrm(k, shape, fan_in):
        return jax.random.normal(k, shape, f32) * fan_in ** -0.5

    def gain(k, shape):
        return 1.0 + 0.02 * jax.random.normal(k, shape, f32)

    L = DEPTH
    x = jax.random.normal(ks[0], (BATCH, SEQ, D_MODEL), f32)
    p = jax.random.normal(ks[1], (DEPTH, BATCH, SEQ, PLE_DIM), f32)
    offsets = jax.random.randint(ks[2], (BATCH, 1), 0, 1024, dtype=jnp.int32)
    positions = (jnp.arange(SEQ, dtype=jnp.int32)[None, :] + offsets).astype(jnp.int32)

    dt_init = jnp.exp(jax.random.uniform(ks[14], (L, SSD_DIRECTIONS * SSD_HEADS), f32)
                      * (math.log(0.1) - math.log(0.001)) + math.log(0.001))
    dt_bias = dt_init + jnp.log(-jnp.expm1(-dt_init))
    a_log = jnp.log(jax.random.uniform(ks[15], (L, SSD_DIRECTIONS * SSD_HEADS), f32, 1.0, 16.0))

    return {
        "x": x,
        "p": p,
        "positions": positions,
        "attn_norm_w": gain(ks[3], (L, D_MODEL)),
        "w_in": nrm(ks[4], (L, D_MODEL, IN_PROJ_COLS), D_MODEL),
        "q_norm_w": gain(ks[5], (L, MLA_Q_RANK)),
        "w_uq": nrm(ks[6], (L, MLA_Q_RANK, MLA_HEADS * (MLA_NOPE_DIM + MLA_ROPE_DIM)), MLA_Q_RANK),
        "kv_norm_w": gain(ks[7], (L, MLA_KV_RANK)),
        "w_ukv": nrm(ks[8], (L, MLA_KV_RANK, MLA_HEADS * (MLA_NOPE_DIM + MLA_V_DIM)), MLA_KV_RANK),
        "attn_out_norm_w": gain(ks[9], (L, MLA_OUT)),
        "conv_w": nrm(ks[10], (L, SSD_CONV, SSD_XBC), SSD_CONV),
        "conv_b": 0.02 * jax.random.normal(ks[11], (L, SSD_XBC), f32),
        "dt_bias": dt_bias,
        "a_log": a_log,
        "ssd_d": 1.0 + 0.1 * jax.random.normal(ks[12], (L, SSD_HEADS), f32),
        "ssd_norm_w": gain(ks[13], (L, SSD_INNER)),
        "w_o": nrm(ks[16], (L, D_MIX, D_MODEL), D_MIX),
        "ffn_norm_w": gain(ks[17], (L, D_MODEL)),
        "w_router_group": nrm(ks[18], (L, D_MODEL, N_EXPERT_GROUPS), D_MODEL),
        "b_router_group": 0.01 * jax.random.normal(ks[19], (L, N_EXPERT_GROUPS), f32),
        "w_router_expert": nrm(ks[20], (L, D_MODEL, N_EXPERTS), D_MODEL),
        "b_router_expert": 0.01 * jax.random.normal(ks[21], (L, N_EXPERTS), f32),
        "w_exp_gate": nrm(ks[22], (L, N_EXPERTS, D_MODEL, D_EXPERT), D_MODEL),
        "w_exp_up": nrm(ks[23], (L, N_EXPERTS, D_MODEL, D_EXPERT), D_MODEL),
        "w_exp_down": nrm(ks[24], (L, N_EXPERTS, D_EXPERT, D_MODEL), D_EXPERT),
        "ple_norm_w": gain(ks[25], (L, D_MODEL)),
        "w_ple_gate": nrm(ks[26], (L, D_MODEL, D_MODEL), D_MODEL),
        "b_ple_gate": 0.02 * jax.random.normal(ks[27], (L, D_MODEL), f32),
        "w_ple_proj": nrm(ks[28], (L, PLE_DIM, D_MODEL), PLE_DIM),
        "ple_post_norm_w": gain(ks[29], (L, D_MODEL)),
        "final_norm_w": gain(ks[30], (D_MODEL,)),
    }


def reference(x, p, positions, attn_norm_w, w_in, q_norm_w, w_uq, kv_norm_w, w_ukv,
              attn_out_norm_w, conv_w, conv_b, dt_bias, a_log, ssd_d, ssd_norm_w, w_o,
              ffn_norm_w, w_router_group, b_router_group, w_router_expert, b_router_expert,
              w_exp_gate, w_exp_up, w_exp_down, ple_norm_w, w_ple_gate, b_ple_gate,
              w_ple_proj, ple_post_norm_w, final_norm_w):
    cos, sin = rotary_tables(positions)
    cos = cos.astype(x.dtype)
    sin = sin.astype(x.dtype)
    for i in range(DEPTH):
        h = rmsnorm(x, attn_norm_w[i])
        x = x + hybrid_token_mixer(h, cos, sin, w_in[i], q_norm_w[i], w_uq[i], kv_norm_w[i],
                                   w_ukv[i], attn_out_norm_w[i], conv_w[i], conv_b[i],
                                   dt_bias[i], a_log[i], ssd_d[i], ssd_norm_w[i], w_o[i])
        h = rmsnorm(x, ffn_norm_w[i])
        x = x + hierarchical_moe(h, w_router_group[i], b_router_group[i], w_router_expert[i],
                                 b_router_expert[i], w_exp_gate[i], w_exp_up[i], w_exp_down[i])
        gate = jax.nn.sigmoid(rmsnorm(x, ple_norm_w[i]) @ w_ple_gate[i] + b_ple_gate[i])
        ple = rmsnorm(p[i] @ w_ple_proj[i], ple_post_norm_w[i])
        x = x + gate * ple
    return rmsnorm(x, final_norm_w)
```

```python
import functools
import math

import jax
import jax.numpy as jnp
from jax import lax
from jax.experimental import pallas as pl
from jax.experimental.pallas import tpu as pltpu

F32 = jnp.float32
BF16 = jnp.bfloat16
I32 = jnp.int32

D_MODEL = 1024
PLE_DIM = 256
NORM_EPS = 1e-6

MLA_HEADS = 8
MLA_Q_RANK = 256
MLA_KV_RANK = 128
MLA_NOPE_DIM = 64
MLA_ROPE_DIM = 32
MLA_V_DIM = 64
MLA_OUT = MLA_HEADS * MLA_V_DIM
ROPE_THETA = 10000.0
HEAD_PAD = 128

SSD_HEADS = 8
SSD_HEAD_DIM = 64
SSD_GROUPS = 2
SSD_HEADS_PER_GROUP = SSD_HEADS // SSD_GROUPS
SSD_STATE = 64
SSD_CONV = 5
SSD_CHUNK = 128
SSD_INNER = SSD_HEADS * SSD_HEAD_DIM
SSD_XBC = SSD_INNER + 2 * SSD_GROUPS * SSD_STATE
SSD_DIRECTIONS = 2
CONV_HALO = 8

N_EXPERT_GROUPS = 4
EXPERTS_PER_GROUP = 8
N_EXPERTS = N_EXPERT_GROUPS * EXPERTS_PER_GROUP
D_EXPERT = 256
PAIRS_PER_GROUP = EXPERTS_PER_GROUP * (EXPERTS_PER_GROUP - 1) // 2
N_CLASSES = N_EXPERT_GROUPS * PAIRS_PER_GROUP

LANES = 128
VMEM_LIMIT = 48 * 1024 * 1024

TOKEN_TILE = 512
ATTN_Q_TILE = 256
SSD_BLOCK = 256
MOE_TILE = 128


def _rms(x, w):
    ms = jnp.mean(x * x, axis=-1, keepdims=True)
    return x * lax.rsqrt(ms + NORM_EPS) * w


def _dot(a, b):
    return jnp.dot(a, b, preferred_element_type=F32)


def _dot_nt(a, b):
    return lax.dot_general(a, b, (((1,), (1,)), ((), ())), preferred_element_type=F32)


def _split3(x):
    x1 = x.astype(BF16)
    r1 = x - x1.astype(F32)
    x2 = r1.astype(BF16)
    x3 = (r1 - x2.astype(F32)).astype(BF16)
    return x1, x2, x3


def _sigmoid(x):
    return 1.0 / (1.0 + jnp.exp(-x))


def _softplus(x):
    return jnp.maximum(x, 0.0) + jnp.log(1.0 + jnp.exp(-jnp.abs(x)))


def _full(shape):
    nd = len(shape)
    return pl.BlockSpec(shape, lambda *_: (0,) * nd)


def _inproj_kernel(x_ref, nw_ref, wa_ref, wz_ref, wxbc_ref, wdt_ref, qnw_ref, wuq_ref, wuqs_ref,
                   kvnw_ref, wuk_ref, wuv_ref, cos_ref, sin_ref,
                   q_out, k_out, v_out, z_out, xbc_out, dt_out, *, scale):
    h = _rms(x_ref[...], nw_ref[...]).astype(BF16)
    pa = _dot(h, wa_ref[...])
    z_out[...] = _dot(h, wz_ref[...]).astype(z_out.dtype)
    xbc_out[...] = _dot(h, wxbc_ref[...]).astype(xbc_out.dtype)
    dt_out[...] = _dot(h, wdt_ref[...])
    cos = cos_ref[...]
    sin = sin_ref[...]
    o1 = MLA_Q_RANK
    o2 = o1 + MLA_KV_RANK
    cqn = _rms(pa[:, :o1], qnw_ref[...]).astype(BF16)
    ckvn = _rms(pa[:, o1:o2], kvnw_ref[...]).astype(BF16)
    q = _dot(cqn, wuq_ref[...])
    qs = _dot(cqn, wuqs_ref[...])
    kn = _dot(ckvn, wuk_ref[...])
    v_out[...] = _dot(ckvn, wuv_ref[...]).astype(v_out.dtype)
    kr = pa[:, o2:o2 + HEAD_PAD] * cos + pa[:, o2 + HEAD_PAD:o2 + 2 * HEAD_PAD] * sin
    for hh in range(MLA_HEADS):
        sl = slice(HEAD_PAD * hh, HEAD_PAD * (hh + 1))
        q_out[:, sl] = ((q[:, sl] * cos + qs[:, sl] * sin) * scale).astype(q_out.dtype)
        k_out[:, sl] = (kn[:, sl] + kr).astype(k_out.dtype)


def _inproj(x2, cos_t, sin_t, nw, wa, wz, wxbc, wdt, qnw, wuq, wuqs, kvnw, wuk, wuv):
    t = x2.shape[0]
    tm = min(TOKEN_TILE, t)
    scale = (MLA_NOPE_DIM + MLA_ROPE_DIM) ** -0.5
    row = lambda w: pl.BlockSpec((tm, w), lambda i: (i, 0))
    weights = (nw, wa, wz, wxbc, wdt, qnw, wuq, wuqs, kvnw, wuk, wuv)
    return pl.pallas_call(
        functools.partial(_inproj_kernel, scale=scale),
        grid=(t // tm,),
        in_specs=[row(D_MODEL)] + [_full(w.shape) for w in weights] + [row(HEAD_PAD), row(HEAD_PAD)],
        out_specs=[row(MLA_HEADS * HEAD_PAD), row(MLA_HEADS * HEAD_PAD), row(MLA_OUT),
                   row(SSD_INNER), row(SSD_XBC), row(LANES)],
        out_shape=[jax.ShapeDtypeStruct((t, MLA_HEADS * HEAD_PAD), BF16),
                   jax.ShapeDtypeStruct((t, MLA_HEADS * HEAD_PAD), BF16),
                   jax.ShapeDtypeStruct((t, MLA_OUT), BF16),
                   jax.ShapeDtypeStruct((t, SSD_INNER), BF16),
                   jax.ShapeDtypeStruct((t, SSD_XBC), F32),
                   jax.ShapeDtypeStruct((t, LANES), F32)],
        compiler_params=pltpu.CompilerParams(dimension_semantics=("parallel",),
                                             vmem_limit_bytes=VMEM_LIMIT),
        name="inproj",
    )(x2, *weights, cos_t, sin_t)


def _attn_kernel(q_ref, k_ref, v_ref, o_ref):
    v = v_ref[...]
    outs = []
    for a in range(2):
        sl = slice(HEAD_PAD * a, HEAD_PAD * (a + 1))
        s = _dot_nt(q_ref[:, sl], k_ref[:, sl])
        m = jnp.max(s, axis=-1, keepdims=True)
        p = jnp.exp(s - m)
        l = jnp.sum(p, axis=-1, keepdims=True)
        outs.append(_dot(p.astype(BF16), v) / l)
    lane = lax.broadcasted_iota(I32, outs[0].shape, 1)
    o_ref[...] = jnp.where(lane < MLA_V_DIM, outs[0], outs[1]).astype(o_ref.dtype)


def _attention(q, k, v, b, s):
    tq = min(ATTN_Q_TILE, s)
    nq = s // tq
    pairs = MLA_HEADS // 2
    return pl.pallas_call(
        _attn_kernel,
        grid=(b, pairs, nq),
        in_specs=[pl.BlockSpec((tq, 2 * HEAD_PAD), lambda bi, pi, qi: (bi * nq + qi, pi)),
                  pl.BlockSpec((s, 2 * HEAD_PAD), lambda bi, pi, qi: (bi, pi)),
                  pl.BlockSpec((s, 2 * MLA_V_DIM), lambda bi, pi, qi: (bi, pi))],
        out_specs=pl.BlockSpec((tq, 2 * MLA_V_DIM), lambda bi, pi, qi: (bi * nq + qi, pi)),
        out_shape=jax.ShapeDtypeStruct((b * s, MLA_OUT), BF16),
        compiler_params=pltpu.CompilerParams(
            dimension_semantics=("parallel", "parallel", "parallel"),
            vmem_limit_bytes=VMEM_LIMIT),
        name="attention",
    )(q, k, v)


def _ssd_conv(cur_ref, prev_ref, next_ref, cw_ref, cb_ref, tile_ref, has_prev, has_next):
    r = cur_ref.shape[0]
    tile_ref[0:CONV_HALO, :] = jnp.where(has_prev, prev_ref[...], 0.0)
    tile_ref[CONV_HALO:CONV_HALO + r, :] = cur_ref[...]
    tile_ref[CONV_HALO + r:2 * CONV_HALO + r, :] = jnp.where(has_next, next_ref[...], 0.0)
    acc = jnp.zeros((r, SSD_XBC), F32) + cb_ref[...]
    base = CONV_HALO - SSD_CONV // 2
    for kk in range(SSD_CONV):
        acc = acc + cw_ref[kk:kk + 1, :] * tile_ref[base + kk:base + kk + r, :]
    return acc * _sigmoid(acc)


def _ssd_chunk(act, dt_raw, direction, h_ref, dtbias, a_all, e_mat, skip):
    n = SSD_CHUNK
    gs = SSD_GROUPS * SSD_STATE
    xs = act[:, :SSD_INNER]
    bm = act[:, SSD_INNER:SSD_INNER + gs]
    cm = act[:, SSD_INNER + gs:SSD_INNER + 2 * gs]
    xs16 = xs.astype(BF16)
    bm16 = bm.astype(BF16)
    cm16 = cm.astype(BF16)

    dt_all = _softplus(dt_raw + dtbias)
    a_mat = dt_all * a_all
    ri = lax.broadcasted_iota(I32, (n, n), 0)
    ci = lax.broadcasted_iota(I32, (n, n), 1)
    mask = (ci <= ri) if direction == 0 else (ci >= ri)
    tri = jnp.where(mask, 1.0, 0.0).astype(BF16)
    a1, a2, a3 = _split3(a_mat)
    cs = _dot(tri, a1) + _dot(tri, a2) + _dot(tri, a3)
    end = n - 1 if direction == 0 else 0
    cs_end = cs[end:end + 1, :]
    w_state = dt_all * jnp.exp(cs_end - cs)
    e_off = jnp.exp(cs)
    c_dec = jnp.broadcast_to(jnp.exp(cs_end), (8, LANES))
    stack = jnp.concatenate([w_state, e_off, c_dec], axis=0).astype(BF16)
    expd = _dot(stack, e_mat)
    ws_x = expd[0:n]
    eo_x = expd[n:2 * n]
    cd_x = expd[2 * n:2 * n + 1]

    cs_t = cs.T
    dt_t = dt_all.T
    bm_t = bm.T.astype(BF16)

    pieces = []
    for g in range(SSD_GROUPS):
        gsl = slice(SSD_STATE * g, SSD_STATE * (g + 1))
        cg = cm16[:, gsl]
        gmat = _dot_nt(cg, bm16[:, gsl])
        for r in range(SSD_HEADS_PER_GROUP):
            hh = g * SSD_HEADS_PER_GROUP + r
            c = direction * SSD_HEADS + hh
            seg = cs[:, c:c + 1] - cs_t[c:c + 1, :]
            lm = jnp.where(mask, jnp.exp(jnp.where(mask, seg, 0.0)), 0.0) * dt_t[c:c + 1, :]
            mh = (gmat * lm).astype(BF16)
            pieces.append(_dot(mh, xs16[:, SSD_HEAD_DIM * hh:SSD_HEAD_DIM * (hh + 1)]))
    y = jnp.concatenate(pieces, axis=1)

    w = SSD_HEADS_PER_GROUP * SSD_HEAD_DIM
    offs = []
    for g in range(SSD_GROUPS):
        lsl = slice(w * g, w * (g + 1))
        gsl = slice(SSD_STATE * g, SSD_STATE * (g + 1))
        h_g = h_ref[:, lsl]
        offs.append(_dot(cm16[:, gsl], h_g.astype(BF16)) * eo_x[:, lsl])
        xd = (xs[:, lsl] * ws_x[:, lsl]).astype(BF16)
        h_ref[:, lsl] = h_g * cd_x[:, lsl] + _dot(bm_t[gsl, :], xd)
    y = y + jnp.concatenate(offs, axis=1)
    if skip is not None:
        y = y + xs * skip
    return y


def _ssd_kernel(xf_ref, xfp_ref, xfn_ref, dtf_ref, xb_ref, xbp_ref, xbn_ref, dtb_ref,
                cw_ref, cb_ref, dtbias_ref, alog_ref, skip_ref, e_ref,
                yf_ref, yb_ref, hf_ref, hb_ref, tile_ref):
    i = pl.program_id(1)
    nb = pl.num_programs(1)

    @pl.when(i == 0)
    def _():
        hf_ref[...] = jnp.zeros_like(hf_ref)
        hb_ref[...] = jnp.zeros_like(hb_ref)

    lane = lax.broadcasted_iota(I32, (1, LANES), 1)
    a_all = jnp.where(lane < SSD_DIRECTIONS * SSD_HEADS, -jnp.exp(alog_ref[...]), 0.0)
    dtbias = dtbias_ref[...]
    skip = skip_ref[...]
    nch = xf_ref.shape[0] // SSD_CHUNK

    act = _ssd_conv(xf_ref, xfp_ref, xfn_ref, cw_ref, cb_ref, tile_ref, i > 0, i < nb - 1)
    for c in range(nch):
        rows = slice(SSD_CHUNK * c, SSD_CHUNK * (c + 1))
        y = _ssd_chunk(act[rows], dtf_ref[rows, :], 0, hf_ref, dtbias, a_all, e_ref[0], skip)
        yf_ref[rows, :] = y.astype(yf_ref.dtype)

    act = _ssd_conv(xb_ref, xbp_ref, xbn_ref, cw_ref, cb_ref, tile_ref, i < nb - 1, i > 0)
    for c in reversed(range(nch)):
        rows = slice(SSD_CHUNK * c, SSD_CHUNK * (c + 1))
        y = _ssd_chunk(act[rows], dtb_ref[rows, :], 1, hb_ref, dtbias, a_all, e_ref[1], None)
        yb_ref[rows, :] = y.astype(yb_ref.dtype)


def _ssd(xbc, dt_raw, cw, cb, dtbias, alog, skip, e_mat, b, s):
    r = min(SSD_BLOCK, s)
    nb = s // r
    hb = r // CONV_HALO

    def cur(rev):
        return (lambda bi, i: (bi * nb + (nb - 1 - i), 0)) if rev else (lambda bi, i: (bi * nb + i, 0))

    def prev(rev):
        def f(bi, i):
            j = (nb - 1 - i) if rev else i
            return ((bi * nb + j) * hb - jnp.where(j > 0, 1, 0), 0)
        return f

    def nxt(rev):
        def f(bi, i):
            j = (nb - 1 - i) if rev else i
            return ((bi * nb + j) * hb + jnp.where(j < nb - 1, hb, 0), 0)
        return f

    def role(rev):
        return [pl.BlockSpec((r, SSD_XBC), cur(rev)),
                pl.BlockSpec((CONV_HALO, SSD_XBC), prev(rev)),
                pl.BlockSpec((CONV_HALO, SSD_XBC), nxt(rev)),
                pl.BlockSpec((r, LANES), cur(rev))]

    consts = (cw, cb, dtbias, alog, skip, e_mat)
    return pl.pallas_call(
        _ssd_kernel,
        grid=(b, nb),
        in_specs=role(False) + role(True) + [_full(c.shape) for c in consts],
        out_specs=[pl.BlockSpec((r, SSD_INNER), cur(False)), pl.BlockSpec((r, SSD_INNER), cur(True))],
        out_shape=[jax.ShapeDtypeStruct((b * s, SSD_INNER), F32)] * 2,
        scratch_shapes=[pltpu.VMEM((SSD_STATE, SSD_INNER), F32),
                        pltpu.VMEM((SSD_STATE, SSD_INNER), F32),
                        pltpu.VMEM((r + 2 * CONV_HALO, SSD_XBC), F32)],
        compiler_params=pltpu.CompilerParams(dimension_semantics=("parallel", "arbitrary"),
                                             vmem_limit_bytes=VMEM_LIMIT),
        name="ssd",
    )(xbc, xbc, xbc, dt_raw, xbc, xbc, xbc, dt_raw, *consts)


def _mixout_kernel(x_ref, attn_ref, yf_ref, yb_ref, z_ref, anw_ref, snw_ref, wo_ref, fnw_ref,
                   wr_ref, br_ref, x1_out, route_out):
    attn = _rms(attn_ref[...].astype(F32), anw_ref[...])
    z = z_ref[...].astype(F32)
    y = (yf_ref[...] + yb_ref[...]) * (z * _sigmoid(z))
    y = _rms(y, snw_ref[...])
    mix = jnp.concatenate([attn, y], axis=1).astype(BF16)
    x1 = x_ref[...] + _dot(mix, wo_ref[...])
    x1_out[...] = x1

    h = _rms(x1, fnw_ref[...])
    h1, h2, h3 = _split3(h)
    w1 = wr_ref[0]
    w2 = wr_ref[1]
    w3 = wr_ref[2]
    logits = (_dot(h1, w1) + (_dot(h1, w2) + _dot(h2, w1))
              + (_dot(h1, w3) + _dot(h2, w2) + _dot(h3, w1))) + br_ref[...]

    lane = lax.broadcasted_iota(I32, logits.shape, 1)
    ninf = -jnp.inf
    big = 4 * LANES

    def argmax_first(vals):
        vmax = jnp.max(vals, axis=-1, keepdims=True)
        idx = jnp.min(jnp.where(vals == vmax, lane, big), axis=-1, keepdims=True)
        return vmax, idx

    gl = jnp.where(lane < N_EXPERT_GROUPS, logits, ninf)
    gmax, gidx = argmax_first(gl)
    gweight = 1.0 / jnp.sum(jnp.exp(gl - gmax), axis=-1, keepdims=True)
    lo_lane = N_EXPERT_GROUPS + EXPERTS_PER_GROUP * gidx
    sl = jnp.where((lane >= lo_lane) & (lane < lo_lane + EXPERTS_PER_GROUP), logits, ninf)
    v1, i1 = argmax_first(sl)
    v2, i2 = argmax_first(jnp.where(lane == i1, ninf, sl))
    e21 = jnp.exp(v2 - v1)
    wt1 = gweight / (1.0 + e21)
    wt2 = gweight * e21 / (1.0 + e21)
    a = i1 - lo_lane
    bb = i2 - lo_lane
    swap = a > bb
    elo = jnp.where(swap, bb, a)
    ehi = jnp.where(swap, a, bb)
    wlo = jnp.where(swap, wt2, wt1)
    whi = jnp.where(swap, wt1, wt2)
    pair = (elo * (2 * EXPERTS_PER_GROUP - 1 - elo)) // 2 + (ehi - elo - 1)
    cls = gidx * PAIRS_PER_GROUP + pair
    out = jnp.where(lane == 0, cls.astype(F32), 0.0)
    out = jnp.where(lane == 1, wlo, out)
    out = jnp.where(lane == 2, whi, out)
    route_out[...] = out


def _mixout(x2, attn, yf, yb, z, anw, snw, wo, fnw, wr, br):
    t = x2.shape[0]
    tm = min(TOKEN_TILE, t)
    row = lambda w: pl.BlockSpec((tm, w), lambda i: (i, 0))
    weights = (anw, snw, wo, fnw, wr, br)
    return pl.pallas_call(
        _mixout_kernel,
        grid=(t // tm,),
        in_specs=[row(D_MODEL), row(MLA_OUT), row(SSD_INNER), row(SSD_INNER), row(SSD_INNER)]
        + [_full(w.shape) for w in weights],
        out_specs=[row(D_MODEL), row(LANES)],
        out_shape=[jax.ShapeDtypeStruct((t, D_MODEL), F32), jax.ShapeDtypeStruct((t, LANES), F32)],
        compiler_params=pltpu.CompilerParams(dimension_semantics=("parallel",),
                                             vmem_limit_bytes=VMEM_LIMIT),
        name="mixout",
    )(x2, attn, yf, yb, z, *weights)


def _moe_kernel(tile_lo_ref, tile_hi_ref, tile_valid_ref, src_ref,
                x1_hbm, wts_ref, fnw_ref, wgu_lo_ref, wdn_lo_ref, wgu_hi_ref, wdn_hi_ref,
                y_hbm, xbuf, obuf, sem_in, sem_out):
    del tile_lo_ref, tile_hi_ref
    t = pl.program_id(0)
    tile = xbuf.shape[0]
    base = t * tile

    def in_copy(r):
        tok = jnp.maximum(src_ref[base + r], 0)
        return pltpu.make_async_copy(x1_hbm.at[pl.ds(tok, 1)], xbuf.at[pl.ds(r, 1)], sem_in)

    def out_copy(r, tok):
        return pltpu.make_async_copy(obuf.at[pl.ds(r, 1)], y_hbm.at[pl.ds(tok, 1)], sem_out)

    @pl.when(tile_valid_ref[t] > 0)
    def _():
        def start_in(r, c):
            in_copy(r).start()
            return c
        lax.fori_loop(0, tile, start_in, 0)

        def wait_in(r, c):
            in_copy(r).wait()
            return c
        lax.fori_loop(0, tile, wait_in, 0)

        h = _rms(xbuf[...], fnw_ref[...]).astype(BF16)
        wts = wts_ref[...]
        acc = jnp.zeros((tile, D_MODEL), F32)
        for col, wgu_ref, wdn_ref in ((0, wgu_lo_ref, wdn_lo_ref), (1, wgu_hi_ref, wdn_hi_ref)):
            gu = _dot(h, wgu_ref[...])
            g = gu[:, :D_EXPERT]
            he = (g * _sigmoid(g) * gu[:, D_EXPERT:]).astype(BF16)
            acc = acc + wts[:, col:col + 1] * _dot(he, wdn_ref[...])
        obuf[...] = acc

        def start_out(r, c):
            tok = src_ref[base + r]

            @pl.when(tok >= 0)
            def _():
                out_copy(r, tok).start()
            return c
        lax.fori_loop(0, tile, start_out, 0)

        def wait_out(r, c):
            tok = src_ref[base + r]

            @pl.when(tok >= 0)
            def _():
                out_copy(r, tok).wait()
            return c
        lax.fori_loop(0, tile, wait_out, 0)


def _moe(x1, src, wts, tile_lo, tile_hi, tile_valid, fnw, wgu, wdn):
    t = x1.shape[0]
    n_tiles = tile_lo.shape[0]
    tile = MOE_TILE
    grid_spec = pltpu.PrefetchScalarGridSpec(
        num_scalar_prefetch=4,
        grid=(n_tiles,),
        in_specs=[pl.BlockSpec(memory_space=pl.ANY),
                  pl.BlockSpec((tile, 2), lambda i, lo, hi, va, sr: (i, 0)),
                  pl.BlockSpec((1, D_MODEL), lambda i, lo, hi, va, sr: (0, 0)),
                  pl.BlockSpec((None, D_MODEL, 2 * D_EXPERT), lambda i, lo, hi, va, sr: (lo[i], 0, 0)),
                  pl.BlockSpec((None, D_EXPERT, D_MODEL), lambda i, lo, hi, va, sr: (lo[i], 0, 0)),
                  pl.BlockSpec((None, D_MODEL, 2 * D_EXPERT), lambda i, lo, hi, va, sr: (hi[i], 0, 0)),
                  pl.BlockSpec((None, D_EXPERT, D_MODEL), lambda i, lo, hi, va, sr: (hi[i], 0, 0))],
        out_specs=pl.BlockSpec(memory_space=pl.ANY),
        scratch_shapes=[pltpu.VMEM((tile, D_MODEL), F32), pltpu.VMEM((tile, D_MODEL), F32),
                        pltpu.SemaphoreType.DMA(()), pltpu.SemaphoreType.DMA(())],
    )
    return pl.pallas_call(
        _moe_kernel,
        grid_spec=grid_spec,
        out_shape=jax.ShapeDtypeStruct((t, D_MODEL), F32),
        compiler_params=pltpu.CompilerParams(dimension_semantics=("arbitrary",),
                                             vmem_limit_bytes=VMEM_LIMIT),
        name="moe",
    )(tile_lo, tile_hi, tile_valid, src, x1, wts, fnw, wgu, wdn, wgu, wdn)


def _moe_plan(route, t):
    tile = MOE_TILE
    n_slots = t + N_CLASSES * tile
    n_slots = (n_slots // tile) * tile
    n_tiles = n_slots // tile
    cls = route[:, 0].astype(I32)
    order = jnp.argsort(cls, stable=True).astype(I32)
    cls_sorted = cls[order]
    counts = jnp.zeros((N_CLASSES,), I32).at[cls].add(1)
    padded = ((counts + tile - 1) // tile) * tile
    starts = jnp.cumsum(counts) - counts
    pad_ends = jnp.cumsum(padded)
    pad_starts = pad_ends - padded
    slot = pad_starts[cls_sorted] + (jnp.arange(t, dtype=I32) - starts[cls_sorted])
    src = jnp.full((n_slots,), -1, I32).at[slot].set(order)
    wts = jnp.zeros((n_slots, 2), F32).at[slot].set(route[order, 1:3])
    tile_start = jnp.arange(n_tiles, dtype=I32) * tile
    tile_valid = (tile_start < pad_ends[-1]).astype(I32)
    tile_cls = jnp.searchsorted(pad_ends, tile_start, side="right").astype(I32)
    last_cls = jnp.searchsorted(pad_ends, pad_ends[-1] - 1, side="right").astype(I32)
    tile_cls = jnp.where(tile_valid > 0, tile_cls, last_cls)
    grp = tile_cls // PAIRS_PER_GROUP
    pair = tile_cls % PAIRS_PER_GROUP
    lo_tab, hi_tab = [], []
    for lo in range(EXPERTS_PER_GROUP):
        for hi in range(lo + 1, EXPERTS_PER_GROUP):
            lo_tab.append(lo)
            hi_tab.append(hi)
    tile_lo = grp * EXPERTS_PER_GROUP + jnp.asarray(lo_tab, I32)[pair]
    tile_hi = grp * EXPERTS_PER_GROUP + jnp.asarray(hi_tab, I32)[pair]
    return src, wts, tile_lo, tile_hi, tile_valid


def _final_kernel(x1_ref, ym_ref, p_ref, pnw_ref, wg_ref, bg_ref, wp_ref, ppnw_ref, fnw_ref, o_ref):
    x2 = x1_ref[...] + ym_ref[...]
    gate = _sigmoid(_dot(_rms(x2, pnw_ref[...]).astype(BF16), wg_ref[...]) + bg_ref[...])
    ple = _rms(_dot(p_ref[...].astype(BF16), wp_ref[...]), ppnw_ref[...])
    o_ref[...] = _rms(x2 + gate * ple, fnw_ref[...])


def _final(x1, ym, p2, pnw, wg, bg, wp, ppnw, fnw):
    t = x1.shape[0]
    tm = min(TOKEN_TILE, t)
    row = lambda w: pl.BlockSpec((tm, w), lambda i: (i, 0))
    weights = (pnw, wg, bg, wp, ppnw, fnw)
    return pl.pallas_call(
        _final_kernel,
        grid=(t // tm,),
        in_specs=[row(D_MODEL), row(D_MODEL), row(PLE_DIM)] + [_full(w.shape) for w in weights],
        out_specs=row(D_MODEL),
        out_shape=jax.ShapeDtypeStruct((t, D_MODEL), F32),
        compiler_params=pltpu.CompilerParams(dimension_semantics=("parallel",),
                                             vmem_limit_bytes=VMEM_LIMIT),
        name="final",
    )(x1, ym, p2, *weights)


def _pad_cols(w, width, offset=0):
    out = jnp.zeros((w.shape[0], width), w.dtype)
    return out.at[:, offset:offset + w.shape[1]].set(w)


def _rope_swap(w):
    half = MLA_ROPE_DIM // 2
    return jnp.concatenate([-w[..., half:], w[..., :half]], axis=-1)


def _layer(x2, p2, cos_t, sin_t, b, s, attn_norm_w, w_in, q_norm_w, w_uq, kv_norm_w, w_ukv,
           attn_out_norm_w, conv_w, conv_b, dt_bias, a_log, ssd_d, ssd_norm_w, w_o, ffn_norm_w,
           w_router_group, b_router_group, w_router_expert, b_router_expert, w_exp_gate,
           w_exp_up, w_exp_down, ple_norm_w, w_ple_gate, b_ple_gate, w_ple_proj, ple_post_norm_w):
    t = x2.shape[0]
    r1 = lambda v: v.reshape(1, -1).astype(F32)
    o1 = MLA_Q_RANK
    o2 = o1 + MLA_KV_RANK
    o3 = o2 + MLA_ROPE_DIM
    o4 = o3 + SSD_INNER
    o5 = o4 + SSD_XBC
    w_kr = w_in[:, o2:o3]
    wa = jnp.concatenate([w_in[:, :o2], _pad_cols(w_kr, HEAD_PAD, MLA_NOPE_DIM),
                          _pad_cols(_rope_swap(w_kr), HEAD_PAD, MLA_NOPE_DIM)], axis=1).astype(BF16)
    wz = w_in[:, o3:o4].astype(BF16)
    wxbc = w_in[:, o4:o5].astype(BF16)
    wdt = _pad_cols(w_in[:, o5:], LANES).astype(BF16)
    uq = w_uq.reshape(MLA_Q_RANK, MLA_HEADS, MLA_NOPE_DIM + MLA_ROPE_DIM)
    zq = jnp.zeros((MLA_Q_RANK, MLA_HEADS, HEAD_PAD - MLA_NOPE_DIM - MLA_ROPE_DIM), F32)
    wuq = jnp.concatenate([uq, zq], axis=-1).reshape(MLA_Q_RANK, -1).astype(BF16)
    wuqs = jnp.concatenate([jnp.zeros_like(uq[..., :MLA_NOPE_DIM]), _rope_swap(uq[..., MLA_NOPE_DIM:]),
                            zq], axis=-1).reshape(MLA_Q_RANK, -1).astype(BF16)
    ukv = w_ukv.reshape(MLA_KV_RANK, MLA_HEADS, MLA_NOPE_DIM + MLA_V_DIM)
    zk = jnp.zeros((MLA_KV_RANK, MLA_HEADS, HEAD_PAD - MLA_NOPE_DIM), F32)
    wuk = jnp.concatenate([ukv[..., :MLA_NOPE_DIM], zk], axis=-1).reshape(MLA_KV_RANK, -1).astype(BF16)
    wuv = ukv[..., MLA_NOPE_DIM:].reshape(MLA_KV_RANK, -1).astype(BF16)

    q, k, v, z, xbc, dt_raw = _inproj(x2, cos_t, sin_t, r1(attn_norm_w), wa, wz, wxbc, wdt,
                                      r1(q_norm_w), wuq, wuqs, r1(kv_norm_w), wuk, wuv)
    attn = _attention(q, k, v, b, s)

    head_of_lane = jnp.arange(SSD_INNER) // SSD_HEAD_DIM
    rows = jnp.arange(LANES)[:, None]
    e_mat = jnp.stack([(rows == d * SSD_HEADS + head_of_lane[None, :]) for d in range(SSD_DIRECTIONS)]
                      ).astype(BF16)
    skip = jnp.repeat(ssd_d.astype(F32), SSD_HEAD_DIM).reshape(1, -1)
    yf, yb = _ssd(xbc, dt_raw, conv_w.astype(F32), r1(conv_b), _pad_cols(r1(dt_bias), LANES),
                  _pad_cols(r1(a_log), LANES), skip, e_mat, b, s)

    wr = _pad_cols(jnp.concatenate([w_router_group, w_router_expert], axis=1).astype(F32), LANES)
    wr3 = jnp.stack(_split3(wr))
    br = _pad_cols(jnp.concatenate([r1(b_router_group), r1(b_router_expert)], axis=1), LANES)
    x1, route = _mixout(x2, attn, yf, yb, z, r1(attn_out_norm_w), r1(ssd_norm_w), w_o.astype(BF16),
                        r1(ffn_norm_w), wr3, br)

    src, wts, tile_lo, tile_hi, tile_valid = _moe_plan(route, t)
    wgu = jnp.concatenate([w_exp_gate, w_exp_up], axis=-1).astype(BF16)
    ym = _moe(x1, src, wts, tile_lo, tile_hi, tile_valid, r1(ffn_norm_w), wgu, w_exp_down.astype(BF16))
    return x1, ym, (r1(ple_norm_w), w_ple_gate.astype(BF16), r1(b_ple_gate), w_ple_proj.astype(BF16),
                    r1(ple_post_norm_w))


def kernel(x, p, positions, attn_norm_w, w_in, q_norm_w, w_uq, kv_norm_w, w_ukv, attn_out_norm_w, conv_w, conv_b, dt_bias, a_log, ssd_d, ssd_norm_w, w_o, ffn_norm_w, w_router_group, b_router_group, w_router_expert, b_router_expert, w_exp_gate, w_exp_up, w_exp_down, ple_norm_w, w_ple_gate, b_ple_gate, w_ple_proj, ple_post_norm_w, final_norm_w):
    b, s, d = x.shape
    depth = p.shape[0]
    assert depth == 1, "the fused final stage assumes a single layer"
    t = b * s
    inv_freq = 1.0 / (ROPE_THETA ** (jnp.arange(0, MLA_ROPE_DIM, 2, dtype=F32) / MLA_ROPE_DIM))
    ang = positions.astype(F32)[..., None] * inv_freq
    cos = jnp.cos(ang).reshape(t, -1)
    sin = jnp.sin(ang).reshape(t, -1)
    ones = jnp.ones((t, MLA_NOPE_DIM), F32)
    zeros = jnp.zeros((t, HEAD_PAD - MLA_NOPE_DIM - MLA_ROPE_DIM), F32)
    cos_t = jnp.concatenate([ones, cos, cos, zeros], axis=1)
    sin_t = jnp.concatenate([0.0 * ones, sin, sin, zeros], axis=1)

    x2 = x.reshape(t, d)
    i = 0
    x1, ym, (pnw, wg, bg, wp, ppnw) = _layer(
        x2, p[i].reshape(t, -1), cos_t, sin_t, b, s, attn_norm_w[i], w_in[i], q_norm_w[i], w_uq[i],
        kv_norm_w[i], w_ukv[i], attn_out_norm_w[i], conv_w[i], conv_b[i], dt_bias[i], a_log[i],
        ssd_d[i], ssd_norm_w[i], w_o[i], ffn_norm_w[i], w_router_group[i], b_router_group[i],
        w_router_expert[i], b_router_expert[i], w_exp_gate[i], w_exp_up[i], w_exp_down[i],
        ple_norm_w[i], w_ple_gate[i], b_ple_gate[i], w_ple_proj[i], ple_post_norm_w[i])
    out = _final(x1, ym, p[i].reshape(t, -1), pnw, wg, bg, wp, ppnw, final_norm_w.reshape(1, -1).astype(F32))
    return out.reshape(b, s, d)
```

```python
import functools
import math

import jax
import jax.numpy as jnp
from jax import lax
from jax.experimental import pallas as pl
from jax.experimental.pallas import tpu as pltpu

F32 = jnp.float32
BF16 = jnp.bfloat16
I32 = jnp.int32

D_MODEL = 1024
PLE_DIM = 256
NORM_EPS = 1e-6

MLA_HEADS = 8
MLA_Q_RANK = 256
MLA_KV_RANK = 128
MLA_NOPE_DIM = 64
MLA_ROPE_DIM = 32
MLA_V_DIM = 64
MLA_OUT = MLA_HEADS * MLA_V_DIM
ROPE_THETA = 10000.0
HEAD_PAD = 128

SSD_HEADS = 8
SSD_HEAD_DIM = 64
SSD_GROUPS = 2
SSD_HEADS_PER_GROUP = SSD_HEADS // SSD_GROUPS
SSD_STATE = 64
SSD_CONV = 5
SSD_CHUNK = 128
SSD_INNER = SSD_HEADS * SSD_HEAD_DIM
SSD_XBC = SSD_INNER + 2 * SSD_GROUPS * SSD_STATE
SSD_DIRECTIONS = 2
CONV_HALO = 8

N_EXPERT_GROUPS = 4
EXPERTS_PER_GROUP = 8
N_EXPERTS = N_EXPERT_GROUPS * EXPERTS_PER_GROUP
D_EXPERT = 256
PAIRS_PER_GROUP = EXPERTS_PER_GROUP * (EXPERTS_PER_GROUP - 1) // 2
N_CLASSES = N_EXPERT_GROUPS * PAIRS_PER_GROUP

LANES = 128
ROW_EXT = D_MODEL + LANES
VMEM_LIMIT = 48 * 1024 * 1024

TOKEN_TILE = 512
ATTN_Q_TILE = 256
SSD_BLOCK = 256
MOE_TILE = 128


def _rms(x, w):
    ms = jnp.mean(x * x, axis=-1, keepdims=True)
    return x * lax.rsqrt(ms + NORM_EPS) * w


def _dot(a, b):
    return jnp.dot(a, b, preferred_element_type=F32)


def _dot_nt(a, b):
    return lax.dot_general(a, b, (((1,), (1,)), ((), ())), preferred_element_type=F32)


def _split3(x):
    x1 = x.astype(BF16)
    r1 = x - x1.astype(F32)
    x2 = r1.astype(BF16)
    x3 = (r1 - x2.astype(F32)).astype(BF16)
    return x1, x2, x3


def _sigmoid(x):
    return 1.0 / (1.0 + jnp.exp(-x))


def _softplus(x):
    return jnp.maximum(x, 0.0) + jnp.log(1.0 + jnp.exp(-jnp.abs(x)))


def _full(shape):
    nd = len(shape)
    return pl.BlockSpec(shape, lambda *_: (0,) * nd)


def _inproj_kernel(x_ref, nw_ref, wa_ref, wz_ref, wxbc_ref, wdt_ref, qnw_ref, wuq_ref, wuqs_ref,
                   kvnw_ref, wuk_ref, wuv_ref, cos_ref, sin_ref,
                   q_out, k_out, v_out, z_out, xbc_out, dt_out, *, scale):
    h = _rms(x_ref[...], nw_ref[...]).astype(BF16)
    pa = _dot(h, wa_ref[...])
    z_out[...] = _dot(h, wz_ref[...]).astype(z_out.dtype)
    xbc_out[...] = _dot(h, wxbc_ref[...]).astype(xbc_out.dtype)
    dt_out[...] = _dot(h, wdt_ref[...])
    cos = cos_ref[...]
    sin = sin_ref[...]
    o1 = MLA_Q_RANK
    o2 = o1 + MLA_KV_RANK
    cqn = _rms(pa[:, :o1], qnw_ref[...]).astype(BF16)
    ckvn = _rms(pa[:, o1:o2], kvnw_ref[...]).astype(BF16)
    q = _dot(cqn, wuq_ref[...])
    qs = _dot(cqn, wuqs_ref[...])
    kn = _dot(ckvn, wuk_ref[...])
    v_out[...] = _dot(ckvn, wuv_ref[...]).astype(v_out.dtype)
    kr = pa[:, o2:o2 + HEAD_PAD] * cos + pa[:, o2 + HEAD_PAD:o2 + 2 * HEAD_PAD] * sin
    for hh in range(MLA_HEADS):
        sl = slice(HEAD_PAD * hh, HEAD_PAD * (hh + 1))
        q_out[:, sl] = ((q[:, sl] * cos + qs[:, sl] * sin) * scale).astype(q_out.dtype)
        k_out[:, sl] = (kn[:, sl] + kr).astype(k_out.dtype)


def _inproj(x2, cos_t, sin_t, nw, wa, wz, wxbc, wdt, qnw, wuq, wuqs, kvnw, wuk, wuv):
    t = x2.shape[0]
    tm = min(TOKEN_TILE, t)
    scale = (MLA_NOPE_DIM + MLA_ROPE_DIM) ** -0.5
    row = lambda w: pl.BlockSpec((tm, w), lambda i: (i, 0))
    weights = (nw, wa, wz, wxbc, wdt, qnw, wuq, wuqs, kvnw, wuk, wuv)
    return pl.pallas_call(
        functools.partial(_inproj_kernel, scale=scale),
        grid=(t // tm,),
        in_specs=[row(D_MODEL)] + [_full(w.shape) for w in weights] + [row(HEAD_PAD), row(HEAD_PAD)],
        out_specs=[row(MLA_HEADS * HEAD_PAD), row(MLA_HEADS * HEAD_PAD), row(MLA_OUT),
                   row(SSD_INNER), row(SSD_XBC), row(LANES)],
        out_shape=[jax.ShapeDtypeStruct((t, MLA_HEADS * HEAD_PAD), BF16),
                   jax.ShapeDtypeStruct((t, MLA_HEADS * HEAD_PAD), BF16),
                   jax.ShapeDtypeStruct((t, MLA_OUT), BF16),
                   jax.ShapeDtypeStruct((t, SSD_INNER), BF16),
                   jax.ShapeDtypeStruct((t, SSD_XBC), F32),
                   jax.ShapeDtypeStruct((t, LANES), F32)],
        compiler_params=pltpu.CompilerParams(dimension_semantics=("parallel",),
                                             vmem_limit_bytes=VMEM_LIMIT),
        name="inproj",
    )(x2, *weights, cos_t, sin_t)


def _attn_kernel(q_ref, k_ref, v_ref, o_ref):
    v = v_ref[...]
    outs = []
    for a in range(2):
        sl = slice(HEAD_PAD * a, HEAD_PAD * (a + 1))
        s = _dot_nt(q_ref[:, sl], k_ref[:, sl])
        m = jnp.max(s, axis=-1, keepdims=True)
        p = jnp.exp(s - m)
        l = jnp.sum(p, axis=-1, keepdims=True)
        outs.append(_dot(p.astype(BF16), v) / l)
    lane = lax.broadcasted_iota(I32, outs[0].shape, 1)
    o_ref[...] = jnp.where(lane < MLA_V_DIM, outs[0], outs[1]).astype(o_ref.dtype)


def _attention(q, k, v, b, s):
    tq = min(ATTN_Q_TILE, s)
    nq = s // tq
    pairs = MLA_HEADS // 2
    return pl.pallas_call(
        _attn_kernel,
        grid=(b, pairs, nq),
        in_specs=[pl.BlockSpec((tq, 2 * HEAD_PAD), lambda bi, pi, qi: (bi * nq + qi, pi)),
                  pl.BlockSpec((s, 2 * HEAD_PAD), lambda bi, pi, qi: (bi, pi)),
                  pl.BlockSpec((s, 2 * MLA_V_DIM), lambda bi, pi, qi: (bi, pi))],
        out_specs=pl.BlockSpec((tq, 2 * MLA_V_DIM), lambda bi, pi, qi: (bi * nq + qi, pi)),
        out_shape=jax.ShapeDtypeStruct((b * s, MLA_OUT), BF16),
        compiler_params=pltpu.CompilerParams(
            dimension_semantics=("parallel", "parallel", "parallel"),
            vmem_limit_bytes=VMEM_LIMIT),
        name="attention",
    )(q, k, v)


def _ssd_conv(cur_ref, prev_ref, next_ref, cw_ref, cb_ref, tile_ref, has_prev, has_next):
    r = cur_ref.shape[0]
    tile_ref[0:CONV_HALO, :] = jnp.where(has_prev, prev_ref[...], 0.0)
    tile_ref[CONV_HALO:CONV_HALO + r, :] = cur_ref[...]
    tile_ref[CONV_HALO + r:2 * CONV_HALO + r, :] = jnp.where(has_next, next_ref[...], 0.0)
    acc = jnp.zeros((r, SSD_XBC), F32) + cb_ref[...]
    base = CONV_HALO - SSD_CONV // 2
    for kk in range(SSD_CONV):
        acc = acc + cw_ref[kk:kk + 1, :] * tile_ref[base + kk:base + kk + r, :]
    return acc * _sigmoid(acc)


def _ssd_chunk(act, dt_raw, direction, h_ref, dtbias, a_all, e_mat, skip):
    n = SSD_CHUNK
    gs = SSD_GROUPS * SSD_STATE
    xs = act[:, :SSD_INNER]
    bm = act[:, SSD_INNER:SSD_INNER + gs]
    cm = act[:, SSD_INNER + gs:SSD_INNER + 2 * gs]
    xs16 = xs.astype(BF16)
    bm16 = bm.astype(BF16)
    cm16 = cm.astype(BF16)

    dt_all = _softplus(dt_raw + dtbias)
    a_mat = dt_all * a_all
    ri = lax.broadcasted_iota(I32, (n, n), 0)
    ci = lax.broadcasted_iota(I32, (n, n), 1)
    mask = (ci <= ri) if direction == 0 else (ci >= ri)
    tri = jnp.where(mask, 1.0, 0.0).astype(BF16)
    a1, a2, a3 = _split3(a_mat)
    cs = _dot(tri, a1) + _dot(tri, a2) + _dot(tri, a3)
    end = n - 1 if direction == 0 else 0
    cs_end = cs[end:end + 1, :]
    w_state = dt_all * jnp.exp(cs_end - cs)
    e_off = jnp.exp(cs)
    c_dec = jnp.broadcast_to(jnp.exp(cs_end), (8, LANES))
    stack = jnp.concatenate([w_state, e_off, c_dec], axis=0).astype(BF16)
    expd = _dot(stack, e_mat)
    ws_x = expd[0:n]
    eo_x = expd[n:2 * n]
    cd_x = expd[2 * n:2 * n + 1]

    cs_t = cs.T
    dt_t = dt_all.T
    bm_t = bm.T.astype(BF16)

    pieces = []
    for g in range(SSD_GROUPS):
        gsl = slice(SSD_STATE * g, SSD_STATE * (g + 1))
        cg = cm16[:, gsl]
        gmat = _dot_nt(cg, bm16[:, gsl])
        for r in range(SSD_HEADS_PER_GROUP):
            hh = g * SSD_HEADS_PER_GROUP + r
            c = direction * SSD_HEADS + hh
            seg = cs[:, c:c + 1] - cs_t[c:c + 1, :]
            lm = jnp.where(mask, jnp.exp(jnp.where(mask, seg, 0.0)), 0.0) * dt_t[c:c + 1, :]
            mh = (gmat * lm).astype(BF16)
            pieces.append(_dot(mh, xs16[:, SSD_HEAD_DIM * hh:SSD_HEAD_DIM * (hh + 1)]))
    y = jnp.concatenate(pieces, axis=1)

    w = SSD_HEADS_PER_GROUP * SSD_HEAD_DIM
    offs = []
    for g in range(SSD_GROUPS):
        lsl = slice(w * g, w * (g + 1))
        gsl = slice(SSD_STATE * g, SSD_STATE * (g + 1))
        h_g = h_ref[:, lsl]
        offs.append(_dot(cm16[:, gsl], h_g.astype(BF16)) * eo_x[:, lsl])
        xd = (xs[:, lsl] * ws_x[:, lsl]).astype(BF16)
        h_ref[:, lsl] = h_g * cd_x[:, lsl] + _dot(bm_t[gsl, :], xd)
    y = y + jnp.concatenate(offs, axis=1)
    if skip is not None:
        y = y + xs * skip
    return y


def _ssd_kernel(xf_ref, xfp_ref, xfn_ref, dtf_ref, xb_ref, xbp_ref, xbn_ref, dtb_ref,
                cw_ref, cb_ref, dtbias_ref, alog_ref, skip_ref, e_ref,
                yf_ref, yb_ref, hf_ref, hb_ref, tile_ref):
    i = pl.program_id(1)
    nb = pl.num_programs(1)

    @pl.when(i == 0)
    def _():
        hf_ref[...] = jnp.zeros_like(hf_ref)
        hb_ref[...] = jnp.zeros_like(hb_ref)

    lane = lax.broadcasted_iota(I32, (1, LANES), 1)
    a_all = jnp.where(lane < SSD_DIRECTIONS * SSD_HEADS, -jnp.exp(alog_ref[...]), 0.0)
    dtbias = dtbias_ref[...]
    skip = skip_ref[...]
    nch = xf_ref.shape[0] // SSD_CHUNK

    act = _ssd_conv(xf_ref, xfp_ref, xfn_ref, cw_ref, cb_ref, tile_ref, i > 0, i < nb - 1)
    for c in range(nch):
        rows = slice(SSD_CHUNK * c, SSD_CHUNK * (c + 1))
        y = _ssd_chunk(act[rows], dtf_ref[rows, :], 0, hf_ref, dtbias, a_all, e_ref[0], skip)
        yf_ref[rows, :] = y.astype(yf_ref.dtype)

    act = _ssd_conv(xb_ref, xbp_ref, xbn_ref, cw_ref, cb_ref, tile_ref, i < nb - 1, i > 0)
    for c in reversed(range(nch)):
        rows = slice(SSD_CHUNK * c, SSD_CHUNK * (c + 1))
        y = _ssd_chunk(act[rows], dtb_ref[rows, :], 1, hb_ref, dtbias, a_all, e_ref[1], None)
        yb_ref[rows, :] = y.astype(yb_ref.dtype)


def _ssd(xbc, dt_raw, cw, cb, dtbias, alog, skip, e_mat, b, s):
    r = min(SSD_BLOCK, s)
    nb = s // r
    hb = r // CONV_HALO

    def cur(rev):
        return (lambda bi, i: (bi * nb + (nb - 1 - i), 0)) if rev else (lambda bi, i: (bi * nb + i, 0))

    def prev(rev):
        def f(bi, i):
            j = (nb - 1 - i) if rev else i
            return ((bi * nb + j) * hb - jnp.where(j > 0, 1, 0), 0)
        return f

    def nxt(rev):
        def f(bi, i):
            j = (nb - 1 - i) if rev else i
            return ((bi * nb + j) * hb + jnp.where(j < nb - 1, hb, 0), 0)
        return f

    def role(rev):
        return [pl.BlockSpec((r, SSD_XBC), cur(rev)),
                pl.BlockSpec((CONV_HALO, SSD_XBC), prev(rev)),
                pl.BlockSpec((CONV_HALO, SSD_XBC), nxt(rev)),
                pl.BlockSpec((r, LANES), cur(rev))]

    consts = (cw, cb, dtbias, alog, skip, e_mat)
    return pl.pallas_call(
        _ssd_kernel,
        grid=(b, nb),
        in_specs=role(False) + role(True) + [_full(c.shape) for c in consts],
        out_specs=[pl.BlockSpec((r, SSD_INNER), cur(False)), pl.BlockSpec((r, SSD_INNER), cur(True))],
        out_shape=[jax.ShapeDtypeStruct((b * s, SSD_INNER), F32)] * 2,
        scratch_shapes=[pltpu.VMEM((SSD_STATE, SSD_INNER), F32),
                        pltpu.VMEM((SSD_STATE, SSD_INNER), F32),
                        pltpu.VMEM((r + 2 * CONV_HALO, SSD_XBC), F32)],
        compiler_params=pltpu.CompilerParams(dimension_semantics=("parallel", "arbitrary"),
                                             vmem_limit_bytes=VMEM_LIMIT),
        name="ssd",
    )(xbc, xbc, xbc, dt_raw, xbc, xbc, xbc, dt_raw, *consts)


def _mixout_kernel(x_ref, attn_ref, yf_ref, yb_ref, z_ref, anw_ref, snw_ref, wo_ref, fnw_ref,
                   wr_ref, br_ref, x1_out, route_out):
    attn = _rms(attn_ref[...].astype(F32), anw_ref[...])
    z = z_ref[...].astype(F32)
    y = (yf_ref[...] + yb_ref[...]) * (z * _sigmoid(z))
    y = _rms(y, snw_ref[...])
    mix = jnp.concatenate([attn, y], axis=1).astype(BF16)
    x1 = x_ref[...] + _dot(mix, wo_ref[...])
    x1_out[:, :D_MODEL] = x1

    h = _rms(x1, fnw_ref[...])
    h1, h2, h3 = _split3(h)
    w1 = wr_ref[0]
    w2 = wr_ref[1]
    w3 = wr_ref[2]
    logits = (_dot(h1, w1) + (_dot(h1, w2) + _dot(h2, w1))
              + (_dot(h1, w3) + _dot(h2, w2) + _dot(h3, w1))) + br_ref[...]

    lane = lax.broadcasted_iota(I32, logits.shape, 1)
    ninf = -jnp.inf
    big = 4 * LANES

    def argmax_first(vals):
        vmax = jnp.max(vals, axis=-1, keepdims=True)
        idx = jnp.min(jnp.where(vals == vmax, lane, big), axis=-1, keepdims=True)
        return vmax, idx

    gl = jnp.where(lane < N_EXPERT_GROUPS, logits, ninf)
    gmax, gidx = argmax_first(gl)
    gweight = 1.0 / jnp.sum(jnp.exp(gl - gmax), axis=-1, keepdims=True)
    lo_lane = N_EXPERT_GROUPS + EXPERTS_PER_GROUP * gidx
    sl = jnp.where((lane >= lo_lane) & (lane < lo_lane + EXPERTS_PER_GROUP), logits, ninf)
    v1, i1 = argmax_first(sl)
    v2, i2 = argmax_first(jnp.where(lane == i1, ninf, sl))
    e21 = jnp.exp(v2 - v1)
    wt1 = gweight / (1.0 + e21)
    wt2 = gweight * e21 / (1.0 + e21)
    a = i1 - lo_lane
    bb = i2 - lo_lane
    swap = a > bb
    elo = jnp.where(swap, bb, a)
    ehi = jnp.where(swap, a, bb)
    wlo = jnp.where(swap, wt2, wt1)
    whi = jnp.where(swap, wt1, wt2)
    pair = (elo * (2 * EXPERTS_PER_GROUP - 1 - elo)) // 2 + (ehi - elo - 1)
    cls = gidx * PAIRS_PER_GROUP + pair
    route_out[...] = jnp.broadcast_to(cls.astype(F32), logits.shape)
    x1_out[:, D_MODEL:] = jnp.where(lane < LANES // 2, wlo, whi)


def _mixout(x2, attn, yf, yb, z, anw, snw, wo, fnw, wr, br):
    t = x2.shape[0]
    tm = min(TOKEN_TILE, t)
    row = lambda w: pl.BlockSpec((tm, w), lambda i: (i, 0))
    weights = (anw, snw, wo, fnw, wr, br)
    return pl.pallas_call(
        _mixout_kernel,
        grid=(t // tm,),
        in_specs=[row(D_MODEL), row(MLA_OUT), row(SSD_INNER), row(SSD_INNER), row(SSD_INNER)]
        + [_full(w.shape) for w in weights],
        out_specs=[row(ROW_EXT), row(LANES)],
        out_shape=[jax.ShapeDtypeStruct((t, ROW_EXT), F32), jax.ShapeDtypeStruct((t, LANES), F32)],
        compiler_params=pltpu.CompilerParams(dimension_semantics=("parallel",),
                                             vmem_limit_bytes=VMEM_LIMIT),
        name="mixout",
    )(x2, attn, yf, yb, z, *weights)


def _moe_kernel(tile_lo_ref, tile_hi_ref, tile_j_ref, tile_n_ref, order_ref,
                x1_hbm, fnw_ref, wgu_lo_ref, wdn_lo_ref, wgu_hi_ref, wdn_hi_ref,
                y_hbm, xbuf, obuf, sem_in, sem_out):
    del tile_lo_ref, tile_hi_ref
    t = pl.program_id(0)
    nt = pl.num_programs(0)
    tile = xbuf.shape[1]
    slot = t % 2

    def gather(tt, sl, wait):
        j0 = tile_j_ref[tt]
        if wait:
            pltpu.make_async_copy(x1_hbm.at[pl.ds(0, tile)], xbuf.at[sl], sem_in.at[sl]).wait()
            return

        def body(r, c):
            tok = order_ref[j0 + r]
            pltpu.make_async_copy(x1_hbm.at[pl.ds(tok, 1)], xbuf.at[sl, pl.ds(r, 1)],
                                  sem_in.at[sl]).start()
            return c
        lax.fori_loop(0, tile, body, 0, unroll=8)

    def scatter_rows(tt, sl, r0, count):
        j0 = tile_j_ref[tt]
        for k in range(count):
            tok = order_ref[j0 + r0 + k]
            pltpu.make_async_copy(obuf.at[sl, pl.ds(r0 + k, 1)], y_hbm.at[pl.ds(tok, 1)],
                                  sem_out.at[sl]).start()

    def scatter(tt, sl):
        n = tile_n_ref[tt]
        groups = n // 8

        def body(g, c):
            scatter_rows(tt, sl, g * 8, 8)
            return c
        lax.fori_loop(0, groups, body, 0)
        done = groups * 8
        for part in (4, 2, 1):
            take = (n & part) > 0

            @pl.when(take)
            def _(done=done, part=part):
                scatter_rows(tt, sl, done, part)
            done = done + jnp.where(take, part, 0)

    def scatter_wait(tt, sl):
        n = tile_n_ref[tt]
        rows8 = pl.multiple_of((n // 8) * 8, 8)

        @pl.when(rows8 > 0)
        def _():
            pltpu.make_async_copy(obuf.at[sl, pl.ds(0, rows8)], y_hbm.at[pl.ds(0, rows8)],
                                  sem_out.at[sl]).wait()
        for part in (4, 2, 1):
            @pl.when((n & part) > 0)
            def _(part=part):
                for _k in range(part):
                    pltpu.make_async_copy(obuf.at[sl, pl.ds(0, 1)], y_hbm.at[pl.ds(0, 1)],
                                          sem_out.at[sl]).wait()

    valid = tile_n_ref[t] > 0
    nxt = jnp.minimum(t + 1, nt - 1)
    next_valid = (t + 1 < nt) & (tile_n_ref[nxt] > 0)

    @pl.when(valid & (t == 0))
    def _():
        gather(t, slot, wait=False)

    @pl.when(valid)
    def _():
        gather(t, slot, wait=True)

        @pl.when(next_valid)
        def _():
            gather(nxt, 1 - slot, wait=False)

        xe = xbuf[slot]
        h = _rms(xe[:, :D_MODEL], fnw_ref[...]).astype(BF16)
        acc = None
        for half, wgu_ref, wdn_ref in ((0, wgu_lo_ref, wdn_lo_ref), (1, wgu_hi_ref, wdn_hi_ref)):
            wt = xe[:, D_MODEL + (LANES // 2) * half:D_MODEL + (LANES // 2) * half + 1]
            gu = _dot(h, wgu_ref[...])
            g = gu[:, :D_EXPERT]
            he = (g * _sigmoid(g) * gu[:, D_EXPERT:] * wt).astype(BF16)
            d = _dot(he, wdn_ref[...])
            acc = d if acc is None else acc + d
        obuf[slot] = acc
        scatter(t, slot)

        @pl.when(t > 0)
        def _():
            scatter_wait(t - 1, 1 - slot)

        @pl.when(jnp.logical_not(next_valid))
        def _():
            scatter_wait(t, slot)


def _moe(x1e, order, tile_lo, tile_hi, tile_j, tile_n, fnw, wgu, wdn):
    t = x1e.shape[0]
    n_tiles = tile_lo.shape[0]
    tile = MOE_TILE
    wspec = lambda shape, which: pl.BlockSpec(
        (None,) + shape, lambda i, lo, hi, tj, tn, od: ((lo, hi)[which][i], 0, 0))
    grid_spec = pltpu.PrefetchScalarGridSpec(
        num_scalar_prefetch=5,
        grid=(n_tiles,),
        in_specs=[pl.BlockSpec(memory_space=pl.ANY),
                  pl.BlockSpec((1, D_MODEL), lambda i, *_: (0, 0)),
                  wspec((D_MODEL, 2 * D_EXPERT), 0), wspec((D_EXPERT, D_MODEL), 0),
                  wspec((D_MODEL, 2 * D_EXPERT), 1), wspec((D_EXPERT, D_MODEL), 1)],
        out_specs=pl.BlockSpec(memory_space=pl.ANY),
        scratch_shapes=[pltpu.VMEM((2, tile, ROW_EXT), F32), pltpu.VMEM((2, tile, D_MODEL), F32),
                        pltpu.SemaphoreType.DMA((2,)), pltpu.SemaphoreType.DMA((2,))],
    )
    return pl.pallas_call(
        _moe_kernel,
        grid_spec=grid_spec,
        out_shape=jax.ShapeDtypeStruct((t, D_MODEL), F32),
        compiler_params=pltpu.CompilerParams(dimension_semantics=("arbitrary",),
                                             vmem_limit_bytes=VMEM_LIMIT),
        name="moe",
    )(tile_lo, tile_hi, tile_j, tile_n, order, x1e, fnw, wgu, wdn, wgu, wdn)


def _moe_plan(route, t):
    tile = MOE_TILE
    n_tiles = t // tile + N_CLASSES
    cls = route[:, 0].astype(I32)
    _, order = lax.sort((cls, jnp.arange(t, dtype=I32)), num_keys=1, is_stable=True)
    order = jnp.concatenate([order, jnp.zeros((tile,), I32)])
    class_ids = jnp.arange(N_CLASSES, dtype=I32)
    counts = jnp.sum((cls[:, None] == class_ids[None, :]).astype(I32), axis=0)
    tiles_per = (counts + tile - 1) // tile
    tile_end = jnp.cumsum(tiles_per)
    tile_begin = tile_end - tiles_per
    starts = jnp.cumsum(counts) - counts
    t_idx = jnp.arange(n_tiles, dtype=I32)
    tile_cls = jnp.sum((tile_end[None, :] <= t_idx[:, None]).astype(I32), axis=1)
    valid = t_idx < tile_end[-1]
    last_cls = jnp.max(jnp.where(counts > 0, class_ids, 0))
    tile_cls = jnp.where(valid, tile_cls, last_cls)
    onehot = (tile_cls[:, None] == class_ids[None, :]).astype(I32)
    pick = lambda v: jnp.sum(onehot * v[None, :], axis=1)
    k = t_idx - pick(tile_begin)
    tile_j = jnp.where(valid, pick(starts) + k * tile, 0)
    tile_n = jnp.where(valid, jnp.clip(pick(counts) - k * tile, 0, tile), 0)
    lo_of_pair, hi_of_pair = [], []
    for lo in range(EXPERTS_PER_GROUP):
        for hi in range(lo + 1, EXPERTS_PER_GROUP):
            lo_of_pair.append(lo)
            hi_of_pair.append(hi)
    grp = class_ids // PAIRS_PER_GROUP
    class_lo = grp * EXPERTS_PER_GROUP + jnp.asarray(lo_of_pair * N_EXPERT_GROUPS, I32)
    class_hi = grp * EXPERTS_PER_GROUP + jnp.asarray(hi_of_pair * N_EXPERT_GROUPS, I32)
    return order, pick(class_lo), pick(class_hi), tile_j.astype(I32), tile_n.astype(I32)


def _final_kernel(x1_ref, ym_ref, p_ref, pnw_ref, wg_ref, bg_ref, wp_ref, ppnw_ref, fnw_ref, o_ref):
    x2 = x1_ref[:, :D_MODEL] + ym_ref[...]
    gate =_sigmoid(_dot(_rms(x2, pnw_ref[...]).astype(BF16), wg_ref[...]) + bg_ref[...])
    ple = _rms(_dot(p_ref[...].astype(BF16), wp_ref[...]), ppnw_ref[...])
    o_ref[...] = _rms(x2 + gate * ple, fnw_ref[...])


def _final(x1, ym, p2, pnw, wg, bg, wp, ppnw, fnw):
    t = x1.shape[0]
    tm = min(TOKEN_TILE, t)
    row = lambda w: pl.BlockSpec((tm, w), lambda i: (i, 0))
    weights = (pnw, wg, bg, wp, ppnw, fnw)
    return pl.pallas_call(
        _final_kernel,
        grid=(t // tm,),
        in_specs=[row(ROW_EXT), row(D_MODEL), row(PLE_DIM)] + [_full(w.shape) for w in weights],
        out_specs=row(D_MODEL),
        out_shape=jax.ShapeDtypeStruct((t, D_MODEL), F32),
        compiler_params=pltpu.CompilerParams(dimension_semantics=("parallel",),
                                             vmem_limit_bytes=VMEM_LIMIT),
        name="final",
    )(x1, ym, p2, *weights)


def _pad_cols(w, width, offset=0):
    out = jnp.zeros((w.shape[0], width), w.dtype)
    return out.at[:, offset:offset + w.shape[1]].set(w)


def _rope_swap(w):
    half = MLA_ROPE_DIM // 2
    return jnp.concatenate([-w[..., half:], w[..., :half]], axis=-1)


def _layer(x2, p2, cos_t, sin_t, b, s, attn_norm_w, w_in, q_norm_w, w_uq, kv_norm_w, w_ukv,
           attn_out_norm_w, conv_w, conv_b, dt_bias, a_log, ssd_d, ssd_norm_w, w_o, ffn_norm_w,
           w_router_group, b_router_group, w_router_expert, b_router_expert, w_exp_gate,
           w_exp_up, w_exp_down, ple_norm_w, w_ple_gate, b_ple_gate, w_ple_proj, ple_post_norm_w):
    t = x2.shape[0]
    r1 = lambda v: v.reshape(1, -1).astype(F32)
    o1 = MLA_Q_RANK
    o2 = o1 + MLA_KV_RANK
    o3 = o2 + MLA_ROPE_DIM
    o4 = o3 + SSD_INNER
    o5 = o4 + SSD_XBC
    w_kr = w_in[:, o2:o3]
    wa = jnp.concatenate([w_in[:, :o2], _pad_cols(w_kr, HEAD_PAD, MLA_NOPE_DIM),
                          _pad_cols(_rope_swap(w_kr), HEAD_PAD, MLA_NOPE_DIM)], axis=1).astype(BF16)
    wz = w_in[:, o3:o4].astype(BF16)
    wxbc = w_in[:, o4:o5].astype(BF16)
    wdt = _pad_cols(w_in[:, o5:], LANES).astype(BF16)
    uq = w_uq.reshape(MLA_Q_RANK, MLA_HEADS, MLA_NOPE_DIM + MLA_ROPE_DIM)
    zq = jnp.zeros((MLA_Q_RANK, MLA_HEADS, HEAD_PAD - MLA_NOPE_DIM - MLA_ROPE_DIM), F32)
    wuq = jnp.concatenate([uq, zq], axis=-1).reshape(MLA_Q_RANK, -1).astype(BF16)
    wuqs = jnp.concatenate([jnp.zeros_like(uq[..., :MLA_NOPE_DIM]), _rope_swap(uq[..., MLA_NOPE_DIM:]),
                            zq], axis=-1).reshape(MLA_Q_RANK, -1).astype(BF16)
    ukv = w_ukv.reshape(MLA_KV_RANK, MLA_HEADS, MLA_NOPE_DIM + MLA_V_DIM)
    zk = jnp.zeros((MLA_KV_RANK, MLA_HEADS, HEAD_PAD - MLA_NOPE_DIM), F32)
    wuk = jnp.concatenate([ukv[..., :MLA_NOPE_DIM], zk], axis=-1).reshape(MLA_KV_RANK, -1).astype(BF16)
    wuv = ukv[..., MLA_NOPE_DIM:].reshape(MLA_KV_RANK, -1).astype(BF16)

    q, k, v, z, xbc, dt_raw = _inproj(x2, cos_t, sin_t, r1(attn_norm_w), wa, wz, wxbc, wdt,
                                      r1(q_norm_w), wuq, wuqs, r1(kv_norm_w), wuk, wuv)
    attn = _attention(q, k, v, b, s)

    head_of_lane = jnp.arange(SSD_INNER) // SSD_HEAD_DIM
    rows = jnp.arange(LANES)[:, None]
    e_mat = jnp.stack([(rows == d * SSD_HEADS + head_of_lane[None, :]) for d in range(SSD_DIRECTIONS)]
                      ).astype(BF16)
    skip = jnp.repeat(ssd_d.astype(F32), SSD_HEAD_DIM).reshape(1, -1)
    yf, yb = _ssd(xbc, dt_raw, conv_w.astype(F32), r1(conv_b), _pad_cols(r1(dt_bias), LANES),
                  _pad_cols(r1(a_log), LANES), skip, e_mat, b, s)

    wr = _pad_cols(jnp.concatenate([w_router_group, w_router_expert], axis=1).astype(F32), LANES)
    wr3 = jnp.stack(_split3(wr))
    br = _pad_cols(jnp.concatenate([r1(b_router_group), r1(b_router_expert)], axis=1), LANES)
    x1, route = _mixout(x2, attn, yf, yb, z, r1(attn_out_norm_w), r1(ssd_norm_w), w_o.astype(BF16),
                        r1(ffn_norm_w), wr3, br)

    order, tile_lo, tile_hi, tile_j, tile_n = _moe_plan(route, t)
    wgu = jnp.concatenate([w_exp_gate, w_exp_up], axis=-1).astype(BF16)
    ym = _moe(x1, order, tile_lo, tile_hi, tile_j, tile_n, r1(ffn_norm_w), wgu, w_exp_down.astype(BF16))
    return x1, ym, (r1(ple_norm_w), w_ple_gate.astype(BF16), r1(b_ple_gate), w_ple_proj.astype(BF16),
                    r1(ple_post_norm_w))


def kernel(x, p, positions, attn_norm_w, w_in, q_norm_w, w_uq, kv_norm_w, w_ukv, attn_out_norm_w, conv_w, conv_b, dt_bias, a_log, ssd_d, ssd_norm_w, w_o, ffn_norm_w, w_router_group, b_router_group, w_router_expert, b_router_expert, w_exp_gate, w_exp_up, w_exp_down, ple_norm_w, w_ple_gate, b_ple_gate, w_ple_proj, ple_post_norm_w, final_norm_w):
    b, s, d = x.shape
    depth = p.shape[0]
    assert depth == 1, "the fused final stage assumes a single layer"
    t = b * s
    inv_freq = 1.0 / (ROPE_THETA ** (jnp.arange(0, MLA_ROPE_DIM, 2, dtype=F32) / MLA_ROPE_DIM))
    half = MLA_ROPE_DIM // 2
    ang = (positions.astype(F32).reshape(t, 1) * inv_freq).reshape(t * half // LANES, LANES)
    cos, sin = lax.optimization_barrier((jnp.cos(ang), jnp.sin(ang)))
    cos = cos.reshape(t, half)
    sin = sin.reshape(t, half)
    ones = jnp.ones((t, MLA_NOPE_DIM), F32)
    zeros = jnp.zeros((t, HEAD_PAD - MLA_NOPE_DIM - MLA_ROPE_DIM), F32)
    cos_t = jnp.concatenate([ones, cos, cos, zeros], axis=1)
    sin_t = jnp.concatenate([0.0 * ones, sin, sin, zeros], axis=1)

    x2 = x.reshape(t, d)
    i = 0
    x1, ym, (pnw, wg, bg, wp, ppnw) = _layer(
        x2, p[i].reshape(t, -1), cos_t, sin_t, b, s, attn_norm_w[i], w_in[i], q_norm_w[i], w_uq[i],
        kv_norm_w[i], w_ukv[i], attn_out_norm_w[i], conv_w[i], conv_b[i], dt_bias[i], a_log[i],
        ssd_d[i], ssd_norm_w[i], w_o[i], ffn_norm_w[i], w_router_group[i], b_router_group[i],
        w_router_expert[i], b_router_expert[i], w_exp_gate[i], w_exp_up[i], w_exp_down[i],
        ple_norm_w[i], w_ple_gate[i], b_ple_gate[i], w_ple_proj[i], ple_post_norm_w[i])
    out = _final(x1, ym, p[i].reshape(t, -1), pnw, wg, bg, wp, ppnw, final_norm_w.reshape(1, -1).astype(F32))
    return out.reshape(b, s, d)
```

```python
import functools
import math

import jax
import jax.numpy as jnp
from jax import lax
from jax.experimental import pallas as pl
from jax.experimental.pallas import tpu as pltpu

F32 = jnp.float32
BF16 = jnp.bfloat16
I32 = jnp.int32

D_MODEL = 1024
PLE_DIM = 256
NORM_EPS = 1e-6

MLA_HEADS = 8
MLA_Q_RANK = 256
MLA_KV_RANK = 128
MLA_NOPE_DIM = 64
MLA_ROPE_DIM = 32
MLA_V_DIM = 64
MLA_OUT = MLA_HEADS * MLA_V_DIM
ROPE_THETA = 10000.0
HEAD_PAD = 128
ONES_ROWS = 16

SSD_HEADS = 8
SSD_HEAD_DIM = 64
SSD_GROUPS = 2
SSD_HEADS_PER_GROUP = SSD_HEADS // SSD_GROUPS
SSD_STATE = 64
SSD_CONV = 5
SSD_CHUNK = 128
SSD_INNER = SSD_HEADS * SSD_HEAD_DIM
SSD_XBC = SSD_INNER + 2 * SSD_GROUPS * SSD_STATE
SSD_DIRECTIONS = 2
CONV_HALO = 8

N_EXPERT_GROUPS = 4
EXPERTS_PER_GROUP = 8
N_EXPERTS = N_EXPERT_GROUPS * EXPERTS_PER_GROUP
D_EXPERT = 256
PAIRS_PER_GROUP = EXPERTS_PER_GROUP * (EXPERTS_PER_GROUP - 1) // 2
N_CLASSES = N_EXPERT_GROUPS * PAIRS_PER_GROUP

LANES = 128
ROW_TILES = D_MODEL // LANES
VMEM_LIMIT = 48 * 1024 * 1024

TOKEN_TILE = 512
ATTN_Q_TILE = 512
ATTN_Q_SUB = 256
ATTN_KV_CHUNK = 256
ATTN_AHEAD = 4
SSD_BLOCK = 256
MOE_TILE = 128


def _rms(x, w):
    ms = jnp.mean(x * x, axis=-1, keepdims=True)
    return x * lax.rsqrt(ms + NORM_EPS) * w


def _dot(a, b):
    return jnp.dot(a, b, preferred_element_type=F32)


def _dot_nt(a, b):
    return lax.dot_general(a, b, (((1,), (1,)), ((), ())), preferred_element_type=F32)


def _split3(x):
    x1 = x.astype(BF16)
    r1 = x - x1.astype(F32)
    x2 = r1.astype(BF16)
    x3 = (r1 - x2.astype(F32)).astype(BF16)
    return x1, x2, x3


def _sigmoid(x):
    return 1.0 / (1.0 + jnp.exp(-x))


def _softplus(x):
    return jnp.maximum(x, 0.0) + jnp.log(1.0 + jnp.exp(-jnp.abs(x)))


def _full(shape):
    nd = len(shape)
    return pl.BlockSpec(shape, lambda *_: (0,) * nd)


def _token_tile_spec(tm):
    return pl.BlockSpec((tm * ROW_TILES, LANES), lambda i: (i, 0))


def _store_token_tiles(ref, x):
    n = x.shape[0]
    for s in range(ROW_TILES):
        ref[pl.ds(s, n, stride=ROW_TILES), :] = x[:, LANES * s:LANES * (s + 1)]


def _load_token_tiles(ref):
    n = ref.shape[0] // ROW_TILES
    return jnp.concatenate([ref[pl.ds(s, n, stride=ROW_TILES), :] for s in range(ROW_TILES)], axis=1)


def _inproj_kernel(x_ref, nw_ref, wa_ref, wz_ref, wxbc_ref, wdt_ref, qnw_ref, wuq_ref, wuqs_ref,
                   kvnw_ref, wuk_ref, wuv_ref, cos_ref, sin_ref,
                   q_out, k_out, v_out, z_out, xbc_out, dt_out, *, scale):
    h = _rms(x_ref[...], nw_ref[...]).astype(BF16)
    pa = _dot(h, wa_ref[...])
    z_out[...] = _dot(h, wz_ref[...]).astype(z_out.dtype)
    xbc_out[...] = _dot(h, wxbc_ref[...]).astype(xbc_out.dtype)
    dt_out[...] = _dot(h, wdt_ref[...])
    cos = cos_ref[...]
    sin = sin_ref[...]
    o1 = MLA_Q_RANK
    o2 = o1 + MLA_KV_RANK
    cqn = _rms(pa[:, :o1], qnw_ref[...]).astype(BF16)
    ckvn = _rms(pa[:, o1:o2], kvnw_ref[...]).astype(BF16)
    q = _dot(cqn, wuq_ref[...])
    qs = _dot(cqn, wuqs_ref[...])
    kn = _dot(ckvn, wuk_ref[...])
    v_out[...] = _dot_nt(wuv_ref[...], ckvn).astype(v_out.dtype)
    kr = pa[:, o2:o2 + HEAD_PAD] * cos + pa[:, o2 + HEAD_PAD:o2 + 2 * HEAD_PAD] * sin
    for hh in range(MLA_HEADS):
        sl = slice(HEAD_PAD * hh, HEAD_PAD * (hh + 1))
        q_out[:, sl] = ((q[:, sl] * cos + qs[:, sl] * sin) * scale).astype(q_out.dtype)
        k_out[:, sl] = (kn[:, sl] + kr).astype(k_out.dtype)


def _inproj(x2, cos_t, sin_t, nw, wa, wz, wxbc, wdt, qnw, wuq, wuqs, kvnw, wuk, wuv):
    t = x2.shape[0]
    tm = min(TOKEN_TILE, t)
    scale = (MLA_NOPE_DIM + MLA_ROPE_DIM) ** -0.5 * math.log2(math.e)
    row = lambda w: pl.BlockSpec((tm, w), lambda i: (i, 0))
    weights = (nw, wa, wz, wxbc, wdt, qnw, wuq, wuqs, kvnw, wuk, wuv)
    return pl.pallas_call(
        functools.partial(_inproj_kernel, scale=scale),
        grid=(t // tm,),
        in_specs=[row(D_MODEL)] + [_full(w.shape) for w in weights] + [row(HEAD_PAD), row(HEAD_PAD)],
        out_specs=[row(MLA_HEADS * HEAD_PAD), row(MLA_HEADS * HEAD_PAD),
                   pl.BlockSpec((None, MLA_OUT, tm), lambda i: (i, 0, 0)),
                   row(SSD_INNER), row(SSD_XBC), row(LANES)],
        out_shape=[jax.ShapeDtypeStruct((t, MLA_HEADS * HEAD_PAD), BF16),
                   jax.ShapeDtypeStruct((t, MLA_HEADS * HEAD_PAD), BF16),
                   jax.ShapeDtypeStruct((t // tm, MLA_OUT, tm), BF16),
                   jax.ShapeDtypeStruct((t, SSD_INNER), BF16),
                   jax.ShapeDtypeStruct((t, SSD_XBC), F32),
                   jax.ShapeDtypeStruct((t, LANES), F32)],
        compiler_params=pltpu.CompilerParams(dimension_semantics=("parallel",),
                                             vmem_limit_bytes=VMEM_LIMIT),
        name="inproj",
    )(x2, *weights, cos_t, sin_t)


def _attn_body(q_ref, k_ref, vt_ref, o_ref, bounded):
    tq = q_ref.shape[0]
    nc, _, tv = vt_ref.shape
    tk = ATTN_KV_CHUNK
    ts = min(ATTN_Q_SUB, tq)
    ones = jnp.ones((ONES_ROWS, tk), BF16)
    streams = []
    for j in range(tq // ts):
        for a in range(2):
            hsl = slice(HEAD_PAD * a, HEAD_PAD * (a + 1))
            vsl = slice(MLA_V_DIM * a, MLA_V_DIM * (a + 1))
            streams.append([hsl, vsl, q_ref[j * ts:(j + 1) * ts, hsl],
                            jnp.full((1, ts), -1e30, F32),
                            jnp.zeros((MLA_V_DIM + ONES_ROWS, ts), F32)])
    units = [(c, i) for c in range(nc * tv // tk) for i in range(len(streams))]
    scores = {}

    def issue(u):
        c, i = units[u]
        scores[u] = _dot_nt(k_ref[c * tk:(c + 1) * tk, streams[i][0]], streams[i][2])

    for u in range(min(ATTN_AHEAD, len(units))):
        issue(u)
    for u, (c, i) in enumerate(units):
        if u + ATTN_AHEAD < len(units):
            issue(u + ATTN_AHEAD)
        st = streams[i]
        _, vsl, _, m, acc = st
        s = scores.pop(u)
        blk, off = divmod(c * tk, tv)
        vt = jnp.concatenate([vt_ref[blk, vsl, off:off + tk], ones], axis=0)
        if bounded:
            st[4] = acc + _dot(vt, jnp.exp2(s).astype(BF16))
        else:
            m_new = jnp.maximum(m, jnp.max(s, axis=0, keepdims=True))
            p = jnp.exp2((s - m_new).astype(BF16))
            st[3] = m_new
            st[4] = acc * jnp.exp2(m - m_new) + _dot(vt, p)
    for j in range(tq // ts):
        halves = [st[4][:MLA_V_DIM] / st[4][MLA_V_DIM:MLA_V_DIM + 1] for st in streams[2 * j:2 * j + 2]]
        o_ref[j * ts:(j + 1) * ts, :] = jnp.concatenate(halves, axis=0).T.astype(o_ref.dtype)


def _attn_kernel(q_ref, k_ref, vt_ref, o_ref):
    _attn_body(q_ref, k_ref, vt_ref, o_ref, False)


def _attention(q, k, vt, b, s):
    tq = min(ATTN_Q_TILE, s)
    nq = s // tq
    pairs = MLA_HEADS // 2
    tk = vt.shape[2]
    nc = s // tk
    return pl.pallas_call(
        _attn_kernel,
        grid=(b, pairs, nq),
        in_specs=[pl.BlockSpec((tq, 2 * HEAD_PAD), lambda bi, pi, qi: (bi * nq + qi, pi)),
                  pl.BlockSpec((s, 2 * HEAD_PAD), lambda bi, pi, qi: (bi, pi)),
                  pl.BlockSpec((nc, 2 * MLA_V_DIM, tk), lambda bi, pi, qi: (bi, pi, 0))],
        out_specs=pl.BlockSpec((tq, 2 * MLA_V_DIM), lambda bi, pi, qi: (bi * nq + qi, pi)),
        out_shape=jax.ShapeDtypeStruct((b * s, MLA_OUT), BF16),
        compiler_params=pltpu.CompilerParams(
            dimension_semantics=("parallel", "parallel", "parallel"),
            vmem_limit_bytes=VMEM_LIMIT),
        name="attention",
    )(q, k, vt)


def _ssd_conv(cur_ref, prev_ref, next_ref, cw_ref, cb_ref, tile_ref, has_prev, has_next):
    r = cur_ref.shape[0]
    tile_ref[0:CONV_HALO, :] = jnp.where(has_prev, prev_ref[...], 0.0)
    tile_ref[CONV_HALO:CONV_HALO + r, :] = cur_ref[...]
    tile_ref[CONV_HALO + r:2 * CONV_HALO + r, :] = jnp.where(has_next, next_ref[...], 0.0)
    acc = jnp.zeros((r, SSD_XBC), F32) + cb_ref[...]
    base = CONV_HALO - SSD_CONV // 2
    for kk in range(SSD_CONV):
        acc = acc + cw_ref[kk:kk + 1, :] * tile_ref[base + kk:base + kk + r, :]
    return acc * _sigmoid(acc)


def _ssd_chunk(act, dt_raw, direction, h_ref, dtbias, a_all, e_mat, skip):
    n = SSD_CHUNK
    gs = SSD_GROUPS * SSD_STATE
    xs = act[:, :SSD_INNER]
    bm = act[:, SSD_INNER:SSD_INNER + gs]
    cm = act[:, SSD_INNER + gs:SSD_INNER + 2 * gs]
    xs16 = xs.astype(BF16)
    bm16 = bm.astype(BF16)
    cm16 = cm.astype(BF16)

    dt_all = _softplus(dt_raw + dtbias)
    a_mat = dt_all * a_all
    ri = lax.broadcasted_iota(I32, (n, n), 0)
    ci = lax.broadcasted_iota(I32, (n, n), 1)
    mask = (ci <= ri) if direction == 0 else (ci >= ri)
    tri = jnp.where(mask, 1.0, 0.0).astype(BF16)
    a1, a2, a3 = _split3(a_mat)
    cs = _dot(tri, a1) + _dot(tri, a2) + _dot(tri, a3)
    end = n - 1 if direction == 0 else 0
    cs_end = cs[end:end + 1, :]
    w_state = dt_all * jnp.exp(cs_end - cs)
    e_off = jnp.exp(cs)
    c_dec = jnp.broadcast_to(jnp.exp(cs_end), (8, LANES))
    stack = jnp.concatenate([w_state, e_off, c_dec], axis=0).astype(BF16)
    expd = _dot(stack, e_mat)
    ws_x = expd[0:n]
    eo_x = expd[n:2 * n]
    cd_x = expd[2 * n:2 * n + 1]

    cs_t = cs.T
    dt_t = dt_all.T
    bm_t = bm.T.astype(BF16)

    pieces = []
    for g in range(SSD_GROUPS):
        gsl = slice(SSD_STATE * g, SSD_STATE * (g + 1))
        cg = cm16[:, gsl]
        gmat = _dot_nt(cg, bm16[:, gsl])
        for r in range(SSD_HEADS_PER_GROUP):
            hh = g * SSD_HEADS_PER_GROUP + r
            c = direction * SSD_HEADS + hh
            seg = cs[:, c:c + 1] - cs_t[c:c + 1, :]
            lm = jnp.where(mask, jnp.exp(jnp.where(mask, seg, 0.0)), 0.0) * dt_t[c:c + 1, :]
            mh = (gmat * lm).astype(BF16)
            pieces.append(_dot(mh, xs16[:, SSD_HEAD_DIM * hh:SSD_HEAD_DIM * (hh + 1)]))
    y = jnp.concatenate(pieces, axis=1)

    w = SSD_HEADS_PER_GROUP * SSD_HEAD_DIM
    offs = []
    for g in range(SSD_GROUPS):
        lsl = slice(w * g, w * (g + 1))
        gsl = slice(SSD_STATE * g, SSD_STATE * (g + 1))
        h_g = h_ref[:, lsl]
        offs.append(_dot(cm16[:, gsl], h_g.astype(BF16)) * eo_x[:, lsl])
        xd = (xs[:, lsl] * ws_x[:, lsl]).astype(BF16)
        h_ref[:, lsl] = h_g * cd_x[:, lsl] + _dot(bm_t[gsl, :], xd)
    y = y + jnp.concatenate(offs, axis=1)
    if skip is not None:
        y = y + xs * skip
    return y


def _ssd_kernel(xf_ref, xfp_ref, xfn_ref, dtf_ref, xb_ref, xbp_ref, xbn_ref, dtb_ref,
                cw_ref, cb_ref, dtbias_ref, alog_ref, skip_ref, e_ref,
                yf_ref, yb_ref, hf_ref, hb_ref, tile_ref):
    i = pl.program_id(1)
    nb = pl.num_programs(1)

    @pl.when(i == 0)
    def _():
        hf_ref[...] = jnp.zeros_like(hf_ref)
        hb_ref[...] = jnp.zeros_like(hb_ref)

    lane = lax.broadcasted_iota(I32, (1, LANES), 1)
    a_all = jnp.where(lane < SSD_DIRECTIONS * SSD_HEADS, -jnp.exp(alog_ref[...]), 0.0)
    dtbias = dtbias_ref[...]
    skip = skip_ref[...]
    nch = xf_ref.shape[0] // SSD_CHUNK

    act = _ssd_conv(xf_ref, xfp_ref, xfn_ref, cw_ref, cb_ref, tile_ref, i > 0, i < nb - 1)
    for c in range(nch):
        rows = slice(SSD_CHUNK * c, SSD_CHUNK * (c + 1))
        y = _ssd_chunk(act[rows], dtf_ref[rows, :], 0, hf_ref, dtbias, a_all, e_ref[0], skip)
        yf_ref[rows, :] = y.astype(yf_ref.dtype)

    act = _ssd_conv(xb_ref, xbp_ref, xbn_ref, cw_ref, cb_ref, tile_ref, i < nb - 1, i > 0)
    for c in reversed(range(nch)):
        rows = slice(SSD_CHUNK * c, SSD_CHUNK * (c + 1))
        y = _ssd_chunk(act[rows], dtb_ref[rows, :], 1, hb_ref, dtbias, a_all, e_ref[1], None)
        yb_ref[rows, :] = y.astype(yb_ref.dtype)


def _ssd(xbc, dt_raw, cw, cb, dtbias, alog, skip, e_mat, b, s):
    r = min(SSD_BLOCK, s)
    nb = s // r
    hb = r // CONV_HALO

    def cur(rev):
        return (lambda bi, i: (bi * nb + (nb - 1 - i), 0)) if rev else (lambda bi, i: (bi * nb + i, 0))

    def prev(rev):
        def f(bi, i):
            j = (nb - 1 - i) if rev else i
            return ((bi * nb + j) * hb - jnp.where(j > 0, 1, 0), 0)
        return f

    def nxt(rev):
        def f(bi, i):
            j = (nb - 1 - i) if rev else i
            return ((bi * nb + j) * hb + jnp.where(j < nb - 1, hb, 0), 0)
        return f

    def role(rev):
        return [pl.BlockSpec((r, SSD_XBC), cur(rev)),
                pl.BlockSpec((CONV_HALO, SSD_XBC), prev(rev)),
                pl.BlockSpec((CONV_HALO, SSD_XBC), nxt(rev)),
                pl.BlockSpec((r, LANES), cur(rev))]

    consts = (cw, cb, dtbias, alog, skip, e_mat)
    return pl.pallas_call(
        _ssd_kernel,
        grid=(b, nb),
        in_specs=role(False) + role(True) + [_full(c.shape) for c in consts],
        out_specs=[pl.BlockSpec((r, SSD_INNER), cur(False)), pl.BlockSpec((r, SSD_INNER), cur(True))],
        out_shape=[jax.ShapeDtypeStruct((b * s, SSD_INNER), F32)] * 2,
        scratch_shapes=[pltpu.VMEM((SSD_STATE, SSD_INNER), F32),
                        pltpu.VMEM((SSD_STATE, SSD_INNER), F32),
                        pltpu.VMEM((r + 2 * CONV_HALO, SSD_XBC), F32)],
        compiler_params=pltpu.CompilerParams(dimension_semantics=("parallel", "arbitrary"),
                                             vmem_limit_bytes=VMEM_LIMIT),
        name="ssd",
    )(xbc, xbc, xbc, dt_raw, xbc, xbc, xbc, dt_raw, *consts)


def _mixout_kernel(x_ref, attn_ref, yf_ref, yb_ref, z_ref, anw_ref, snw_ref, wo_ref, fnw_ref,
                   wr_ref, br_ref, x1_out, route_out):
    attn = _rms(attn_ref[...].astype(F32), anw_ref[...])
    z = z_ref[...].astype(F32)
    y = (yf_ref[...] + yb_ref[...]) * (z * _sigmoid(z))
    y = _rms(y, snw_ref[...])
    mix = jnp.concatenate([attn, y], axis=1).astype(BF16)
    x1 = x_ref[...] + _dot(mix, wo_ref[...])
    _store_token_tiles(x1_out, x1)

    h = _rms(x1, fnw_ref[...])
    h1, h2, h3 = _split3(h)
    w1 = wr_ref[0]
    w2 = wr_ref[1]
    w3 = wr_ref[2]
    logits = (_dot(h1, w1) + (_dot(h1, w2) + _dot(h2, w1))
              + (_dot(h1, w3) + _dot(h2, w2) + _dot(h3, w1))) + br_ref[...]

    lane = lax.broadcasted_iota(I32, logits.shape, 1)
    ninf = -jnp.inf
    big = 4 * LANES

    def argmax_first(vals):
        vmax = jnp.max(vals, axis=-1, keepdims=True)
        idx = jnp.min(jnp.where(vals == vmax, lane, big), axis=-1, keepdims=True)
        return vmax, idx

    gl = jnp.where(lane < N_EXPERT_GROUPS, logits, ninf)
    gmax, gidx = argmax_first(gl)
    gweight = 1.0 / jnp.sum(jnp.exp(gl - gmax), axis=-1, keepdims=True)
    lo_lane = N_EXPERT_GROUPS + EXPERTS_PER_GROUP * gidx
    sl = jnp.where((lane >= lo_lane) & (lane < lo_lane + EXPERTS_PER_GROUP), logits, ninf)
    v1, i1 = argmax_first(sl)
    v2, i2 = argmax_first(jnp.where(lane == i1, ninf, sl))
    e21 = jnp.exp(v2 - v1)
    wt1 = gweight / (1.0 + e21)
    wt2 = gweight * e21 / (1.0 + e21)
    a = i1 - lo_lane
    bb = i2 - lo_lane
    swap = a > bb
    elo = jnp.where(swap, bb, a)
    ehi = jnp.where(swap, a, bb)
    wlo = jnp.where(swap, wt2, wt1)
    whi = jnp.where(swap, wt1, wt2)
    pair = (elo * (2 * EXPERTS_PER_GROUP - 1 - elo)) // 2 + (ehi - elo - 1)
    cls = gidx * PAIRS_PER_GROUP + pair
    out = jnp.where(lane == 0, cls.astype(F32), 0.0)
    out = jnp.where(lane == 1, wlo, out)
    out = jnp.where(lane == 2, whi, out)
    route_out[...] = out


def _mixout(x2, attn, yf, yb, z, anw, snw, wo, fnw, wr, br):
    t = x2.shape[0]
    tm = min(TOKEN_TILE, t)
    row = lambda w: pl.BlockSpec((tm, w), lambda i: (i, 0))
    weights = (anw, snw, wo, fnw, wr, br)
    return pl.pallas_call(
        _mixout_kernel,
        grid=(t // tm,),
        in_specs=[row(D_MODEL), row(MLA_OUT), row(SSD_INNER), row(SSD_INNER), row(SSD_INNER)]
        + [_full(w.shape) for w in weights],
        out_specs=[_token_tile_spec(tm), row(LANES)],
        out_shape=[jax.ShapeDtypeStruct((t * ROW_TILES, LANES), F32),
                   jax.ShapeDtypeStruct((t, LANES), F32)],
        compiler_params=pltpu.CompilerParams(dimension_semantics=("parallel",),
                                             vmem_limit_bytes=VMEM_LIMIT),
        name="mixout",
    )(x2, attn, yf, yb, z, *weights)


def _moe_kernel(tile_lo_ref, tile_hi_ref, tile_j_ref, tile_n_ref, order_ref,
                x1_hbm, wts_ref, fnw_ref, wgu_lo_ref, wdn_lo_ref, wgu_hi_ref, wdn_hi_ref,
                y_hbm, xbuf, obuf, sem_in, sem_out):
    del tile_lo_ref, tile_hi_ref
    t = pl.program_id(0)
    nt = pl.num_programs(0)
    tile = xbuf.shape[1] // ROW_TILES
    slot = t % 2

    def token_rows(r):
        return pl.ds(pl.multiple_of(r * ROW_TILES, ROW_TILES), ROW_TILES)

    def gather(tt, sl, wait):
        j0 = tile_j_ref[tt]
        if wait:
            pltpu.make_async_copy(xbuf.at[sl], xbuf.at[sl], sem_in.at[sl]).wait()
            return

        def body(r, c):
            tok = order_ref[j0 + r]
            pltpu.make_async_copy(x1_hbm.at[tok], xbuf.at[sl, token_rows(r)], sem_in.at[sl]).start()
            return c
        lax.fori_loop(0, tile, body, 0, unroll=8)

    def scatter_rows(tt, sl, r0, count):
        j0 = tile_j_ref[tt]
        for k in range(count):
            tok = order_ref[j0 + r0 + k]
            pltpu.make_async_copy(obuf.at[sl, token_rows(r0 + k)], y_hbm.at[tok],
                                  sem_out.at[sl]).start()

    def scatter(tt, sl):
        n = tile_n_ref[tt]
        groups = n // 8

        def body(g, c):
            scatter_rows(tt, sl, g * 8, 8)
            return c
        lax.fori_loop(0, groups, body, 0)
        done = groups * 8
        for part in (4, 2, 1):
            take = (n & part) > 0

            @pl.when(take)
            def _(done=done, part=part):
                scatter_rows(tt, sl, done, part)
            done = done + jnp.where(take, part, 0)

    def scatter_wait(tt, sl):
        rows = pl.ds(0, pl.multiple_of(tile_n_ref[tt] * ROW_TILES, ROW_TILES))
        pltpu.make_async_copy(obuf.at[sl, rows], obuf.at[sl, rows], sem_out.at[sl]).wait()

    valid = tile_n_ref[t] > 0
    nxt = jnp.minimum(t + 1, nt - 1)
    next_valid = (t + 1 < nt) & (tile_n_ref[nxt] > 0)

    @pl.when(valid & (t == 0))
    def _():
        gather(t, slot, wait=False)

    @pl.when(valid)
    def _():
        gather(t, slot, wait=True)

        @pl.when(next_valid)
        def _():
            gather(nxt, 1 - slot, wait=False)

        h = _rms(_load_token_tiles(xbuf.at[slot]), fnw_ref[...]).astype(BF16)
        wts = wts_ref[...]
        acc = None
        for half, wgu_ref, wdn_ref in ((0, wgu_lo_ref, wdn_lo_ref), (1, wgu_hi_ref, wdn_hi_ref)):
            gu = _dot(h, wgu_ref[...])
            g = gu[:, :D_EXPERT]
            he = (g * _sigmoid(g) * gu[:, D_EXPERT:] * wts[:, half:half + 1]).astype(BF16)
            d = _dot(he, wdn_ref[...])
            acc = d if acc is None else acc + d
        _store_token_tiles(obuf.at[slot], acc)
        scatter(t, slot)

        @pl.when(t > 0)
        def _():
            scatter_wait(t - 1, 1 - slot)

        @pl.when(jnp.logical_not(next_valid))
        def _():
            scatter_wait(t, slot)


def _moe(x1t, wts, order, tile_lo, tile_hi, tile_j, tile_n, fnw, wgu, wdn):
    t = x1t.shape[0]
    n_tiles = tile_lo.shape[0]
    tile = MOE_TILE
    wspec = lambda shape, which: pl.BlockSpec(
        (None,) + shape, lambda i, lo, hi, tj, tn, od: ((lo, hi)[which][i], 0, 0))
    grid_spec = pltpu.PrefetchScalarGridSpec(
        num_scalar_prefetch=5,
        grid=(n_tiles,),
        in_specs=[pl.BlockSpec(memory_space=pl.ANY),
                  pl.BlockSpec((None, tile, 2), lambda i, *_: (i, 0, 0)),
                  pl.BlockSpec((1, D_MODEL), lambda i, *_: (0, 0)),
                  wspec((D_MODEL, 2 * D_EXPERT), 0), wspec((D_EXPERT, D_MODEL), 0),
                  wspec((D_MODEL, 2 * D_EXPERT), 1), wspec((D_EXPERT, D_MODEL), 1)],
        out_specs=pl.BlockSpec(memory_space=pl.ANY),
        scratch_shapes=[pltpu.VMEM((2, tile * ROW_TILES, LANES), F32),
                        pltpu.VMEM((2, tile * ROW_TILES, LANES), F32),
                        pltpu.SemaphoreType.DMA((2,)), pltpu.SemaphoreType.DMA((2,))],
    )
    return pl.pallas_call(
        _moe_kernel,
        grid_spec=grid_spec,
        out_shape=jax.ShapeDtypeStruct((t, ROW_TILES, LANES), F32),
        compiler_params=pltpu.CompilerParams(dimension_semantics=("arbitrary",),
                                             vmem_limit_bytes=VMEM_LIMIT),
        name="moe",
    )(tile_lo, tile_hi, tile_j, tile_n, order, x1t, wts, fnw, wgu, wdn, wgu, wdn)


def _moe_plan(route, t):
    tile = MOE_TILE
    n_tiles = t // tile + N_CLASSES
    cls = route[:, 0].astype(I32)
    _, order, wlo, whi = lax.sort((cls, jnp.arange(t, dtype=I32), route[:, 1], route[:, 2]),
                                  num_keys=1, is_stable=True)
    order = jnp.concatenate([order, jnp.zeros((tile,), I32)])
    wts_sorted = jnp.concatenate([jnp.stack([wlo, whi], axis=1), jnp.zeros((tile, 2), F32)])
    class_ids = jnp.arange(N_CLASSES, dtype=I32)
    counts = jnp.sum((cls[:, None] == class_ids[None, :]).astype(I32), axis=0)
    tiles_per = (counts + tile - 1) // tile
    tile_end = jnp.cumsum(tiles_per)
    tile_begin = tile_end - tiles_per
    starts = jnp.cumsum(counts) - counts
    t_idx = jnp.arange(n_tiles, dtype=I32)
    tile_cls = jnp.sum((tile_end[None, :] <= t_idx[:, None]).astype(I32), axis=1)
    valid = t_idx < tile_end[-1]
    last_cls = jnp.max(jnp.where(counts > 0, class_ids, 0))
    tile_cls = jnp.where(valid, tile_cls, last_cls)
    onehot = (tile_cls[:, None] == class_ids[None, :]).astype(I32)
    pick = lambda v: jnp.sum(onehot * v[None, :], axis=1)
    k = t_idx - pick(tile_begin)
    tile_j = jnp.where(valid, pick(starts) + k * tile, 0)
    tile_n = jnp.where(valid, jnp.clip(pick(counts) - k * tile, 0, tile), 0)
    lo_of_pair, hi_of_pair = [], []
    for lo in range(EXPERTS_PER_GROUP):
        for hi in range(lo + 1, EXPERTS_PER_GROUP):
            lo_of_pair.append(lo)
            hi_of_pair.append(hi)
    grp = class_ids // PAIRS_PER_GROUP
    class_lo = grp * EXPERTS_PER_GROUP + jnp.asarray(lo_of_pair * N_EXPERT_GROUPS, I32)
    class_hi = grp * EXPERTS_PER_GROUP + jnp.asarray(hi_of_pair * N_EXPERT_GROUPS, I32)
    tile_j = tile_j.astype(I32)
    wts = jax.vmap(lambda j: lax.dynamic_slice(wts_sorted, (j, 0), (tile, 2)))(tile_j)
    return order, wts, pick(class_lo), pick(class_hi), tile_j, tile_n.astype(I32)


def _final_kernel(x1_ref, ym_ref, p_ref, pnw_ref, wg_ref, bg_ref, wp_ref, ppnw_ref, fnw_ref, o_ref):
    x2 = _load_token_tiles(x1_ref) + _load_token_tiles(ym_ref)
    gate = _sigmoid(_dot(_rms(x2, pnw_ref[...]).astype(BF16), wg_ref[...]) + bg_ref[...])
    ple = _rms(_dot(p_ref[...].astype(BF16), wp_ref[...]), ppnw_ref[...])
    o_ref[...] = _rms(x2 + gate * ple, fnw_ref[...])


def _final(x1, ym, p2, pnw, wg, bg, wp, ppnw, fnw):
    t = p2.shape[0]
    tm = min(TOKEN_TILE, t)
    row = lambda w: pl.BlockSpec((tm, w), lambda i: (i, 0))
    weights = (pnw, wg, bg, wp, ppnw, fnw)
    return pl.pallas_call(
        _final_kernel,
        grid=(t // tm,),
        in_specs=[_token_tile_spec(tm), _token_tile_spec(tm), row(PLE_DIM)]
        + [_full(w.shape) for w in weights],
        out_specs=row(D_MODEL),
        out_shape=jax.ShapeDtypeStruct((t, D_MODEL), F32),
        compiler_params=pltpu.CompilerParams(dimension_semantics=("parallel",),
                                             vmem_limit_bytes=VMEM_LIMIT),
        name="final",
    )(x1, ym, p2, *weights)


def _pad_cols(w, width, offset=0):
    out = jnp.zeros((w.shape[0], width), w.dtype)
    return out.at[:, offset:offset + w.shape[1]].set(w)


def _rope_swap(w):
    half = MLA_ROPE_DIM // 2
    return jnp.concatenate([-w[..., half:], w[..., :half]], axis=-1)


def _layer(x2, p2, cos_t, sin_t, b, s, attn_norm_w, w_in, q_norm_w, w_uq, kv_norm_w, w_ukv,
           attn_out_norm_w, conv_w, conv_b, dt_bias, a_log, ssd_d, ssd_norm_w, w_o, ffn_norm_w,
           w_router_group, b_router_group, w_router_expert, b_router_expert, w_exp_gate,
           w_exp_up, w_exp_down, ple_norm_w, w_ple_gate, b_ple_gate, w_ple_proj, ple_post_norm_w):
    t = x2.shape[0]
    r1 = lambda v: v.reshape(1, -1).astype(F32)
    o1 = MLA_Q_RANK
    o2 = o1 + MLA_KV_RANK
    o3 = o2 + MLA_ROPE_DIM
    o4 = o3 + SSD_INNER
    o5 = o4 + SSD_XBC
    w_kr = w_in[:, o2:o3]
    wa = jnp.concatenate([w_in[:, :o2], _pad_cols(w_kr, HEAD_PAD, MLA_NOPE_DIM),
                          _pad_cols(_rope_swap(w_kr), HEAD_PAD, MLA_NOPE_DIM)], axis=1).astype(BF16)
    wz = w_in[:, o3:o4].astype(BF16)
    wxbc = w_in[:, o4:o5].astype(BF16)
    wdt = _pad_cols(w_in[:, o5:], LANES).astype(BF16)
    uq = w_uq.reshape(MLA_Q_RANK, MLA_HEADS, MLA_NOPE_DIM + MLA_ROPE_DIM)
    zq = jnp.zeros((MLA_Q_RANK, MLA_HEADS, HEAD_PAD - MLA_NOPE_DIM - MLA_ROPE_DIM), F32)
    wuq = jnp.concatenate([uq, zq], axis=-1).reshape(MLA_Q_RANK, -1).astype(BF16)
    wuqs = jnp.concatenate([jnp.zeros_like(uq[..., :MLA_NOPE_DIM]), _rope_swap(uq[..., MLA_NOPE_DIM:]),
                            zq], axis=-1).reshape(MLA_Q_RANK, -1).astype(BF16)
    ukv = w_ukv.reshape(MLA_KV_RANK, MLA_HEADS, MLA_NOPE_DIM + MLA_V_DIM)
    zk = jnp.zeros((MLA_KV_RANK, MLA_HEADS, HEAD_PAD - MLA_NOPE_DIM), F32)
    wuk = jnp.concatenate([ukv[..., :MLA_NOPE_DIM], zk], axis=-1).reshape(MLA_KV_RANK, -1).astype(BF16)
    wuv = ukv[..., MLA_NOPE_DIM:].reshape(MLA_KV_RANK, -1).T.astype(BF16)

    q, k, v, z, xbc, dt_raw = _inproj(x2, cos_t, sin_t, r1(attn_norm_w), wa, wz, wxbc, wdt,
                                      r1(q_norm_w), wuq, wuqs, r1(kv_norm_w), wuk, wuv)
    attn = _attention(q, k, v, b, s)

    head_of_lane = jnp.arange(SSD_INNER) // SSD_HEAD_DIM
    rows = jnp.arange(LANES)[:, None]
    e_mat = jnp.stack([(rows == d * SSD_HEADS + head_of_lane[None, :]) for d in range(SSD_DIRECTIONS)]
                      ).astype(BF16)
    skip = jnp.repeat(ssd_d.astype(F32), SSD_HEAD_DIM).reshape(1, -1)
    yf, yb = _ssd(xbc, dt_raw, conv_w.astype(F32), r1(conv_b), _pad_cols(r1(dt_bias), LANES),
                  _pad_cols(r1(a_log), LANES), skip, e_mat, b, s)

    wr = _pad_cols(jnp.concatenate([w_router_group, w_router_expert], axis=1).astype(F32), LANES)
    wr3 = jnp.stack(_split3(wr))
    br = _pad_cols(jnp.concatenate([r1(b_router_group), r1(b_router_expert)], axis=1), LANES)
    x1, route = _mixout(x2, attn, yf, yb, z, r1(attn_out_norm_w), r1(ssd_norm_w), w_o.astype(BF16),
                        r1(ffn_norm_w), wr3, br)

    order, wts, tile_lo, tile_hi, tile_j, tile_n = _moe_plan(route, t)
    wgu = jnp.concatenate([w_exp_gate, w_exp_up], axis=-1).astype(BF16)
    ym = _moe(x1.reshape(t, ROW_TILES, LANES), wts, order, tile_lo, tile_hi, tile_j, tile_n,
              r1(ffn_norm_w), wgu, w_exp_down.astype(BF16)).reshape(t * ROW_TILES, LANES)
    return x1, ym, (r1(ple_norm_w), w_ple_gate.astype(BF16), r1(b_ple_gate), w_ple_proj.astype(BF16),
                    r1(ple_post_norm_w))


def kernel(x, p, positions, attn_norm_w, w_in, q_norm_w, w_uq, kv_norm_w, w_ukv, attn_out_norm_w, conv_w, conv_b, dt_bias, a_log, ssd_d, ssd_norm_w, w_o, ffn_norm_w, w_router_group, b_router_group, w_router_expert, b_router_expert, w_exp_gate, w_exp_up, w_exp_down, ple_norm_w, w_ple_gate, b_ple_gate, w_ple_proj, ple_post_norm_w, final_norm_w):
    b, s, d = x.shape
    depth = p.shape[0]
    assert depth == 1, "the fused final stage assumes a single layer"
    t = b * s
    inv_freq = 1.0 / (ROPE_THETA ** (jnp.arange(0, MLA_ROPE_DIM, 2, dtype=F32) / MLA_ROPE_DIM))
    half = MLA_ROPE_DIM // 2
    ang = (positions.astype(F32).reshape(t, 1) * inv_freq).reshape(t * half // LANES, LANES)
    cos, sin = lax.optimization_barrier((jnp.cos(ang), jnp.sin(ang)))
    cos = cos.reshape(t, half)
    sin = sin.reshape(t, half)
    ones = jnp.ones((t, MLA_NOPE_DIM), F32)
    zeros = jnp.zeros((t, HEAD_PAD - MLA_NOPE_DIM - MLA_ROPE_DIM), F32)
    cos_t = jnp.concatenate([ones, cos, cos, zeros], axis=1)
    sin_t = jnp.concatenate([0.0 * ones, sin, sin, zeros], axis=1)

    x2 = x.reshape(t, d)
    i = 0
    x1, ym, (pnw, wg, bg, wp, ppnw) = _layer(
        x2, p[i].reshape(t, -1), cos_t, sin_t, b, s, attn_norm_w[i], w_in[i], q_norm_w[i], w_uq[i],
        kv_norm_w[i], w_ukv[i], attn_out_norm_w[i], conv_w[i], conv_b[i], dt_bias[i], a_log[i],
        ssd_d[i], ssd_norm_w[i], w_o[i], ffn_norm_w[i], w_router_group[i], b_router_group[i],
        w_router_expert[i], b_router_expert[i], w_exp_gate[i], w_exp_up[i], w_exp_down[i],
        ple_norm_w[i], w_ple_gate[i], b_ple_gate[i], w_ple_proj[i], ple_post_norm_w[i])
    out = _final(x1, ym, p[i].reshape(t, -1), pnw, wg, bg, wp, ppnw, final_norm_w.reshape(1, -1).astype(F32))
    return out.reshape(b, s, d)
```

```python
import functools
import math

import jax
import jax.numpy as jnp
from jax import lax
from jax.experimental import pallas as pl
from jax.experimental.pallas import tpu as pltpu

F32 = jnp.float32
BF16 = jnp.bfloat16
I32 = jnp.int32

D_MODEL = 1024
PLE_DIM = 256
NORM_EPS = 1e-6

MLA_HEADS = 8
MLA_Q_RANK = 256
MLA_KV_RANK = 128
MLA_NOPE_DIM = 64
MLA_ROPE_DIM = 32
MLA_V_DIM = 64
MLA_OUT = MLA_HEADS * MLA_V_DIM
ROPE_THETA = 10000.0
HEAD_PAD = 128
ONES_ROWS = 16

SSD_HEADS = 8
SSD_HEAD_DIM = 64
SSD_GROUPS = 2
SSD_HEADS_PER_GROUP = SSD_HEADS // SSD_GROUPS
SSD_STATE = 64
SSD_CONV = 5
SSD_CHUNK = 128
SSD_INNER = SSD_HEADS * SSD_HEAD_DIM
SSD_XBC = SSD_INNER + 2 * SSD_GROUPS * SSD_STATE
SSD_DIRECTIONS = 2
CONV_HALO = 8

N_EXPERT_GROUPS = 4
EXPERTS_PER_GROUP = 8
N_EXPERTS = N_EXPERT_GROUPS * EXPERTS_PER_GROUP
D_EXPERT = 256
PAIRS_PER_GROUP = EXPERTS_PER_GROUP * (EXPERTS_PER_GROUP - 1) // 2
N_CLASSES = N_EXPERT_GROUPS * PAIRS_PER_GROUP

LANES = 128
ROW_TILES = D_MODEL // LANES
VMEM_LIMIT = 48 * 1024 * 1024

TOKEN_TILE = 512
ATTN_Q_TILE = 512
ATTN_Q_SUB = 256
ATTN_KV_CHUNK = 256
ATTN_AHEAD = 4
SSD_BLOCK = 256
MOE_TILE = 128


def _rms(x, w):
    ms = jnp.mean(x * x, axis=-1, keepdims=True)
    return x * lax.rsqrt(ms + NORM_EPS) * w


def _dot(a, b):
    return jnp.dot(a, b, preferred_element_type=F32)


def _dot_nt(a, b):
    return lax.dot_general(a, b, (((1,), (1,)), ((), ())), preferred_element_type=F32)


def _split3(x):
    x1 = x.astype(BF16)
    r1 = x - x1.astype(F32)
    x2 = r1.astype(BF16)
    x3 = (r1 - x2.astype(F32)).astype(BF16)
    return x1, x2, x3


def _sigmoid(x):
    return 1.0 / (1.0 + jnp.exp(-x))


def _softplus(x):
    return jnp.maximum(x, 0.0) + jnp.log(1.0 + jnp.exp(-jnp.abs(x)))


def _full(shape):
    nd = len(shape)
    return pl.BlockSpec(shape, lambda *_: (0,) * nd)


def _token_tile_spec(tm):
    return pl.BlockSpec((tm * ROW_TILES, LANES), lambda i: (i, 0))


def _store_token_tiles(ref, x):
    n = x.shape[0]
    for s in range(ROW_TILES):
        ref[pl.ds(s, n, stride=ROW_TILES), :] = x[:, LANES * s:LANES * (s + 1)]


def _load_token_tiles(ref):
    n = ref.shape[0] // ROW_TILES
    return jnp.concatenate([ref[pl.ds(s, n, stride=ROW_TILES), :] for s in range(ROW_TILES)], axis=1)


def _inproj_kernel(x_ref, nw_ref, wa_ref, wz_ref, wxbc_ref, qnw_ref, wuq_ref, wuqs_ref,
                   kvnw_ref, wuk_ref, wuv_ref, cos_ref, sin_ref,
                   q_out, k_out, v_out, z_out, xbc_out, dt_out, *, scale):
    h = _rms(x_ref[...], nw_ref[...]).astype(BF16)
    pa = _dot(h, wa_ref[...])
    z_out[...] = _dot(h, wz_ref[...]).astype(z_out.dtype)
    xbc_out[...] = _dot(h, wxbc_ref[...]).astype(xbc_out.dtype)
    dt_out[...] = pa[:, MLA_Q_RANK + MLA_KV_RANK + 2 * HEAD_PAD:]
    cos = cos_ref[...]
    sin = sin_ref[...]
    o1 = MLA_Q_RANK
    o2 = o1 + MLA_KV_RANK
    cqn = _rms(pa[:, :o1], qnw_ref[...]).astype(BF16)
    ckvn = _rms(pa[:, o1:o2], kvnw_ref[...]).astype(BF16)
    q = _dot(cqn, wuq_ref[...])
    qs = _dot(cqn, wuqs_ref[...])
    kn = _dot(ckvn, wuk_ref[...])
    v_out[...] = _dot_nt(wuv_ref[...], ckvn).astype(v_out.dtype)
    kr = pa[:, o2:o2 + HEAD_PAD] * cos + pa[:, o2 + HEAD_PAD:o2 + 2 * HEAD_PAD] * sin
    for hh in range(MLA_HEADS):
        sl = slice(HEAD_PAD * hh, HEAD_PAD * (hh + 1))
        q_out[:, sl] = ((q[:, sl] * cos + qs[:, sl] * sin) * scale).astype(q_out.dtype)
        k_out[:, sl] = (kn[:, sl] + kr).astype(k_out.dtype)


def _inproj(x2, cos_t, sin_t, nw, wa, wz, wxbc, qnw, wuq, wuqs, kvnw, wuk, wuv):
    t = x2.shape[0]
    tm = min(TOKEN_TILE, t)
    scale = (MLA_NOPE_DIM + MLA_ROPE_DIM) ** -0.5 * math.log2(math.e)
    row = lambda w: pl.BlockSpec((tm, w), lambda i: (i, 0))
    weights = (nw, wa, wz, wxbc, qnw, wuq, wuqs, kvnw, wuk, wuv)
    return pl.pallas_call(
        functools.partial(_inproj_kernel, scale=scale),
        grid=(t // tm,),
        in_specs=[row(D_MODEL)] + [_full(w.shape) for w in weights] + [row(HEAD_PAD), row(HEAD_PAD)],
        out_specs=[row(MLA_HEADS * HEAD_PAD), row(MLA_HEADS * HEAD_PAD),
                   pl.BlockSpec((None, MLA_OUT, tm), lambda i: (i, 0, 0)),
                   row(SSD_INNER), row(SSD_XBC), row(LANES)],
        out_shape=[jax.ShapeDtypeStruct((t, MLA_HEADS * HEAD_PAD), BF16),
                   jax.ShapeDtypeStruct((t, MLA_HEADS * HEAD_PAD), BF16),
                   jax.ShapeDtypeStruct((t // tm, MLA_OUT, tm), BF16),
                   jax.ShapeDtypeStruct((t, SSD_INNER), BF16),
                   jax.ShapeDtypeStruct((t, SSD_XBC), F32),
                   jax.ShapeDtypeStruct((t, LANES), F32)],
        compiler_params=pltpu.CompilerParams(dimension_semantics=("parallel",),
                                             vmem_limit_bytes=VMEM_LIMIT),
        name="inproj",
    )(x2, *weights, cos_t, sin_t)


def _attn_body(q_ref, k_ref, vt_ref, o_ref, bounded):
    tq = q_ref.shape[0]
    nc, _, tv = vt_ref.shape
    tk = ATTN_KV_CHUNK
    ts = min(ATTN_Q_SUB, tq)
    ones = jnp.ones((ONES_ROWS, tk), BF16)
    streams = []
    for j in range(tq // ts):
        for a in range(2):
            hsl = slice(HEAD_PAD * a, HEAD_PAD * (a + 1))
            vsl = slice(MLA_V_DIM * a, MLA_V_DIM * (a + 1))
            streams.append([hsl, vsl, q_ref[j * ts:(j + 1) * ts, hsl],
                            jnp.full((1, ts), -1e30, F32),
                            jnp.zeros((MLA_V_DIM + ONES_ROWS, ts), F32)])
    units = [(c, i) for c in range(nc * tv // tk) for i in range(len(streams))]
    scores = {}

    def issue(u):
        c, i = units[u]
        scores[u] = _dot_nt(k_ref[c * tk:(c + 1) * tk, streams[i][0]], streams[i][2])

    for u in range(min(ATTN_AHEAD, len(units))):
        issue(u)
    for u, (c, i) in enumerate(units):
        if u + ATTN_AHEAD < len(units):
            issue(u + ATTN_AHEAD)
        st = streams[i]
        _, vsl, _, m, acc = st
        s = scores.pop(u)
        blk, off = divmod(c * tk, tv)
        vt = jnp.concatenate([vt_ref[blk, vsl, off:off + tk], ones], axis=0)
        if bounded:
            st[4] = acc + _dot(vt, jnp.exp2(s).astype(BF16))
        else:
            m_new = jnp.maximum(m, jnp.max(s, axis=0, keepdims=True))
            p = jnp.exp2((s - m_new).astype(BF16))
            st[3] = m_new
            st[4] = acc * jnp.exp2(m - m_new) + _dot(vt, p)
    for j in range(tq // ts):
        halves = [st[4][:MLA_V_DIM] / st[4][MLA_V_DIM:MLA_V_DIM + 1] for st in streams[2 * j:2 * j + 2]]
        o_ref[j * ts:(j + 1) * ts, :] = jnp.concatenate(halves, axis=0).T.astype(o_ref.dtype)


def _attn_kernel(q_ref, k_ref, vt_ref, o_ref):
    _attn_body(q_ref, k_ref, vt_ref, o_ref, False)


def _attention(q, k, vt, b, s):
    tq = min(ATTN_Q_TILE, s)
    nq = s // tq
    pairs = MLA_HEADS // 2
    tk = vt.shape[2]
    nc = s // tk
    return pl.pallas_call(
        _attn_kernel,
        grid=(b, pairs, nq),
        in_specs=[pl.BlockSpec((tq, 2 * HEAD_PAD), lambda bi, pi, qi: (bi * nq + qi, pi)),
                  pl.BlockSpec((s, 2 * HEAD_PAD), lambda bi, pi, qi: (bi, pi)),
                  pl.BlockSpec((nc, 2 * MLA_V_DIM, tk), lambda bi, pi, qi: (bi, pi, 0))],
        out_specs=pl.BlockSpec((tq, 2 * MLA_V_DIM), lambda bi, pi, qi: (bi * nq + qi, pi)),
        out_shape=jax.ShapeDtypeStruct((b * s, MLA_OUT), BF16),
        compiler_params=pltpu.CompilerParams(
            dimension_semantics=("parallel", "parallel", "parallel"),
            vmem_limit_bytes=VMEM_LIMIT),
        name="attention",
    )(q, k, vt)


def _ssd_conv(cur_ref, prev_ref, next_ref, cw_ref, cb_ref, tile_ref, has_prev, has_next):
    r = cur_ref.shape[0]
    tile_ref[0:CONV_HALO, :] = jnp.where(has_prev, prev_ref[...], 0.0)
    tile_ref[CONV_HALO:CONV_HALO + r, :] = cur_ref[...]
    tile_ref[CONV_HALO + r:2 * CONV_HALO + r, :] = jnp.where(has_next, next_ref[...], 0.0)
    acc = jnp.zeros((r, SSD_XBC), F32) + cb_ref[...]
    base = CONV_HALO - SSD_CONV // 2
    for kk in range(SSD_CONV):
        acc = acc + cw_ref[kk:kk + 1, :] * tile_ref[base + kk:base + kk + r, :]
    return acc * _sigmoid(acc)


def _ssd_chunk(act, dt_raw, direction, h_ref, dtbias, a_all, e_mat, skip):
    n = SSD_CHUNK
    gs = SSD_GROUPS * SSD_STATE
    xs = act[:, :SSD_INNER]
    bm = act[:, SSD_INNER:SSD_INNER + gs]
    cm = act[:, SSD_INNER + gs:SSD_INNER + 2 * gs]
    xs16 = xs.astype(BF16)
    bm16 = bm.astype(BF16)
    cm16 = cm.astype(BF16)

    dt_all = _softplus(dt_raw + dtbias)
    a_mat = dt_all * a_all
    ri = lax.broadcasted_iota(I32, (n, n), 0)
    ci = lax.broadcasted_iota(I32, (n, n), 1)
    mask = (ci <= ri) if direction == 0 else (ci >= ri)
    tri = jnp.where(mask, 1.0, 0.0).astype(BF16)
    a1, a2, a3 = _split3(a_mat)
    cs = _dot(tri, a1) + _dot(tri, a2) + _dot(tri, a3)
    end = n - 1 if direction == 0 else 0
    cs_end = cs[end:end + 1, :]
    w_state = dt_all * jnp.exp(cs_end - cs)
    e_off = jnp.exp(cs)
    c_dec = jnp.broadcast_to(jnp.exp(cs_end), (8, LANES))
    stack = jnp.concatenate([w_state, e_off, c_dec], axis=0).astype(BF16)
    expd = _dot(stack, e_mat)
    ws_x = expd[0:n]
    eo_x = expd[n:2 * n]
    cd_x = expd[2 * n:2 * n + 1]

    cs_t = cs.T
    dt_t = dt_all.T
    bm_t = bm.T.astype(BF16)

    pieces = []
    for g in range(SSD_GROUPS):
        gsl = slice(SSD_STATE * g, SSD_STATE * (g + 1))
        cg = cm16[:, gsl]
        gmat = _dot_nt(cg, bm16[:, gsl])
        for r in range(SSD_HEADS_PER_GROUP):
            hh = g * SSD_HEADS_PER_GROUP + r
            c = direction * SSD_HEADS + hh
            seg = cs[:, c:c + 1] - cs_t[c:c + 1, :]
            lm = jnp.where(mask, jnp.exp(jnp.where(mask, seg, 0.0)), 0.0) * dt_t[c:c + 1, :]
            mh = (gmat * lm).astype(BF16)
            pieces.append(_dot(mh, xs16[:, SSD_HEAD_DIM * hh:SSD_HEAD_DIM * (hh + 1)]))
    y = jnp.concatenate(pieces, axis=1)

    w = SSD_HEADS_PER_GROUP * SSD_HEAD_DIM
    offs = []
    for g in range(SSD_GROUPS):
        lsl = slice(w * g, w * (g + 1))
        gsl = slice(SSD_STATE * g, SSD_STATE * (g + 1))
        h_g = h_ref[:, lsl]
        offs.append(_dot(cm16[:, gsl], h_g.astype(BF16)) * eo_x[:, lsl])
        xd = (xs[:, lsl] * ws_x[:, lsl]).astype(BF16)
        h_ref[:, lsl] = h_g * cd_x[:, lsl] + _dot(bm_t[gsl, :], xd)
    y = y + jnp.concatenate(offs, axis=1)
    if skip is not None:
        y = y + xs * skip
    return y


def _ssd_kernel(xf_ref, xfp_ref, xfn_ref, dtf_ref, xb_ref, xbp_ref, xbn_ref, dtb_ref,
                cw_ref, cb_ref, dtbias_ref, alog_ref, skip_ref, e_ref,
                yf_ref, yb_ref, hf_ref, hb_ref, tile_ref):
    i = pl.program_id(1)
    nb = pl.num_programs(1)

    @pl.when(i == 0)
    def _():
        hf_ref[...] = jnp.zeros_like(hf_ref)
        hb_ref[...] = jnp.zeros_like(hb_ref)

    lane = lax.broadcasted_iota(I32, (1, LANES), 1)
    a_all = jnp.where(lane < SSD_DIRECTIONS * SSD_HEADS, -jnp.exp(alog_ref[...]), 0.0)
    dtbias = dtbias_ref[...]
    skip = skip_ref[...]
    nch = xf_ref.shape[0] // SSD_CHUNK

    act = _ssd_conv(xf_ref, xfp_ref, xfn_ref, cw_ref, cb_ref, tile_ref, i > 0, i < nb - 1)
    for c in range(nch):
        rows = slice(SSD_CHUNK * c, SSD_CHUNK * (c + 1))
        y = _ssd_chunk(act[rows], dtf_ref[rows, :], 0, hf_ref, dtbias, a_all, e_ref[0], skip)
        yf_ref[rows, :] = y.astype(yf_ref.dtype)

    act = _ssd_conv(xb_ref, xbp_ref, xbn_ref, cw_ref, cb_ref, tile_ref, i < nb - 1, i > 0)
    for c in reversed(range(nch)):
        rows = slice(SSD_CHUNK * c, SSD_CHUNK * (c + 1))
        y = _ssd_chunk(act[rows], dtb_ref[rows, :], 1, hb_ref, dtbias, a_all, e_ref[1], None)
        yb_ref[rows, :] = y.astype(yb_ref.dtype)


def _ssd(xbc, dt_raw, cw, cb, dtbias, alog, skip, e_mat, b, s):
    r = min(SSD_BLOCK, s)
    nb = s // r
    hb = r // CONV_HALO

    def cur(rev):
        return (lambda bi, i: (bi * nb + (nb - 1 - i), 0)) if rev else (lambda bi, i: (bi * nb + i, 0))

    def prev(rev):
        def f(bi, i):
            j = (nb - 1 - i) if rev else i
            return ((bi * nb + j) * hb - jnp.where(j > 0, 1, 0), 0)
        return f

    def nxt(rev):
        def f(bi, i):
            j = (nb - 1 - i) if rev else i
            return ((bi * nb + j) * hb + jnp.where(j < nb - 1, hb, 0), 0)
        return f

    def role(rev):
        return [pl.BlockSpec((r, SSD_XBC), cur(rev)),
                pl.BlockSpec((CONV_HALO, SSD_XBC), prev(rev)),
                pl.BlockSpec((CONV_HALO, SSD_XBC), nxt(rev)),
                pl.BlockSpec((r, LANES), cur(rev))]

    consts = (cw, cb, dtbias, alog, skip, e_mat)
    return pl.pallas_call(
        _ssd_kernel,
        grid=(b, nb),
        in_specs=role(False) + role(True) + [_full(c.shape) for c in consts],
        out_specs=[pl.BlockSpec((r, SSD_INNER), cur(False)), pl.BlockSpec((r, SSD_INNER), cur(True))],
        out_shape=[jax.ShapeDtypeStruct((b * s, SSD_INNER), F32)] * 2,
        scratch_shapes=[pltpu.VMEM((SSD_STATE, SSD_INNER), F32),
                        pltpu.VMEM((SSD_STATE, SSD_INNER), F32),
                        pltpu.VMEM((r + 2 * CONV_HALO, SSD_XBC), F32)],
        compiler_params=pltpu.CompilerParams(dimension_semantics=("parallel", "arbitrary"),
                                             vmem_limit_bytes=VMEM_LIMIT),
        name="ssd",
    )(xbc, xbc, xbc, dt_raw, xbc, xbc, xbc, dt_raw, *consts)


def _mixout_kernel(x_ref, attn_ref, yf_ref, yb_ref, z_ref, anw_ref, snw_ref, wo_ref, fnw_ref,
                   wr_ref, br_ref, x1_out, route_out):
    attn = _rms(attn_ref[...].astype(F32), anw_ref[...])
    z = z_ref[...].astype(F32)
    y = (yf_ref[...] + yb_ref[...]) * (z * _sigmoid(z))
    y = _rms(y, snw_ref[...])
    mix = jnp.concatenate([attn, y], axis=1).astype(BF16)
    x1 = x_ref[...] + _dot(mix, wo_ref[...])
    _store_token_tiles(x1_out, x1)

    h = _rms(x1, fnw_ref[...])
    h1, h2, _ = _split3(h)
    two = _dot(h1, wr_ref[...]) + _dot(h2, wr_ref[...])
    logits = two[:, :LANES] + two[:, LANES:] + br_ref[...]

    lane = lax.broadcasted_iota(I32, logits.shape, 1)
    ninf = -jnp.inf
    big = 4 * LANES

    def argmax_first(vals):
        vmax = jnp.max(vals, axis=-1, keepdims=True)
        idx = jnp.min(jnp.where(vals == vmax, lane, big), axis=-1, keepdims=True)
        return vmax, idx

    _, gidx = argmax_first(jnp.where(lane < N_EXPERT_GROUPS, logits, ninf))
    lo_lane = N_EXPERT_GROUPS + EXPERTS_PER_GROUP * gidx
    sl = jnp.where((lane >= lo_lane) & (lane < lo_lane + EXPERTS_PER_GROUP), logits, ninf)
    _, i1 = argmax_first(sl)
    _, i2 = argmax_first(jnp.where(lane == i1, ninf, sl))
    elo = jnp.minimum(i1, i2) - lo_lane
    ehi = jnp.maximum(i1, i2) - lo_lane
    pair = (elo * (2 * EXPERTS_PER_GROUP - 1 - elo)) // 2 + (ehi - elo - 1)
    cls = gidx * PAIRS_PER_GROUP + pair
    route_out[...] = jnp.broadcast_to(cls.astype(F32), logits.shape)


def _mixout(x2, attn, yf, yb, z, anw, snw, wo, fnw, wr, br):
    t = x2.shape[0]
    tm = min(TOKEN_TILE, t)
    row = lambda w: pl.BlockSpec((tm, w), lambda i: (i, 0))
    weights = (anw, snw, wo, fnw, wr, br)
    return pl.pallas_call(
        _mixout_kernel,
        grid=(t // tm,),
        in_specs=[row(D_MODEL), row(MLA_OUT), row(SSD_INNER), row(SSD_INNER), row(SSD_INNER)]
        + [_full(w.shape) for w in weights],
        out_specs=[_token_tile_spec(tm), row(LANES)],
        out_shape=[jax.ShapeDtypeStruct((t * ROW_TILES, LANES), F32),
                   jax.ShapeDtypeStruct((t, LANES), F32)],
        compiler_params=pltpu.CompilerParams(dimension_semantics=("parallel",),
                                             vmem_limit_bytes=VMEM_LIMIT),
        name="mixout",
    )(x2, attn, yf, yb, z, *weights)


def _moe_kernel(tile_lo_ref, tile_hi_ref, tile_j_ref, tile_n_ref, order_ref,
                x1_hbm, fnw_ref, wr_ref, br_ref, wgu_lo_ref, wdn_lo_ref, wgu_hi_ref, wdn_hi_ref,
                y_hbm, xbuf, obuf, sem_in, sem_out):
    t = pl.program_id(0)
    tile = xbuf.shape[1] // ROW_TILES
    n_tok = y_hbm.shape[0] - 2 * tile
    slot = t % 2

    def token_rows(r):
        return pl.ds(pl.multiple_of(r * ROW_TILES, ROW_TILES), ROW_TILES)

    def gather_start(tt, sl, inline):
        j0 = tile_j_ref[tt]

        def one(r):
            tok = order_ref[j0 + r]
            pltpu.make_async_copy(x1_hbm.at[tok], xbuf.at[sl, token_rows(r)], sem_in.at[sl]).start()

        if inline:
            for r in range(tile):
                one(r)
        else:
            def body(r, c):
                one(r)
                return c
            lax.fori_loop(0, tile, body, 0, unroll=8)

    def gather_wait(sl):
        pltpu.make_async_copy(xbuf.at[sl], xbuf.at[sl], sem_in.at[sl]).wait()

    def scatter_start(j0, n, sl, inline):
        def one(r):
            tok = jnp.where(r < n, order_ref[j0 + r], n_tok + sl * tile + r)
            pltpu.make_async_copy(obuf.at[sl, token_rows(r)], y_hbm.at[tok], sem_out.at[sl]).start()

        if inline:
            for r in range(tile):
                one(r)
        else:
            def body(r, c):
                one(r)
                return c
            lax.fori_loop(0, tile, body, 0, unroll=8)

    def scatter_wait(sl):
        pltpu.make_async_copy(obuf.at[sl], obuf.at[sl], sem_out.at[sl]).wait()

    prev = jnp.maximum(t - 1, 0)
    valid = tile_n_ref[t] > 0
    prev_valid = (t > 0) & (tile_n_ref[prev] > 0)

    @pl.when(t == 0)
    def _():
        obuf[...] = jnp.zeros_like(obuf)
        scatter_start(0, 0, 0, inline=False)
        scatter_wait(0)

        @pl.when(valid)
        def _():
            gather_start(t, slot, inline=False)

    @pl.when(jnp.logical_not(valid) & prev_valid)
    def _():
        gather_wait(slot)
        scatter_start(tile_j_ref[prev], tile_n_ref[prev], 1 - slot, inline=False)
        scatter_wait(1 - slot)

    @pl.when(valid)
    def _():
        gather_wait(slot)
        gather_start(t + 1, 1 - slot, inline=True)
        scatter_start(tile_j_ref[prev], jnp.where(t > 0, tile_n_ref[prev], 0), 1 - slot, inline=True)

        h = _rms(_load_token_tiles(xbuf.at[slot]), fnw_ref[...]).astype(BF16)
        logits = _dot(h, wr_ref[...]) + br_ref[...]
        lane = lax.broadcasted_iota(I32, logits.shape, 1)
        gl = jnp.where(lane < N_EXPERT_GROUPS, logits, -jnp.inf)
        gweight = 1.0 / jnp.sum(jnp.exp(gl - jnp.max(gl, axis=-1, keepdims=True)), axis=-1, keepdims=True)
        pick = lambda e: jnp.sum(jnp.where(lane == N_EXPERT_GROUPS + e, logits, 0.0), axis=-1, keepdims=True)
        l_lo = pick(tile_lo_ref[t])
        l_hi = pick(tile_hi_ref[t])
        wts = (gweight / (1.0 + jnp.exp(l_hi - l_lo)), gweight / (1.0 + jnp.exp(l_lo - l_hi)))
        acc = None
        for half, wgu_ref, wdn_ref in ((0, wgu_lo_ref, wdn_lo_ref), (1, wgu_hi_ref, wdn_hi_ref)):
            gu = _dot(h, wgu_ref[...])
            g = gu[:, :D_EXPERT]
            he = (g * _sigmoid(g) * gu[:, D_EXPERT:] * wts[half]).astype(BF16)
            d = _dot(he, wdn_ref[...])
            acc = d if acc is None else acc + d
        _store_token_tiles(obuf.at[slot], acc)
        scatter_wait(1 - slot)


def _moe(x1t, order, tile_lo, tile_hi, tile_j, tile_n, fnw, wr, br, wgu, wdn):
    t = x1t.shape[0]
    n_tiles = tile_lo.shape[0]
    tile = MOE_TILE
    wspec = lambda shape, which: pl.BlockSpec(
        (None,) + shape, lambda i, lo, hi, tj, tn, od: ((lo, hi)[which][i], 0, 0))
    grid_spec = pltpu.PrefetchScalarGridSpec(
        num_scalar_prefetch=5,
        grid=(n_tiles,),
        in_specs=[pl.BlockSpec(memory_space=pl.ANY),
                  pl.BlockSpec((1, D_MODEL), lambda i, *_: (0, 0)),
                  pl.BlockSpec((D_MODEL, LANES), lambda i, *_: (0, 0)),
                  pl.BlockSpec((1, LANES), lambda i, *_: (0, 0)),
                  wspec((D_MODEL, 2 * D_EXPERT), 0), wspec((D_EXPERT, D_MODEL), 0),
                  wspec((D_MODEL, 2 * D_EXPERT), 1), wspec((D_EXPERT, D_MODEL), 1)],
        out_specs=pl.BlockSpec(memory_space=pl.ANY),
        scratch_shapes=[pltpu.VMEM((2, tile * ROW_TILES, LANES), F32),
                        pltpu.VMEM((2, tile * ROW_TILES, LANES), F32),
                        pltpu.SemaphoreType.DMA((2,)), pltpu.SemaphoreType.DMA((2,))],
    )
    return pl.pallas_call(
        _moe_kernel,
        grid_spec=grid_spec,
        out_shape=jax.ShapeDtypeStruct((t + 2 * tile, ROW_TILES, LANES), F32),
        compiler_params=pltpu.CompilerParams(dimension_semantics=("arbitrary",),
                                             vmem_limit_bytes=VMEM_LIMIT),
        name="moe",
    )(tile_lo, tile_hi, tile_j, tile_n, order, x1t, fnw, wr, br, wgu, wdn, wgu, wdn)


def _moe_plan(route, t):
    tile = MOE_TILE
    n_tiles = t // tile + N_CLASSES + 1
    cls = route[:, 0].astype(I32)
    _, order = lax.sort((cls, jnp.arange(t, dtype=I32)), num_keys=1, is_stable=True)
    order = jnp.concatenate([order, jnp.zeros((tile,), I32)])
    class_ids = jnp.arange(N_CLASSES, dtype=I32)
    counts = jnp.sum((cls[:, None] == class_ids[None, :]).astype(I32), axis=0)
    tiles_per = (counts + tile - 1) // tile
    tile_end = jnp.cumsum(tiles_per)
    tile_begin = tile_end - tiles_per
    starts = jnp.cumsum(counts) - counts
    t_idx = jnp.arange(n_tiles, dtype=I32)
    tile_cls = jnp.sum((tile_end[None, :] <= t_idx[:, None]).astype(I32), axis=1)
    valid = t_idx < tile_end[-1]
    last_cls = jnp.max(jnp.where(counts > 0, class_ids, 0))
    tile_cls = jnp.where(valid, tile_cls, last_cls)
    onehot = (tile_cls[:, None] == class_ids[None, :]).astype(I32)
    pick = lambda v: jnp.sum(onehot * v[None, :], axis=1)
    k = t_idx - pick(tile_begin)
    tile_j = jnp.where(valid, pick(starts) + k * tile, 0)
    tile_n = jnp.where(valid, jnp.clip(pick(counts) - k * tile, 0, tile), 0)
    lo_of_pair, hi_of_pair = [], []
    for lo in range(EXPERTS_PER_GROUP):
        for hi in range(lo + 1, EXPERTS_PER_GROUP):
            lo_of_pair.append(lo)
            hi_of_pair.append(hi)
    grp = class_ids // PAIRS_PER_GROUP
    class_lo = grp * EXPERTS_PER_GROUP + jnp.asarray(lo_of_pair * N_EXPERT_GROUPS, I32)
    class_hi = grp * EXPERTS_PER_GROUP + jnp.asarray(hi_of_pair * N_EXPERT_GROUPS, I32)
    return order, pick(class_lo), pick(class_hi), tile_j.astype(I32), tile_n.astype(I32)


def _final_kernel(x1_ref, ym_ref, p_ref, pnw_ref, wg_ref, bg_ref, wp_ref, ppnw_ref, fnw_ref, o_ref):
    x2 = _load_token_tiles(x1_ref) + _load_token_tiles(ym_ref)
    gate = _sigmoid(_dot(_rms(x2, pnw_ref[...]).astype(BF16), wg_ref[...]) + bg_ref[...])
    ple = _rms(_dot(p_ref[...].astype(BF16), wp_ref[...]), ppnw_ref[...])
    o_ref[...] = _rms(x2 + gate * ple, fnw_ref[...])


def _final(x1, ym, p2, pnw, wg, bg, wp, ppnw, fnw):
    t = p2.shape[0]
    tm = min(TOKEN_TILE, t)
    row = lambda w: pl.BlockSpec((tm, w), lambda i: (i, 0))
    weights = (pnw, wg, bg, wp, ppnw, fnw)
    return pl.pallas_call(
        _final_kernel,
        grid=(t // tm,),
        in_specs=[_token_tile_spec(tm), _token_tile_spec(tm), row(PLE_DIM)]
        + [_full(w.shape) for w in weights],
        out_specs=row(D_MODEL),
        out_shape=jax.ShapeDtypeStruct((t, D_MODEL), F32),
        compiler_params=pltpu.CompilerParams(dimension_semantics=("parallel",),
                                             vmem_limit_bytes=VMEM_LIMIT),
        name="final",
    )(x1, ym, p2, *weights)


def _pad_cols(w, width, offset=0):
    out = jnp.zeros((w.shape[0], width), w.dtype)
    return out.at[:, offset:offset + w.shape[1]].set(w)


def _rope_swap(w):
    half = MLA_ROPE_DIM // 2
    return jnp.concatenate([-w[..., half:], w[..., :half]], axis=-1)


def _layer(x2, p2, cos_t, sin_t, b, s, attn_norm_w, w_in, q_norm_w, w_uq, kv_norm_w, w_ukv,
           attn_out_norm_w, conv_w, conv_b, dt_bias, a_log, ssd_d, ssd_norm_w, w_o, ffn_norm_w,
           w_router_group, b_router_group, w_router_expert, b_router_expert, w_exp_gate,
           w_exp_up, w_exp_down, ple_norm_w, w_ple_gate, b_ple_gate, w_ple_proj, ple_post_norm_w):
    t = x2.shape[0]
    r1 = lambda v: v.reshape(1, -1).astype(F32)
    o1 = MLA_Q_RANK
    o2 = o1 + MLA_KV_RANK
    o3 = o2 + MLA_ROPE_DIM
    o4 = o3 + SSD_INNER
    o5 = o4 + SSD_XBC
    w_kr = w_in[:, o2:o3]
    wa = jnp.concatenate([w_in[:, :o2], _pad_cols(w_kr, HEAD_PAD, MLA_NOPE_DIM),
                          _pad_cols(_rope_swap(w_kr), HEAD_PAD, MLA_NOPE_DIM),
                          _pad_cols(w_in[:, o5:], LANES)], axis=1).astype(BF16)
    wz = w_in[:, o3:o4].astype(BF16)
    wxbc = w_in[:, o4:o5].astype(BF16)
    uq = w_uq.reshape(MLA_Q_RANK, MLA_HEADS, MLA_NOPE_DIM + MLA_ROPE_DIM)
    zq = jnp.zeros((MLA_Q_RANK, MLA_HEADS, HEAD_PAD - MLA_NOPE_DIM - MLA_ROPE_DIM), F32)
    wuq = jnp.concatenate([uq, zq], axis=-1).reshape(MLA_Q_RANK, -1).astype(BF16)
    wuqs = jnp.concatenate([jnp.zeros_like(uq[..., :MLA_NOPE_DIM]), _rope_swap(uq[..., MLA_NOPE_DIM:]),
                            zq], axis=-1).reshape(MLA_Q_RANK, -1).astype(BF16)
    ukv = w_ukv.reshape(MLA_KV_RANK, MLA_HEADS, MLA_NOPE_DIM + MLA_V_DIM)
    zk = jnp.zeros((MLA_KV_RANK, MLA_HEADS, HEAD_PAD - MLA_NOPE_DIM), F32)
    wuk = jnp.concatenate([ukv[..., :MLA_NOPE_DIM], zk], axis=-1).reshape(MLA_KV_RANK, -1).astype(BF16)
    wuv = ukv[..., MLA_NOPE_DIM:].reshape(MLA_KV_RANK, -1).T.astype(BF16)

    q, k, v, z, xbc, dt_raw = _inproj(x2, cos_t, sin_t, r1(attn_norm_w), wa, wz, wxbc,
                                      r1(q_norm_w), wuq, wuqs, r1(kv_norm_w), wuk, wuv)
    attn = _attention(q, k, v, b, s)

    head_of_lane = jnp.arange(SSD_INNER) // SSD_HEAD_DIM
    rows = jnp.arange(LANES)[:, None]
    e_mat = jnp.stack([(rows == d * SSD_HEADS + head_of_lane[None, :]) for d in range(SSD_DIRECTIONS)]
                      ).astype(BF16)
    skip = jnp.repeat(ssd_d.astype(F32), SSD_HEAD_DIM).reshape(1, -1)
    yf, yb = _ssd(xbc, dt_raw, conv_w.astype(F32), r1(conv_b), _pad_cols(r1(dt_bias), LANES),
                  _pad_cols(r1(a_log), LANES), skip, e_mat, b, s)

    wr = _pad_cols(jnp.concatenate([w_router_group, w_router_expert], axis=1).astype(F32), LANES)
    wr_hi, wr_lo, _ = _split3(wr)
    wr2 = jnp.concatenate([wr_hi, wr_lo], axis=1)
    br = _pad_cols(jnp.concatenate([r1(b_router_group), r1(b_router_expert)], axis=1), LANES)
    x1, route = _mixout(x2, attn, yf, yb, z, r1(attn_out_norm_w), r1(ssd_norm_w), w_o.astype(BF16),
                        r1(ffn_norm_w), wr2, br)

    order, tile_lo, tile_hi, tile_j, tile_n = _moe_plan(route, t)
    wgu = jnp.concatenate([w_exp_gate, w_exp_up], axis=-1).astype(BF16)
    ym = _moe(x1.reshape(t, ROW_TILES, LANES), order, tile_lo, tile_hi, tile_j, tile_n,
              r1(ffn_norm_w), wr_hi, br, wgu, w_exp_down.astype(BF16))
    ym = ym.reshape(ym.shape[0] * ROW_TILES, LANES)
    return x1, ym, (r1(ple_norm_w), w_ple_gate.astype(BF16), r1(b_ple_gate), w_ple_proj.astype(BF16),
                    r1(ple_post_norm_w))


def kernel(x, p, positions, attn_norm_w, w_in, q_norm_w, w_uq, kv_norm_w, w_ukv, attn_out_norm_w, conv_w, conv_b, dt_bias, a_log, ssd_d, ssd_norm_w, w_o, ffn_norm_w, w_router_group, b_router_group, w_router_expert, b_router_expert, w_exp_gate, w_exp_up, w_exp_down, ple_norm_w, w_ple_gate, b_ple_gate, w_ple_proj, ple_post_norm_w, final_norm_w):
    b, s, d = x.shape
    depth = p.shape[0]
    assert depth == 1, "the fused final stage assumes a single layer"
    t = b * s
    inv_freq = 1.0 / (ROPE_THETA ** (jnp.arange(0, MLA_ROPE_DIM, 2, dtype=F32) / MLA_ROPE_DIM))
    half = MLA_ROPE_DIM // 2
    ang = (positions.astype(F32).reshape(t, 1) * inv_freq).reshape(t * half // LANES, LANES)
    cos, sin = lax.optimization_barrier((jnp.cos(ang), jnp.sin(ang)))
    cos = cos.reshape(t, half)
    sin = sin.reshape(t, half)
    ones = jnp.ones((t, MLA_NOPE_DIM), F32)
    zeros = jnp.zeros((t, HEAD_PAD - MLA_NOPE_DIM - MLA_ROPE_DIM), F32)
    cos_t = jnp.concatenate([ones, cos, cos, zeros], axis=1)
    sin_t = jnp.concatenate([0.0 * ones, sin, sin, zeros], axis=1)

    x2 = x.reshape(t, d)
    i = 0
    x1, ym, (pnw, wg, bg, wp, ppnw) = _layer(
        x2, p[i].reshape(t, -1), cos_t, sin_t, b, s, attn_norm_w[i], w_in[i], q_norm_w[i], w_uq[i],
        kv_norm_w[i], w_ukv[i], attn_out_norm_w[i], conv_w[i], conv_b[i], dt_bias[i], a_log[i],
        ssd_d[i], ssd_norm_w[i], w_o[i], ffn_norm_w[i], w_router_group[i], b_router_group[i],
        w_router_expert[i], b_router_expert[i], w_exp_gate[i], w_exp_up[i], w_exp_down[i],
        ple_norm_w[i], w_ple_gate[i], b_ple_gate[i], w_ple_proj[i], ple_post_norm_w[i])
    out = _final(x1, ym, p[i].reshape(t, -1), pnw, wg, bg, wp, ppnw, final_norm_w.reshape(1, -1).astype(F32))
    return out.reshape(b, s, d)
```

```python
import functools
import math

import jax
import jax.numpy as jnp
from jax import lax
from jax.experimental import pallas as pl
from jax.experimental.pallas import tpu as pltpu

F32 = jnp.float32
BF16 = jnp.bfloat16
I32 = jnp.int32

D_MODEL = 1024
PLE_DIM = 256
NORM_EPS = 1e-6

MLA_HEADS = 8
MLA_Q_RANK = 256
MLA_KV_RANK = 128
MLA_NOPE_DIM = 64
MLA_ROPE_DIM = 32
MLA_V_DIM = 64
MLA_OUT = MLA_HEADS * MLA_V_DIM
ROPE_THETA = 10000.0
HEAD_PAD = 128
ONES_ROWS = 16

SSD_HEADS = 8
SSD_HEAD_DIM = 64
SSD_GROUPS = 2
SSD_HEADS_PER_GROUP = SSD_HEADS // SSD_GROUPS
SSD_STATE = 64
SSD_CONV = 5
SSD_CHUNK = 128
SSD_INNER = SSD_HEADS * SSD_HEAD_DIM
SSD_XBC = SSD_INNER + 2 * SSD_GROUPS * SSD_STATE
SSD_DIRECTIONS = 2
CONV_HALO = 8

N_EXPERT_GROUPS = 4
EXPERTS_PER_GROUP = 8
N_EXPERTS = N_EXPERT_GROUPS * EXPERTS_PER_GROUP
D_EXPERT = 256
PAIRS_PER_GROUP = EXPERTS_PER_GROUP * (EXPERTS_PER_GROUP - 1) // 2
N_CLASSES = N_EXPERT_GROUPS * PAIRS_PER_GROUP

LANES = 128
ROW_TILES = D_MODEL // LANES
VMEM_LIMIT = 48 * 1024 * 1024

TOKEN_TILE = 512
ATTN_Q_TILE = 512
ATTN_Q_SUB = 256
ATTN_KV_CHUNK = 256
ATTN_AHEAD = 4
ATTN_LOGIT_BOUND = 80.0
SSD_BLOCK = 256
MOE_TILE = 128


def _rms(x, w):
    ms = jnp.mean(x * x, axis=-1, keepdims=True)
    return x * lax.rsqrt(ms + NORM_EPS) * w


def _dot(a, b):
    return jnp.dot(a, b, preferred_element_type=F32)


def _dot_nt(a, b):
    return lax.dot_general(a, b, (((1,), (1,)), ((), ())), preferred_element_type=F32)


def _split3(x):
    x1 = x.astype(BF16)
    r1 = x - x1.astype(F32)
    x2 = r1.astype(BF16)
    x3 = (r1 - x2.astype(F32)).astype(BF16)
    return x1, x2, x3


def _sigmoid(x):
    return 1.0 / (1.0 + jnp.exp(-x))


def _softplus(x):
    return jnp.maximum(x, 0.0) + jnp.log(1.0 + jnp.exp(-jnp.abs(x)))


def _full(shape):
    nd = len(shape)
    return pl.BlockSpec(shape, lambda *_: (0,) * nd)


def _token_tile_spec(tm):
    return pl.BlockSpec((tm * ROW_TILES, LANES), lambda i: (i, 0))


def _store_token_tiles(ref, x):
    n = x.shape[0]
    for s in range(ROW_TILES):
        ref[pl.ds(s, n, stride=ROW_TILES), :] = x[:, LANES * s:LANES * (s + 1)]


def _load_token_tiles(ref):
    n = ref.shape[0] // ROW_TILES
    return jnp.concatenate([ref[pl.ds(s, n, stride=ROW_TILES), :] for s in range(ROW_TILES)], axis=1)


def _inproj_kernel(x_ref, nw_ref, wa_ref, wz_ref, wxbc_ref, qnw_ref, wuq_ref, wuqs_ref,
                   kvnw_ref, wuk_ref, wuv_ref, cos_ref, sin_ref,
                   q_out, k_out, v_out, z_out, xbc_out, dt_out, *, scale):
    h = _rms(x_ref[...], nw_ref[...]).astype(BF16)
    pa = _dot(h, wa_ref[...])
    z_out[...] = _dot(h, wz_ref[...]).astype(z_out.dtype)
    xbc_out[...] = _dot(h, wxbc_ref[...]).astype(xbc_out.dtype)
    dt_out[...] = pa[:, MLA_Q_RANK + MLA_KV_RANK + 2 * HEAD_PAD:]
    cos = cos_ref[...]
    sin = sin_ref[...]
    o1 = MLA_Q_RANK
    o2 = o1 + MLA_KV_RANK
    cqn = _rms(pa[:, :o1], qnw_ref[...]).astype(BF16)
    ckvn = _rms(pa[:, o1:o2], kvnw_ref[...]).astype(BF16)
    q = _dot(cqn, wuq_ref[...])
    qs = _dot(cqn, wuqs_ref[...])
    kn = _dot(ckvn, wuk_ref[...])
    v_out[...] = _dot_nt(wuv_ref[...], ckvn).astype(v_out.dtype)
    kr = pa[:, o2:o2 + HEAD_PAD] * cos + pa[:, o2 + HEAD_PAD:o2 + 2 * HEAD_PAD] * sin
    for hh in range(MLA_HEADS):
        sl = slice(HEAD_PAD * hh, HEAD_PAD * (hh + 1))
        q_out[:, sl] = ((q[:, sl] * cos + qs[:, sl] * sin) * scale).astype(q_out.dtype)
        k_out[:, sl] = (kn[:, sl] + kr).astype(k_out.dtype)


def _inproj(x2, cos_t, sin_t, nw, wa, wz, wxbc, qnw, wuq, wuqs, kvnw, wuk, wuv):
    t = x2.shape[0]
    tm = min(TOKEN_TILE, t)
    scale = (MLA_NOPE_DIM + MLA_ROPE_DIM) ** -0.5 * math.log2(math.e)
    row = lambda w: pl.BlockSpec((tm, w), lambda i: (i, 0))
    weights = (nw, wa, wz, wxbc, qnw, wuq, wuqs, kvnw, wuk, wuv)
    return pl.pallas_call(
        functools.partial(_inproj_kernel, scale=scale),
        grid=(t // tm,),
        in_specs=[row(D_MODEL)] + [_full(w.shape) for w in weights] + [row(HEAD_PAD), row(HEAD_PAD)],
        out_specs=[row(MLA_HEADS * HEAD_PAD), row(MLA_HEADS * HEAD_PAD),
                   pl.BlockSpec((None, MLA_OUT, tm), lambda i: (i, 0, 0)),
                   row(SSD_INNER), row(SSD_XBC), row(LANES)],
        out_shape=[jax.ShapeDtypeStruct((t, MLA_HEADS * HEAD_PAD), BF16),
                   jax.ShapeDtypeStruct((t, MLA_HEADS * HEAD_PAD), BF16),
                   jax.ShapeDtypeStruct((t // tm, MLA_OUT, tm), BF16),
                   jax.ShapeDtypeStruct((t, SSD_INNER), BF16),
                   jax.ShapeDtypeStruct((t, SSD_XBC), F32),
                   jax.ShapeDtypeStruct((t, LANES), F32)],
        compiler_params=pltpu.CompilerParams(dimension_semantics=("parallel",),
                                             vmem_limit_bytes=VMEM_LIMIT),
        name="inproj",
    )(x2, *weights, cos_t, sin_t)


def _attn_body(q_ref, k_ref, vt_ref, o_ref, bounded):
    tq = q_ref.shape[0]
    nc, _, tv = vt_ref.shape
    tk = ATTN_KV_CHUNK
    ts = min(ATTN_Q_SUB, tq)
    ones = jnp.ones((ONES_ROWS, tk), BF16)
    streams = []
    for j in range(tq // ts):
        for a in range(2):
            hsl = slice(HEAD_PAD * a, HEAD_PAD * (a + 1))
            vsl = slice(MLA_V_DIM * a, MLA_V_DIM * (a + 1))
            streams.append([hsl, vsl, q_ref[j * ts:(j + 1) * ts, hsl],
                            jnp.full((1, ts), -1e30, F32),
                            jnp.zeros((MLA_V_DIM + ONES_ROWS, ts), F32)])
    units = [(c, i) for c in range(nc * tv // tk) for i in range(len(streams))]
    scores = {}

    def issue(u):
        c, i = units[u]
        scores[u] = _dot_nt(k_ref[c * tk:(c + 1) * tk, streams[i][0]], streams[i][2])

    for u in range(min(ATTN_AHEAD, len(units))):
        issue(u)
    for u, (c, i) in enumerate(units):
        if u + ATTN_AHEAD < len(units):
            issue(u + ATTN_AHEAD)
        st = streams[i]
        _, vsl, _, m, acc = st
        s = scores.pop(u)
        blk, off = divmod(c * tk, tv)
        vt = jnp.concatenate([vt_ref[blk, vsl, off:off + tk], ones], axis=0)
        if bounded:
            st[4] = acc + _dot(vt, jnp.exp2(s).astype(BF16))
        else:
            m_new = jnp.maximum(m, jnp.max(s, axis=0, keepdims=True))
            p = jnp.exp2((s - m_new).astype(BF16))
            st[3] = m_new
            st[4] = acc * jnp.exp2(m - m_new) + _dot(vt, p)
    for j in range(tq // ts):
        halves = [st[4][:MLA_V_DIM] / st[4][MLA_V_DIM:MLA_V_DIM + 1] for st in streams[2 * j:2 * j + 2]]
        o_ref[j * ts:(j + 1) * ts, :] = jnp.concatenate(halves, axis=0).T.astype(o_ref.dtype)


def _attn_kernel(q_ref, k_ref, vt_ref, o_ref, kmax_ref):
    @pl.when(pl.program_id(2) == 0)
    def _():
        for a in range(2):
            kmax_ref[a] = jnp.max(jnp.abs(k_ref[:, HEAD_PAD * a:HEAD_PAD * (a + 1)].astype(F32)))

    bounded = None
    for a in range(2):
        q1 = jnp.sum(jnp.abs(q_ref[:, HEAD_PAD * a:HEAD_PAD * (a + 1)].astype(F32)), axis=1, keepdims=True)
        inside = jnp.max(q1) * kmax_ref[a] <= ATTN_LOGIT_BOUND
        bounded = inside if bounded is None else bounded & inside

    @pl.when(bounded)
    def _():
        _attn_body(q_ref, k_ref, vt_ref, o_ref, True)

    @pl.when(jnp.logical_not(bounded))
    def _():
        _attn_body(q_ref, k_ref, vt_ref, o_ref, False)


def _attention(q, k, vt, b, s):
    tq = min(ATTN_Q_TILE, s)
    nq = s // tq
    pairs = MLA_HEADS // 2
    tk = vt.shape[2]
    nc = s // tk
    return pl.pallas_call(
        _attn_kernel,
        grid=(b, pairs, nq),
        in_specs=[pl.BlockSpec((tq, 2 * HEAD_PAD), lambda bi, pi, qi: (bi * nq + qi, pi)),
                  pl.BlockSpec((s, 2 * HEAD_PAD), lambda bi, pi, qi: (bi, pi)),
                  pl.BlockSpec((nc, 2 * MLA_V_DIM, tk), lambda bi, pi, qi: (bi, pi, 0))],
        out_specs=pl.BlockSpec((tq, 2 * MLA_V_DIM), lambda bi, pi, qi: (bi * nq + qi, pi)),
        out_shape=jax.ShapeDtypeStruct((b * s, MLA_OUT), BF16),
        scratch_shapes=[pltpu.SMEM((2,), F32)],
        compiler_params=pltpu.CompilerParams(
            dimension_semantics=("parallel", "parallel", "arbitrary"),
            vmem_limit_bytes=VMEM_LIMIT),
        name="attention",
    )(q, k, vt)


def _ssd_conv(cur_ref, prev_ref, next_ref, cw_ref, cb_ref, tile_ref, has_prev, has_next):
    r = cur_ref.shape[0]
    tile_ref[0:CONV_HALO, :] = jnp.where(has_prev, prev_ref[...], 0.0)
    tile_ref[CONV_HALO:CONV_HALO + r, :] = cur_ref[...]
    tile_ref[CONV_HALO + r:2 * CONV_HALO + r, :] = jnp.where(has_next, next_ref[...], 0.0)
    acc = jnp.zeros((r, SSD_XBC), F32) + cb_ref[...]
    base = CONV_HALO - SSD_CONV // 2
    for kk in range(SSD_CONV):
        acc = acc + cw_ref[kk:kk + 1, :] * tile_ref[base + kk:base + kk + r, :]
    return acc * _sigmoid(acc)


def _ssd_chunk(act, dt_raw, direction, h_ref, dtbias, a_all, e_mat, skip):
    n = SSD_CHUNK
    gs = SSD_GROUPS * SSD_STATE
    xs = act[:, :SSD_INNER]
    bm = act[:, SSD_INNER:SSD_INNER + gs]
    cm = act[:, SSD_INNER + gs:SSD_INNER + 2 * gs]
    xs16 = xs.astype(BF16)
    bm16 = bm.astype(BF16)
    cm16 = cm.astype(BF16)

    dt_all = _softplus(dt_raw + dtbias)
    a_mat = dt_all * a_all
    ri = lax.broadcasted_iota(I32, (n, n), 0)
    ci = lax.broadcasted_iota(I32, (n, n), 1)
    mask = (ci <= ri) if direction == 0 else (ci >= ri)
    tri = jnp.where(mask, 1.0, 0.0).astype(BF16)
    a1, a2, a3 = _split3(a_mat)
    cs = _dot(tri, a1) + _dot(tri, a2) + _dot(tri, a3)
    end = n - 1 if direction == 0 else 0
    cs_end = cs[end:end + 1, :]
    w_state = dt_all * jnp.exp(cs_end - cs)
    e_off = jnp.exp(cs)
    c_dec = jnp.broadcast_to(jnp.exp(cs_end), (8, LANES))
    stack = jnp.concatenate([w_state, e_off, c_dec], axis=0).astype(BF16)
    expd = _dot(stack, e_mat)
    ws_x = expd[0:n]
    eo_x = expd[n:2 * n]
    cd_x = expd[2 * n:2 * n + 1]

    cs_t = cs.T
    dt_t = dt_all.T
    bm_t = bm.T.astype(BF16)

    pieces = []
    for g in range(SSD_GROUPS):
        gsl = slice(SSD_STATE * g, SSD_STATE * (g + 1))
        cg = cm16[:, gsl]
        gmat = _dot_nt(cg, bm16[:, gsl])
        for r in range(SSD_HEADS_PER_GROUP):
            hh = g * SSD_HEADS_PER_GROUP + r
            c = direction * SSD_HEADS + hh
            seg = cs[:, c:c + 1] - cs_t[c:c + 1, :]
            lm = jnp.where(mask, jnp.exp(jnp.where(mask, seg, 0.0)), 0.0) * dt_t[c:c + 1, :]
            mh = (gmat * lm).astype(BF16)
            pieces.append(_dot(mh, xs16[:, SSD_HEAD_DIM * hh:SSD_HEAD_DIM * (hh + 1)]))
    y = jnp.concatenate(pieces, axis=1)

    w = SSD_HEADS_PER_GROUP * SSD_HEAD_DIM
    offs = []
    for g in range(SSD_GROUPS):
        lsl = slice(w * g, w * (g + 1))
        gsl = slice(SSD_STATE * g, SSD_STATE * (g + 1))
        h_g = h_ref[:, lsl]
        offs.append(_dot(cm16[:, gsl], h_g.astype(BF16)) * eo_x[:, lsl])
        xd = (xs[:, lsl] * ws_x[:, lsl]).astype(BF16)
        h_ref[:, lsl] = h_g * cd_x[:, lsl] + _dot(bm_t[gsl, :], xd)
    y = y + jnp.concatenate(offs, axis=1)
    if skip is not None:
        y = y + xs * skip
    return y


def _ssd_kernel(xf_ref, xfp_ref, xfn_ref, dtf_ref, xb_ref, xbp_ref, xbn_ref, dtb_ref,
                cw_ref, cb_ref, dtbias_ref, alog_ref, skip_ref, e_ref,
                yf_ref, yb_ref, hf_ref, hb_ref, tile_ref):
    i = pl.program_id(1)
    nb = pl.num_programs(1)

    @pl.when(i == 0)
    def _():
        hf_ref[...] = jnp.zeros_like(hf_ref)
        hb_ref[...] = jnp.zeros_like(hb_ref)

    lane = lax.broadcasted_iota(I32, (1, LANES), 1)
    a_all = jnp.where(lane < SSD_DIRECTIONS * SSD_HEADS, -jnp.exp(alog_ref[...]), 0.0)
    dtbias = dtbias_ref[...]
    skip = skip_ref[...]
    nch = xf_ref.shape[0] // SSD_CHUNK

    act = _ssd_conv(xf_ref, xfp_ref, xfn_ref, cw_ref, cb_ref, tile_ref, i > 0, i < nb - 1)
    for c in range(nch):
        rows = slice(SSD_CHUNK * c, SSD_CHUNK * (c + 1))
        y = _ssd_chunk(act[rows], dtf_ref[rows, :], 0, hf_ref, dtbias, a_all, e_ref[0], skip)
        yf_ref[rows, :] = y.astype(yf_ref.dtype)

    act = _ssd_conv(xb_ref, xbp_ref, xbn_ref, cw_ref, cb_ref, tile_ref, i < nb - 1, i > 0)
    for c in reversed(range(nch)):
        rows = slice(SSD_CHUNK * c, SSD_CHUNK * (c + 1))
        y = _ssd_chunk(act[rows], dtb_ref[rows, :], 1, hb_ref, dtbias, a_all, e_ref[1], None)
        yb_ref[rows, :] = y.astype(yb_ref.dtype)


def _ssd(xbc, dt_raw, cw, cb, dtbias, alog, skip, e_mat, b, s):
    r = min(SSD_BLOCK, s)
    nb = s // r
    hb = r // CONV_HALO

    def cur(rev):
        return (lambda bi, i: (bi * nb + (nb - 1 - i), 0)) if rev else (lambda bi, i: (bi * nb + i, 0))

    def prev(rev):
        def f(bi, i):
            j = (nb - 1 - i) if rev else i
            return ((bi * nb + j) * hb - jnp.where(j > 0, 1, 0), 0)
        return f

    def nxt(rev):
        def f(bi, i):
            j = (nb - 1 - i) if rev else i
            return ((bi * nb + j) * hb + jnp.where(j < nb - 1, hb, 0), 0)
        return f

    def role(rev):
        return [pl.BlockSpec((r, SSD_XBC), cur(rev)),
                pl.BlockSpec((CONV_HALO, SSD_XBC), prev(rev)),
                pl.BlockSpec((CONV_HALO, SSD_XBC), nxt(rev)),
                pl.BlockSpec((r, LANES), cur(rev))]

    consts = (cw, cb, dtbias, alog, skip, e_mat)
    return pl.pallas_call(
        _ssd_kernel,
        grid=(b, nb),
        in_specs=role(False) + role(True) + [_full(c.shape) for c in consts],
        out_specs=[pl.BlockSpec((r, SSD_INNER), cur(False)), pl.BlockSpec((r, SSD_INNER), cur(True))],
        out_shape=[jax.ShapeDtypeStruct((b * s, SSD_INNER), F32)] * 2,
        scratch_shapes=[pltpu.VMEM((SSD_STATE, SSD_INNER), F32),
                        pltpu.VMEM((SSD_STATE, SSD_INNER), F32),
                        pltpu.VMEM((r + 2 * CONV_HALO, SSD_XBC), F32)],
        compiler_params=pltpu.CompilerParams(dimension_semantics=("parallel", "arbitrary"),
                                             vmem_limit_bytes=VMEM_LIMIT),
        name="ssd",
    )(xbc, xbc, xbc, dt_raw, xbc, xbc, xbc, dt_raw, *consts)


def _mixout_kernel(x_ref, attn_ref, yf_ref, yb_ref, z_ref, anw_ref, snw_ref, wo_ref, fnw_ref,
                   wr_ref, br_ref, x1_out, route_out):
    attn = _rms(attn_ref[...].astype(F32), anw_ref[...])
    z = z_ref[...].astype(F32)
    y = (yf_ref[...] + yb_ref[...]) * (z * _sigmoid(z))
    y = _rms(y, snw_ref[...])
    mix = jnp.concatenate([attn, y], axis=1).astype(BF16)
    x1 = x_ref[...] + _dot(mix, wo_ref[...])
    _store_token_tiles(x1_out, x1)

    h = _rms(x1, fnw_ref[...])
    h1, h2, _ = _split3(h)
    two = _dot(h1, wr_ref[...]) + _dot(h2, wr_ref[...])
    logits = two[:, :LANES] + two[:, LANES:] + br_ref[...]

    lane = lax.broadcasted_iota(I32, logits.shape, 1)
    ninf = -jnp.inf
    big = 4 * LANES

    def argmax_first(vals):
        vmax = jnp.max(vals, axis=-1, keepdims=True)
        idx = jnp.min(jnp.where(vals == vmax, lane, big), axis=-1, keepdims=True)
        return vmax, idx

    _, gidx = argmax_first(jnp.where(lane < N_EXPERT_GROUPS, logits, ninf))
    lo_lane = N_EXPERT_GROUPS + EXPERTS_PER_GROUP * gidx
    sl = jnp.where((lane >= lo_lane) & (lane < lo_lane + EXPERTS_PER_GROUP), logits, ninf)
    _, i1 = argmax_first(sl)
    _, i2 = argmax_first(jnp.where(lane == i1, ninf, sl))
    elo = jnp.minimum(i1, i2) - lo_lane
    ehi = jnp.maximum(i1, i2) - lo_lane
    pair = (elo * (2 * EXPERTS_PER_GROUP - 1 - elo)) // 2 + (ehi - elo - 1)
    cls = gidx * PAIRS_PER_GROUP + pair
    route_out[...] = jnp.broadcast_to(cls.astype(F32), logits.shape)


def _mixout(x2, attn, yf, yb, z, anw, snw, wo, fnw, wr, br):
    t = x2.shape[0]
    tm = min(TOKEN_TILE, t)
    row = lambda w: pl.BlockSpec((tm, w), lambda i: (i, 0))
    weights = (anw, snw, wo, fnw, wr, br)
    return pl.pallas_call(
        _mixout_kernel,
        grid=(t // tm,),
        in_specs=[row(D_MODEL), row(MLA_OUT), row(SSD_INNER), row(SSD_INNER), row(SSD_INNER)]
        + [_full(w.shape) for w in weights],
        out_specs=[_token_tile_spec(tm), row(LANES)],
        out_shape=[jax.ShapeDtypeStruct((t * ROW_TILES, LANES), F32),
                   jax.ShapeDtypeStruct((t, LANES), F32)],
        compiler_params=pltpu.CompilerParams(dimension_semantics=("parallel",),
                                             vmem_limit_bytes=VMEM_LIMIT),
        name="mixout",
    )(x2, attn, yf, yb, z, *weights)


def _moe_kernel(tile_lo_ref, tile_hi_ref, tile_j_ref, tile_n_ref, order_ref,
                x1_hbm, fnw_ref, wr_ref, br_ref, wgu_lo_ref, wdn_lo_ref, wgu_hi_ref, wdn_hi_ref,
                y_hbm, xbuf, obuf, sem_in, sem_out):
    t = pl.program_id(0)
    tile = xbuf.shape[1] // ROW_TILES
    n_tok = y_hbm.shape[0] - 2 * tile
    slot = t % 2

    def token_rows(r):
        return pl.ds(pl.multiple_of(r * ROW_TILES, ROW_TILES), ROW_TILES)

    def gather_start(tt, sl, inline):
        j0 = tile_j_ref[tt]

        def one(r, priority=0):
            tok = order_ref[j0 + r]
            pltpu.make_async_copy(x1_hbm.at[tok], xbuf.at[sl, token_rows(r)],
                                  sem_in.at[sl]).start(priority=priority)

        if inline:
            for r in range(tile):
                one(r, r % 2)
        else:
            def body(r, c):
                one(r)
                return c
            lax.fori_loop(0, tile, body, 0, unroll=8)

    def gather_wait(sl):
        pltpu.make_async_copy(xbuf.at[sl], xbuf.at[sl], sem_in.at[sl]).wait()

    def scatter_start(j0, n, sl, inline):
        def one(r, priority=0):
            tok = jnp.where(r < n, order_ref[j0 + r], n_tok + sl * tile + r)
            pltpu.make_async_copy(obuf.at[sl, token_rows(r)], y_hbm.at[tok],
                                  sem_out.at[sl]).start(priority=priority)

        if inline:
            for r in range(tile):
                one(r, r % 2)
        else:
            def body(r, c):
                one(r)
                return c
            lax.fori_loop(0, tile, body, 0, unroll=8)

    def scatter_wait(sl):
        pltpu.make_async_copy(obuf.at[sl], obuf.at[sl], sem_out.at[sl]).wait()

    prev = jnp.maximum(t - 1, 0)
    valid = tile_n_ref[t] > 0
    prev_valid = (t > 0) & (tile_n_ref[prev] > 0)

    @pl.when(t == 0)
    def _():
        obuf[...] = jnp.zeros_like(obuf)
        scatter_start(0, 0, 0, inline=False)
        scatter_wait(0)

        @pl.when(valid)
        def _():
            gather_start(t, slot, inline=False)

    @pl.when(jnp.logical_not(valid) & prev_valid)
    def _():
        gather_wait(slot)
        scatter_start(tile_j_ref[prev], tile_n_ref[prev], 1 - slot, inline=False)
        scatter_wait(1 - slot)

    @pl.when(valid)
    def _():
        gather_wait(slot)
        gather_start(t + 1, 1 - slot, inline=True)
        scatter_start(tile_j_ref[prev], jnp.where(t > 0, tile_n_ref[prev], 0), 1 - slot, inline=True)

        h = _rms(_load_token_tiles(xbuf.at[slot]), fnw_ref[...]).astype(BF16)
        logits = _dot(h, wr_ref[...]) + br_ref[...]
        lane = lax.broadcasted_iota(I32, logits.shape, 1)
        gl = jnp.where(lane < N_EXPERT_GROUPS, logits, -jnp.inf)
        gweight = 1.0 / jnp.sum(jnp.exp(gl - jnp.max(gl, axis=-1, keepdims=True)), axis=-1, keepdims=True)
        pick = lambda e: jnp.sum(jnp.where(lane == N_EXPERT_GROUPS + e, logits, 0.0), axis=-1, keepdims=True)
        l_lo = pick(tile_lo_ref[t])
        l_hi = pick(tile_hi_ref[t])
        wts = (gweight / (1.0 + jnp.exp(l_hi - l_lo)), gweight / (1.0 + jnp.exp(l_lo - l_hi)))
        acc = None
        for half, wgu_ref, wdn_ref in ((0, wgu_lo_ref, wdn_lo_ref), (1, wgu_hi_ref, wdn_hi_ref)):
            gu = _dot(h, wgu_ref[...])
            g = gu[:, :D_EXPERT]
            he = (g * _sigmoid(g) * gu[:, D_EXPERT:] * wts[half]).astype(BF16)
            d = _dot(he, wdn_ref[...])
            acc = d if acc is None else acc + d
        _store_token_tiles(obuf.at[slot], acc)
        scatter_wait(1 - slot)


def _moe(x1t, order, tile_lo, tile_hi, tile_j, tile_n, fnw, wr, br, wgu, wdn):
    t = x1t.shape[0]
    n_tiles = tile_lo.shape[0]
    tile = MOE_TILE
    wspec = lambda shape, which: pl.BlockSpec(
        (None,) + shape, lambda i, lo, hi, tj, tn, od: ((lo, hi)[which][i], 0, 0))
    grid_spec = pltpu.PrefetchScalarGridSpec(
        num_scalar_prefetch=5,
        grid=(n_tiles,),
        in_specs=[pl.BlockSpec(memory_space=pl.ANY),
                  pl.BlockSpec((1, D_MODEL), lambda i, *_: (0, 0)),
                  pl.BlockSpec((D_MODEL, LANES), lambda i, *_: (0, 0)),
                  pl.BlockSpec((1, LANES), lambda i, *_: (0, 0)),
                  wspec((D_MODEL, 2 * D_EXPERT), 0), wspec((D_EXPERT, D_MODEL), 0),
                  wspec((D_MODEL, 2 * D_EXPERT), 1), wspec((D_EXPERT, D_MODEL), 1)],
        out_specs=pl.BlockSpec(memory_space=pl.ANY),
        scratch_shapes=[pltpu.VMEM((2, tile * ROW_TILES, LANES), F32),
                        pltpu.VMEM((2, tile * ROW_TILES, LANES), F32),
                        pltpu.SemaphoreType.DMA((2,)), pltpu.SemaphoreType.DMA((2,))],
    )
    return pl.pallas_call(
        _moe_kernel,
        grid_spec=grid_spec,
        out_shape=jax.ShapeDtypeStruct((t + 2 * tile, ROW_TILES, LANES), F32),
        compiler_params=pltpu.CompilerParams(dimension_semantics=("arbitrary",),
                                             vmem_limit_bytes=VMEM_LIMIT),
        name="moe",
    )(tile_lo, tile_hi, tile_j, tile_n, order, x1t, fnw, wr, br, wgu, wdn, wgu, wdn)


def _moe_plan(route, t):
    tile = MOE_TILE
    n_tiles = t // tile + N_CLASSES + 1
    cls = route[:, 0].astype(I32)
    _, order = lax.sort((cls, jnp.arange(t, dtype=I32)), num_keys=1, is_stable=True)
    order = jnp.concatenate([order, jnp.zeros((tile,), I32)])
    class_ids = jnp.arange(N_CLASSES, dtype=I32)
    counts = jnp.sum((cls[:, None] == class_ids[None, :]).astype(I32), axis=0)
    tiles_per = (counts + tile - 1) // tile
    tile_end = jnp.cumsum(tiles_per)
    tile_begin = tile_end - tiles_per
    starts = jnp.cumsum(counts) - counts
    t_idx = jnp.arange(n_tiles, dtype=I32)
    tile_cls = jnp.sum((tile_end[None, :] <= t_idx[:, None]).astype(I32), axis=1)
    valid = t_idx < tile_end[-1]
    last_cls = jnp.max(jnp.where(counts > 0, class_ids, 0))
    tile_cls = jnp.where(valid, tile_cls, last_cls)
    onehot = (tile_cls[:, None] == class_ids[None, :]).astype(I32)
    pick = lambda v: jnp.sum(onehot * v[None, :], axis=1)
    k = t_idx - pick(tile_begin)
    tile_j = jnp.where(valid, pick(starts) + k * tile, 0)
    tile_n = jnp.where(valid, jnp.clip(pick(counts) - k * tile, 0, tile), 0)
    lo_of_pair, hi_of_pair = [], []
    for lo in range(EXPERTS_PER_GROUP):
        for hi in range(lo + 1, EXPERTS_PER_GROUP):
            lo_of_pair.append(lo)
            hi_of_pair.append(hi)
    grp = class_ids // PAIRS_PER_GROUP
    class_lo = grp * EXPERTS_PER_GROUP + jnp.asarray(lo_of_pair * N_EXPERT_GROUPS, I32)
    class_hi = grp * EXPERTS_PER_GROUP + jnp.asarray(hi_of_pair * N_EXPERT_GROUPS, I32)
    return order, pick(class_lo), pick(class_hi), tile_j.astype(I32), tile_n.astype(I32)


def _final_kernel(x1_ref, ym_ref, p_ref, pnw_ref, wg_ref, bg_ref, wp_ref, ppnw_ref, fnw_ref, o_ref):
    x2 = _load_token_tiles(x1_ref) + _load_token_tiles(ym_ref)
    gate = _sigmoid(_dot(_rms(x2, pnw_ref[...]).astype(BF16), wg_ref[...]) + bg_ref[...])
    ple = _rms(_dot(p_ref[...].astype(BF16), wp_ref[...]), ppnw_ref[...])
    o_ref[...] = _rms(x2 + gate * ple, fnw_ref[...])


def _final(x1, ym, p2, pnw, wg, bg, wp, ppnw, fnw):
    t = p2.shape[0]
    tm = min(TOKEN_TILE, t)
    row = lambda w: pl.BlockSpec((tm, w), lambda i: (i, 0))
    weights = (pnw, wg, bg, wp, ppnw, fnw)
    return pl.pallas_call(
        _final_kernel,
        grid=(t // tm,),
        in_specs=[_token_tile_spec(tm), _token_tile_spec(tm), row(PLE_DIM)]
        + [_full(w.shape) for w in weights],
        out_specs=row(D_MODEL),
        out_shape=jax.ShapeDtypeStruct((t, D_MODEL), F32),
        compiler_params=pltpu.CompilerParams(dimension_semantics=("parallel",),
                                             vmem_limit_bytes=VMEM_LIMIT),
        name="final",
    )(x1, ym, p2, *weights)


def _pad_cols(w, width, offset=0):
    out = jnp.zeros((w.shape[0], width), w.dtype)
    return out.at[:, offset:offset + w.shape[1]].set(w)


def _rope_swap(w):
    half = MLA_ROPE_DIM // 2
    return jnp.concatenate([-w[..., half:], w[..., :half]], axis=-1)


def _layer(x2, p2, cos_t, sin_t, b, s, attn_norm_w, w_in, q_norm_w, w_uq, kv_norm_w, w_ukv,
           attn_out_norm_w, conv_w, conv_b, dt_bias, a_log, ssd_d, ssd_norm_w, w_o, ffn_norm_w,
           w_router_group, b_router_group, w_router_expert, b_router_expert, w_exp_gate,
           w_exp_up, w_exp_down, ple_norm_w, w_ple_gate, b_ple_gate, w_ple_proj, ple_post_norm_w):
    t = x2.shape[0]
    r1 = lambda v: v.reshape(1, -1).astype(F32)
    o1 = MLA_Q_RANK
    o2 = o1 + MLA_KV_RANK
    o3 = o2 + MLA_ROPE_DIM
    o4 = o3 + SSD_INNER
    o5 = o4 + SSD_XBC
    w_kr = w_in[:, o2:o3]
    wa = jnp.concatenate([w_in[:, :o2], _pad_cols(w_kr, HEAD_PAD, MLA_NOPE_DIM),
                          _pad_cols(_rope_swap(w_kr), HEAD_PAD, MLA_NOPE_DIM),
                          _pad_cols(w_in[:, o5:], LANES)], axis=1).astype(BF16)
    wz = w_in[:, o3:o4].astype(BF16)
    wxbc = w_in[:, o4:o5].astype(BF16)
    uq = w_uq.reshape(MLA_Q_RANK, MLA_HEADS, MLA_NOPE_DIM + MLA_ROPE_DIM)
    zq = jnp.zeros((MLA_Q_RANK, MLA_HEADS, HEAD_PAD - MLA_NOPE_DIM - MLA_ROPE_DIM), F32)
    wuq = jnp.concatenate([uq, zq], axis=-1).reshape(MLA_Q_RANK, -1).astype(BF16)
    wuqs = jnp.concatenate([jnp.zeros_like(uq[..., :MLA_NOPE_DIM]), _rope_swap(uq[..., MLA_NOPE_DIM:]),
                            zq], axis=-1).reshape(MLA_Q_RANK, -1).astype(BF16)
    ukv = w_ukv.reshape(MLA_KV_RANK, MLA_HEADS, MLA_NOPE_DIM + MLA_V_DIM)
    zk = jnp.zeros((MLA_KV_RANK, MLA_HEADS, HEAD_PAD - MLA_NOPE_DIM), F32)
    wuk = jnp.concatenate([ukv[..., :MLA_NOPE_DIM], zk], axis=-1).reshape(MLA_KV_RANK, -1).astype(BF16)
    wuv = ukv[..., MLA_NOPE_DIM:].reshape(MLA_KV_RANK, -1).T.astype(BF16)

    q, k, v, z, xbc, dt_raw = _inproj(x2, cos_t, sin_t, r1(attn_norm_w), wa, wz, wxbc,
                                      r1(q_norm_w), wuq, wuqs, r1(kv_norm_w), wuk, wuv)
    attn = _attention(q, k, v, b, s)

    head_of_lane = jnp.arange(SSD_INNER) // SSD_HEAD_DIM
    rows = jnp.arange(LANES)[:, None]
    e_mat = jnp.stack([(rows == d * SSD_HEADS + head_of_lane[None, :]) for d in range(SSD_DIRECTIONS)]
                      ).astype(BF16)
    skip = jnp.repeat(ssd_d.astype(F32), SSD_HEAD_DIM).reshape(1, -1)
    yf, yb = _ssd(xbc, dt_raw, conv_w.astype(F32), r1(conv_b), _pad_cols(r1(dt_bias), LANES),
                  _pad_cols(r1(a_log), LANES), skip, e_mat, b, s)

    wr = _pad_cols(jnp.concatenate([w_router_group, w_router_expert], axis=1).astype(F32), LANES)
    wr_hi, wr_lo, _ = _split3(wr)
    wr2 = jnp.concatenate([wr_hi, wr_lo], axis=1)
    br = _pad_cols(jnp.concatenate([r1(b_router_group), r1(b_router_expert)], axis=1), LANES)
    x1, route = _mixout(x2, attn, yf, yb, z, r1(attn_out_norm_w), r1(ssd_norm_w), w_o.astype(BF16),
                        r1(ffn_norm_w), wr2, br)

    order, tile_lo, tile_hi, tile_j, tile_n = _moe_plan(route, t)
    wgu = jnp.concatenate([w_exp_gate, w_exp_up], axis=-1).astype(BF16)
    ym = _moe(x1.reshape(t, ROW_TILES, LANES), order, tile_lo, tile_hi, tile_j, tile_n,
              r1(ffn_norm_w), wr_hi, br, wgu, w_exp_down.astype(BF16))
    ym = ym.reshape(ym.shape[0] * ROW_TILES, LANES)
    return x1, ym, (r1(ple_norm_w), w_ple_gate.astype(BF16), r1(b_ple_gate), w_ple_proj.astype(BF16),
                    r1(ple_post_norm_w))


def kernel(x, p, positions, attn_norm_w, w_in, q_norm_w, w_uq, kv_norm_w, w_ukv, attn_out_norm_w, conv_w, conv_b, dt_bias, a_log, ssd_d, ssd_norm_w, w_o, ffn_norm_w, w_router_group, b_router_group, w_router_expert, b_router_expert, w_exp_gate, w_exp_up, w_exp_down, ple_norm_w, w_ple_gate, b_ple_gate, w_ple_proj, ple_post_norm_w, final_norm_w):
    b, s, d = x.shape
    depth = p.shape[0]
    assert depth == 1, "the fused final stage assumes a single layer"
    t = b * s
    inv_freq = 1.0 / (ROPE_THETA ** (jnp.arange(0, MLA_ROPE_DIM, 2, dtype=F32) / MLA_ROPE_DIM))
    half = MLA_ROPE_DIM // 2
    ang = (positions.astype(F32).reshape(t, 1) * inv_freq).reshape(t * half // LANES, LANES)
    cos, sin = lax.optimization_barrier((jnp.cos(ang), jnp.sin(ang)))
    cos = cos.reshape(t, half)
    sin = sin.reshape(t, half)
    ones = jnp.ones((t, MLA_NOPE_DIM), F32)
    zeros = jnp.zeros((t, HEAD_PAD - MLA_NOPE_DIM - MLA_ROPE_DIM), F32)
    cos_t = jnp.concatenate([ones, cos, cos, zeros], axis=1)
    sin_t = jnp.concatenate([0.0 * ones, sin, sin, zeros], axis=1)

    x2 = x.reshape(t, d)
    i = 0
    x1, ym, (pnw, wg, bg, wp, ppnw) = _layer(
        x2, p[i].reshape(t, -1), cos_t, sin_t, b, s, attn_norm_w[i], w_in[i], q_norm_w[i], w_uq[i],
        kv_norm_w[i], w_ukv[i], attn_out_norm_w[i], conv_w[i], conv_b[i], dt_bias[i], a_log[i],
        ssd_d[i], ssd_norm_w[i], w_o[i], ffn_norm_w[i], w_router_group[i], b_router_group[i],
        w_router_expert[i], b_router_expert[i], w_exp_gate[i], w_exp_up[i], w_exp_down[i],
        ple_norm_w[i], w_ple_gate[i], b_ple_gate[i], w_ple_proj[i], ple_post_norm_w[i])
    out = _final(x1, ym, p[i].reshape(t, -1), pnw, wg, bg, wp, ppnw, final_norm_w.reshape(1, -1).astype(F32))
    return out.reshape(b, s, d)
```

```python
import functools
import math

import jax
import jax.numpy as jnp
from jax import lax
from jax.experimental import pallas as pl
from jax.experimental.pallas import tpu as pltpu

F32 = jnp.float32
BF16 = jnp.bfloat16
I32 = jnp.int32

D_MODEL = 1024
PLE_DIM = 256
NORM_EPS = 1e-6

MLA_HEADS = 8
MLA_Q_RANK = 256
MLA_KV_RANK = 128
MLA_NOPE_DIM = 64
MLA_ROPE_DIM = 32
MLA_V_DIM = 64
MLA_OUT = MLA_HEADS * MLA_V_DIM
ROPE_THETA = 10000.0
HEAD_PAD = 128
ONES_ROWS = 16

SSD_HEADS = 8
SSD_HEAD_DIM = 64
SSD_GROUPS = 2
SSD_HEADS_PER_GROUP = SSD_HEADS // SSD_GROUPS
SSD_STATE = 64
SSD_CONV = 5
SSD_CHUNK = 128
SSD_INNER = SSD_HEADS * SSD_HEAD_DIM
SSD_XBC = SSD_INNER + 2 * SSD_GROUPS * SSD_STATE
SSD_DIRECTIONS = 2
CONV_HALO = 8

N_EXPERT_GROUPS = 4
EXPERTS_PER_GROUP = 8
N_EXPERTS = N_EXPERT_GROUPS * EXPERTS_PER_GROUP
D_EXPERT = 256
PAIRS_PER_GROUP = EXPERTS_PER_GROUP * (EXPERTS_PER_GROUP - 1) // 2
N_CLASSES = N_EXPERT_GROUPS * PAIRS_PER_GROUP

LANES = 128
ROW_TILES = D_MODEL // LANES
VMEM_LIMIT = 48 * 1024 * 1024

TOKEN_TILE = 512
ATTN_Q_TILE = 512
ATTN_Q_SUB = 256
ATTN_PIPELINE = {True: (512, 3), False: (256, 4)}
ATTN_LOGIT_BOUND = 80.0
SSD_BLOCK = 256
MOE_TILE = 128


def _rms(x, w):
    ms = jnp.mean(x * x, axis=-1, keepdims=True)
    return x * lax.rsqrt(ms + NORM_EPS) * w


def _dot(a, b):
    return jnp.dot(a, b, preferred_element_type=F32)


def _dot_nt(a, b):
    return lax.dot_general(a, b, (((1,), (1,)), ((), ())), preferred_element_type=F32)


def _split3(x):
    x1 = x.astype(BF16)
    r1 = x - x1.astype(F32)
    x2 = r1.astype(BF16)
    x3 = (r1 - x2.astype(F32)).astype(BF16)
    return x1, x2, x3


def _sigmoid(x):
    return 1.0 / (1.0 + jnp.exp(-x))


def _softplus(x):
    return jnp.maximum(x, 0.0) + jnp.log(1.0 + jnp.exp(-jnp.abs(x)))


def _full(shape):
    nd = len(shape)
    return pl.BlockSpec(shape, lambda *_: (0,) * nd)


def _token_tile_spec(tm):
    return pl.BlockSpec((tm * ROW_TILES, LANES), lambda i: (i, 0))


def _store_token_tiles(ref, x):
    n = x.shape[0]
    for s in range(ROW_TILES):
        ref[pl.ds(s, n, stride=ROW_TILES), :] = x[:, LANES * s:LANES * (s + 1)]


def _load_token_tiles(ref):
    n = ref.shape[0] // ROW_TILES
    return jnp.concatenate([ref[pl.ds(s, n, stride=ROW_TILES), :] for s in range(ROW_TILES)], axis=1)


def _inproj_kernel(x_ref, xprev_ref, xnext_ref, nw_ref, wa_ref, wz_ref, wxbc_ref, cw_ref, cb_ref,
                   qnw_ref, wuq_ref, wuqs_ref, kvnw_ref, wuk_ref, wuv_ref, cos_ref, sin_ref,
                   q_out, k_out, v_out, z_out, act_out, dt_out, conv_ref, *, scale, tiles_per_seq):
    tm = x_ref.shape[0]
    h = _rms(x_ref[...], nw_ref[...]).astype(BF16)
    pa = _dot(h, wa_ref[...])
    z_out[...] = _dot(h, wz_ref[...]).astype(z_out.dtype)
    dt_out[...] = pa[:, MLA_Q_RANK + MLA_KV_RANK + 2 * HEAD_PAD:]

    pos = pl.program_id(0) % tiles_per_seq
    halo = lambda ref: _dot(_rms(ref[...], nw_ref[...]).astype(BF16), wxbc_ref[...])
    conv_ref[0:CONV_HALO, :] = jnp.where(pos > 0, halo(xprev_ref), 0.0)
    conv_ref[CONV_HALO:CONV_HALO + tm, :] = _dot(h, wxbc_ref[...])
    conv_ref[CONV_HALO + tm:2 * CONV_HALO + tm, :] = jnp.where(pos < tiles_per_seq - 1,
                                                               halo(xnext_ref), 0.0)
    acc = jnp.zeros((tm, SSD_XBC), F32) + cb_ref[...]
    base = CONV_HALO - SSD_CONV // 2
    for kk in range(SSD_CONV):
        acc = acc + cw_ref[kk:kk + 1, :] * conv_ref[base + kk:base + kk + tm, :]
    act_out[...] = (acc * _sigmoid(acc)).astype(act_out.dtype)

    cos = cos_ref[...]
    sin = sin_ref[...]
    o1 = MLA_Q_RANK
    o2 = o1 + MLA_KV_RANK
    cqn = _rms(pa[:, :o1], qnw_ref[...]).astype(BF16)
    ckvn = _rms(pa[:, o1:o2], kvnw_ref[...]).astype(BF16)
    q = _dot(cqn, wuq_ref[...])
    qs = _dot(cqn, wuqs_ref[...])
    kn = _dot(ckvn, wuk_ref[...])
    v_out[...] = _dot_nt(wuv_ref[...], ckvn).astype(v_out.dtype)
    kr = pa[:, o2:o2 + HEAD_PAD] * cos + pa[:, o2 + HEAD_PAD:o2 + 2 * HEAD_PAD] * sin
    for hh in range(MLA_HEADS):
        sl = slice(HEAD_PAD * hh, HEAD_PAD * (hh + 1))
        q_out[:, sl] = ((q[:, sl] * cos + qs[:, sl] * sin) * scale).astype(q_out.dtype)
        k_out[:, sl] = (kn[:, sl] + kr).astype(k_out.dtype)


def _inproj(x2, cos_t, sin_t, s, nw, wa, wz, wxbc, cw, cb, qnw, wuq, wuqs, kvnw, wuk, wuv):
    t = x2.shape[0]
    tm = min(TOKEN_TILE, s)
    hb = tm // CONV_HALO
    last = t // CONV_HALO - 1
    scale = (MLA_NOPE_DIM + MLA_ROPE_DIM) ** -0.5 * math.log2(math.e)
    row = lambda w: pl.BlockSpec((tm, w), lambda i: (i, 0))
    weights = (nw, wa, wz, wxbc, cw, cb, qnw, wuq, wuqs, kvnw, wuk, wuv)
    return pl.pallas_call(
        functools.partial(_inproj_kernel, scale=scale, tiles_per_seq=s // tm),
        grid=(t // tm,),
        in_specs=[row(D_MODEL),
                  pl.BlockSpec((CONV_HALO, D_MODEL), lambda i: (jnp.maximum(i * hb - 1, 0), 0)),
                  pl.BlockSpec((CONV_HALO, D_MODEL), lambda i: (jnp.minimum((i + 1) * hb, last), 0))]
        + [_full(w.shape) for w in weights] + [row(HEAD_PAD), row(HEAD_PAD)],
        out_specs=[row(MLA_HEADS * HEAD_PAD), row(MLA_HEADS * HEAD_PAD),
                   pl.BlockSpec((None, MLA_OUT, tm), lambda i: (i, 0, 0)),
                   row(SSD_INNER), row(SSD_XBC), row(LANES)],
        out_shape=[jax.ShapeDtypeStruct((t, MLA_HEADS * HEAD_PAD), BF16),
                   jax.ShapeDtypeStruct((t, MLA_HEADS * HEAD_PAD), BF16),
                   jax.ShapeDtypeStruct((t // tm, MLA_OUT, tm), BF16),
                   jax.ShapeDtypeStruct((t, SSD_INNER), BF16),
                   jax.ShapeDtypeStruct((t, SSD_XBC), BF16),
                   jax.ShapeDtypeStruct((t, LANES), F32)],
        scratch_shapes=[pltpu.VMEM((tm + 2 * CONV_HALO, SSD_XBC), F32)],
        compiler_params=pltpu.CompilerParams(dimension_semantics=("parallel",),
                                             vmem_limit_bytes=VMEM_LIMIT),
        name="inproj",
    )(x2, x2, x2, *weights, cos_t, sin_t)


def _attn_body(q_ref, k_ref, vt_ref, o_ref, bounded):
    tq = q_ref.shape[0]
    nc, _, tv = vt_ref.shape
    tk, ahead = ATTN_PIPELINE[bounded]
    ts = min(ATTN_Q_SUB, tq)
    ones = jnp.ones((ONES_ROWS, tk), BF16)
    streams = []
    for j in range(tq // ts):
        for a in range(2):
            hsl = slice(HEAD_PAD * a, HEAD_PAD * (a + 1))
            vsl = slice(MLA_V_DIM * a, MLA_V_DIM * (a + 1))
            streams.append([hsl, vsl, q_ref[j * ts:(j + 1) * ts, hsl],
                            jnp.full((1, ts), -1e30, F32),
                            jnp.zeros((MLA_V_DIM + ONES_ROWS, ts), F32)])
    units = [(c, i) for c in range(nc * tv // tk) for i in range(len(streams))]
    scores = {}

    def issue(u):
        c, i = units[u]
        scores[u] = _dot_nt(k_ref[c * tk:(c + 1) * tk, streams[i][0]], streams[i][2])

    for u in range(min(ahead, len(units))):
        issue(u)
    for u, (c, i) in enumerate(units):
        if u + ahead < len(units):
            issue(u + ahead)
        st = streams[i]
        _, vsl, _, m, acc = st
        s = scores.pop(u)
        blk, off = divmod(c * tk, tv)
        vt = jnp.concatenate([vt_ref[blk, vsl, off:off + tk], ones], axis=0)
        if bounded:
            st[4] = acc + _dot(vt, jnp.exp2(s).astype(BF16))
        else:
            m_new = jnp.maximum(m, jnp.max(s, axis=0, keepdims=True))
            p = jnp.exp2((s - m_new).astype(BF16))
            st[3] = m_new
            st[4] = acc * jnp.exp2(m - m_new) + _dot(vt, p)
    for j in range(tq // ts):
        halves = [st[4][:MLA_V_DIM] / st[4][MLA_V_DIM:MLA_V_DIM + 1] for st in streams[2 * j:2 * j + 2]]
        o_ref[j * ts:(j + 1) * ts, :] = jnp.concatenate(halves, axis=0).T.astype(o_ref.dtype)


def _attn_kernel(q_ref, k_ref, vt_ref, o_ref, kmax_ref):
    @pl.when(pl.program_id(2) == 0)
    def _():
        for a in range(2):
            kmax_ref[a] = jnp.max(jnp.abs(k_ref[:, HEAD_PAD * a:HEAD_PAD * (a + 1)].astype(F32)))

    bounded = None
    for a in range(2):
        q1 = jnp.sum(jnp.abs(q_ref[:, HEAD_PAD * a:HEAD_PAD * (a + 1)].astype(F32)), axis=1, keepdims=True)
        inside = jnp.max(q1) * kmax_ref[a] <= ATTN_LOGIT_BOUND
        bounded = inside if bounded is None else bounded & inside

    @pl.when(bounded)
    def _():
        _attn_body(q_ref, k_ref, vt_ref, o_ref, True)

    @pl.when(jnp.logical_not(bounded))
    def _():
        _attn_body(q_ref, k_ref, vt_ref, o_ref, False)


def _attention(q, k, vt, b, s):
    tq = min(ATTN_Q_TILE, s)
    nq = s // tq
    pairs = MLA_HEADS // 2
    tk = vt.shape[2]
    nc = s // tk
    return pl.pallas_call(
        _attn_kernel,
        grid=(b, pairs, nq),
        in_specs=[pl.BlockSpec((tq, 2 * HEAD_PAD), lambda bi, pi, qi: (bi * nq + qi, pi)),
                  pl.BlockSpec((s, 2 * HEAD_PAD), lambda bi, pi, qi: (bi, pi)),
                  pl.BlockSpec((nc, 2 * MLA_V_DIM, tk), lambda bi, pi, qi: (bi, pi, 0))],
        out_specs=pl.BlockSpec((tq, 2 * MLA_V_DIM), lambda bi, pi, qi: (bi * nq + qi, pi)),
        out_shape=jax.ShapeDtypeStruct((b * s, MLA_OUT), BF16),
        scratch_shapes=[pltpu.SMEM((2,), F32)],
        compiler_params=pltpu.CompilerParams(
            dimension_semantics=("parallel", "parallel", "arbitrary"),
            vmem_limit_bytes=VMEM_LIMIT),
        name="attention",
    )(q, k, vt)


def _ssd_chunk(act, dt_raw, direction, h_ref, dtbias, a_all, e_mat, skip):
    n = SSD_CHUNK
    gs = SSD_GROUPS * SSD_STATE
    xs16 = act[:, :SSD_INNER]
    bm16 = act[:, SSD_INNER:SSD_INNER + gs]
    cm16 = act[:, SSD_INNER + gs:SSD_INNER + 2 * gs]
    xs = xs16.astype(F32)
    bm = bm16.astype(F32)

    dt_all = _softplus(dt_raw + dtbias)
    a_mat = dt_all * a_all
    ri = lax.broadcasted_iota(I32, (n, n), 0)
    ci = lax.broadcasted_iota(I32, (n, n), 1)
    mask = (ci <= ri) if direction == 0 else (ci >= ri)
    tri = jnp.where(mask, 1.0, 0.0).astype(BF16)
    a1, a2, a3 = _split3(a_mat)
    cs = _dot(tri, a1) + _dot(tri, a2) + _dot(tri, a3)
    end = n - 1 if direction == 0 else 0
    cs_end = cs[end:end + 1, :]
    w_state = dt_all * jnp.exp(cs_end - cs)
    e_off = jnp.exp(cs)
    c_dec = jnp.broadcast_to(jnp.exp(cs_end), (8, LANES))
    stack = jnp.concatenate([w_state, e_off, c_dec], axis=0).astype(BF16)
    expd = _dot(stack, e_mat)
    ws_x = expd[0:n]
    eo_x = expd[n:2 * n]
    cd_x = expd[2 * n:2 * n + 1]

    cs_t = cs.T
    dt_t = dt_all.T
    bm_t = bm.T.astype(BF16)

    pieces = []
    for g in range(SSD_GROUPS):
        gsl = slice(SSD_STATE * g, SSD_STATE * (g + 1))
        cg = cm16[:, gsl]
        gmat = _dot_nt(cg, bm16[:, gsl])
        for r in range(SSD_HEADS_PER_GROUP):
            hh = g * SSD_HEADS_PER_GROUP + r
            c = direction * SSD_HEADS + hh
            seg = cs[:, c:c + 1] - cs_t[c:c + 1, :]
            lm = jnp.where(mask, jnp.exp(jnp.where(mask, seg, 0.0)), 0.0) * dt_t[c:c + 1, :]
            mh = (gmat * lm).astype(BF16)
            pieces.append(_dot(mh, xs16[:, SSD_HEAD_DIM * hh:SSD_HEAD_DIM * (hh + 1)]))
    y = jnp.concatenate(pieces, axis=1)

    w = SSD_HEADS_PER_GROUP * SSD_HEAD_DIM
    offs = []
    for g in range(SSD_GROUPS):
        lsl = slice(w * g, w * (g + 1))
        gsl = slice(SSD_STATE * g, SSD_STATE * (g + 1))
        h_g = h_ref[:, lsl]
        offs.append(_dot(cm16[:, gsl], h_g.astype(BF16)) * eo_x[:, lsl])
        xd = (xs[:, lsl] * ws_x[:, lsl]).astype(BF16)
        h_ref[:, lsl] = h_g * cd_x[:, lsl] + _dot(bm_t[gsl, :], xd)
    y = y + jnp.concatenate(offs, axis=1)
    if skip is not None:
        y = y + xs * skip
    return y


def _ssd_kernel(xf_ref, dtf_ref, xb_ref, dtb_ref, dtbias_ref, alog_ref, skip_ref, e_ref,
                yf_ref, yb_ref, hf_ref, hb_ref):
    i = pl.program_id(1)

    @pl.when(i == 0)
    def _():
        hf_ref[...] = jnp.zeros_like(hf_ref)
        hb_ref[...] = jnp.zeros_like(hb_ref)

    lane = lax.broadcasted_iota(I32, (1, LANES), 1)
    a_all = jnp.where(lane < SSD_DIRECTIONS * SSD_HEADS, -jnp.exp(alog_ref[...]), 0.0)
    dtbias = dtbias_ref[...]
    skip = skip_ref[...]
    nch = xf_ref.shape[0] // SSD_CHUNK

    for c in range(nch):
        rows = slice(SSD_CHUNK * c, SSD_CHUNK * (c + 1))
        y = _ssd_chunk(xf_ref[rows, :], dtf_ref[rows, :], 0, hf_ref, dtbias, a_all, e_ref[0], skip)
        yf_ref[rows, :] = y.astype(yf_ref.dtype)

    for c in reversed(range(nch)):
        rows = slice(SSD_CHUNK * c, SSD_CHUNK * (c + 1))
        y = _ssd_chunk(xb_ref[rows, :], dtb_ref[rows, :], 1, hb_ref, dtbias, a_all, e_ref[1], None)
        yb_ref[rows, :] = y.astype(yb_ref.dtype)


def _ssd(act, dt_raw, dtbias, alog, skip, e_mat, b, s):
    r = min(SSD_BLOCK, s)
    nb = s // r

    def cur(rev):
        return (lambda bi, i: (bi * nb + (nb - 1 - i), 0)) if rev else (lambda bi, i: (bi * nb + i, 0))

    def role(rev):
        return [pl.BlockSpec((r, SSD_XBC), cur(rev)), pl.BlockSpec((r, LANES), cur(rev))]

    consts = (dtbias, alog, skip, e_mat)
    return pl.pallas_call(
        _ssd_kernel,
        grid=(b, nb),
        in_specs=role(False) + role(True) + [_full(c.shape) for c in consts],
        out_specs=[pl.BlockSpec((r, SSD_INNER), cur(False)), pl.BlockSpec((r, SSD_INNER), cur(True))],
        out_shape=[jax.ShapeDtypeStruct((b * s, SSD_INNER), F32)] * 2,
        scratch_shapes=[pltpu.VMEM((SSD_STATE, SSD_INNER), F32),
                        pltpu.VMEM((SSD_STATE, SSD_INNER), F32)],
        compiler_params=pltpu.CompilerParams(dimension_semantics=("parallel", "arbitrary"),
                                             vmem_limit_bytes=VMEM_LIMIT),
        name="ssd",
    )(act, dt_raw, act, dt_raw, *consts)


def _mixout_kernel(x_ref, attn_ref, yf_ref, yb_ref, z_ref, anw_ref, snw_ref, wo_ref, fnw_ref,
                   wr_ref, br_ref, x1_out, route_out):
    attn = _rms(attn_ref[...].astype(F32), anw_ref[...])
    z = z_ref[...].astype(F32)
    y = (yf_ref[...] + yb_ref[...]) * (z * _sigmoid(z))
    y = _rms(y, snw_ref[...])
    mix = jnp.concatenate([attn, y], axis=1).astype(BF16)
    x1 = x_ref[...] + _dot(mix, wo_ref[...])
    _store_token_tiles(x1_out, x1)

    h = _rms(x1, fnw_ref[...])
    h1, h2, _ = _split3(h)
    two = _dot(h1, wr_ref[...]) + _dot(h2, wr_ref[...])
    logits = two[:, :LANES] + two[:, LANES:] + br_ref[...]

    lane = lax.broadcasted_iota(I32, logits.shape, 1)
    ninf = -jnp.inf
    big = 4 * LANES

    def argmax_first(vals):
        vmax = jnp.max(vals, axis=-1, keepdims=True)
        idx = jnp.min(jnp.where(vals == vmax, lane, big), axis=-1, keepdims=True)
        return vmax, idx

    _, gidx = argmax_first(jnp.where(lane < N_EXPERT_GROUPS, logits, ninf))
    lo_lane = N_EXPERT_GROUPS + EXPERTS_PER_GROUP * gidx
    sl = jnp.where((lane >= lo_lane) & (lane < lo_lane + EXPERTS_PER_GROUP), logits, ninf)
    _, i1 = argmax_first(sl)
    _, i2 = argmax_first(jnp.where(lane == i1, ninf, sl))
    elo = jnp.minimum(i1, i2) - lo_lane
    ehi = jnp.maximum(i1, i2) - lo_lane
    pair = (elo * (2 * EXPERTS_PER_GROUP - 1 - elo)) // 2 + (ehi - elo - 1)
    cls = gidx * PAIRS_PER_GROUP + pair
    route_out[...] = jnp.broadcast_to(cls.astype(F32), logits.shape)


def _mixout(x2, attn, yf, yb, z, anw, snw, wo, fnw, wr, br):
    t = x2.shape[0]
    tm = min(TOKEN_TILE, t)
    row = lambda w: pl.BlockSpec((tm, w), lambda i: (i, 0))
    weights = (anw, snw, wo, fnw, wr, br)
    return pl.pallas_call(
        _mixout_kernel,
        grid=(t // tm,),
        in_specs=[row(D_MODEL), row(MLA_OUT), row(SSD_INNER), row(SSD_INNER), row(SSD_INNER)]
        + [_full(w.shape) for w in weights],
        out_specs=[_token_tile_spec(tm), row(LANES)],
        out_shape=[jax.ShapeDtypeStruct((t * ROW_TILES, LANES), F32),
                   jax.ShapeDtypeStruct((t, LANES), F32)],
        compiler_params=pltpu.CompilerParams(dimension_semantics=("parallel",),
                                             vmem_limit_bytes=VMEM_LIMIT),
        name="mixout",
    )(x2, attn, yf, yb, z, *weights)


def _moe_kernel(tile_lo_ref, tile_hi_ref, tile_j_ref, tile_n_ref, order_ref,
                x1_hbm, fnw_ref, wr_ref, br_ref, wgu_lo_ref, wdn_lo_ref, wgu_hi_ref, wdn_hi_ref,
                y_hbm, xbuf, obuf, sem_in, sem_out):
    t = pl.program_id(0)
    tile = xbuf.shape[1] // ROW_TILES
    n_tok = y_hbm.shape[0] - 2 * tile
    slot = t % 2

    def token_rows(r):
        return pl.ds(pl.multiple_of(r * ROW_TILES, ROW_TILES), ROW_TILES)

    def gather_start(tt, sl, inline):
        j0 = tile_j_ref[tt]

        def one(r, priority=0):
            tok = order_ref[j0 + r]
            pltpu.make_async_copy(x1_hbm.at[tok], xbuf.at[sl, token_rows(r)],
                                  sem_in.at[sl]).start(priority=priority)

        if inline:
            for r in range(tile):
                one(r, 1)
        else:
            def body(r, c):
                one(r)
                return c
            lax.fori_loop(0, tile, body, 0, unroll=8)

    def gather_wait(sl):
        pltpu.make_async_copy(xbuf.at[sl], xbuf.at[sl], sem_in.at[sl]).wait()

    def scatter_start(j0, n, sl, inline):
        def one(r, priority=0):
            tok = jnp.where(r < n, order_ref[j0 + r], n_tok + sl * tile + r)
            pltpu.make_async_copy(obuf.at[sl, token_rows(r)], y_hbm.at[tok],
                                  sem_out.at[sl]).start(priority=priority)

        if inline:
            for r in range(tile):
                one(r, r % 2)
        else:
            def body(r, c):
                one(r)
                return c
            lax.fori_loop(0, tile, body, 0, unroll=8)

    def scatter_wait(sl):
        pltpu.make_async_copy(obuf.at[sl], obuf.at[sl], sem_out.at[sl]).wait()

    prev = jnp.maximum(t - 1, 0)
    valid = tile_n_ref[t] > 0
    prev_valid = (t > 0) & (tile_n_ref[prev] > 0)

    @pl.when(t == 0)
    def _():
        obuf[...] = jnp.zeros_like(obuf)
        scatter_start(0, 0, 0, inline=False)
        scatter_wait(0)

        @pl.when(valid)
        def _():
            gather_start(t, slot, inline=False)

    @pl.when(jnp.logical_not(valid) & prev_valid)
    def _():
        gather_wait(slot)
        scatter_start(tile_j_ref[prev], tile_n_ref[prev], 1 - slot, inline=False)
        scatter_wait(1 - slot)

    @pl.when(valid)
    def _():
        gather_wait(slot)
        gather_start(t + 1, 1 - slot, inline=True)
        scatter_start(tile_j_ref[prev], jnp.where(t > 0, tile_n_ref[prev], 0), 1 - slot, inline=True)

        h = _rms(_load_token_tiles(xbuf.at[slot]), fnw_ref[...]).astype(BF16)
        logits = _dot(h, wr_ref[...]) + br_ref[...]
        lane = lax.broadcasted_iota(I32, logits.shape, 1)
        gl = jnp.where(lane < N_EXPERT_GROUPS, logits, -jnp.inf)
        gweight = 1.0 / jnp.sum(jnp.exp(gl - jnp.max(gl, axis=-1, keepdims=True)), axis=-1, keepdims=True)
        pick = lambda e: jnp.sum(jnp.where(lane == N_EXPERT_GROUPS + e, logits, 0.0), axis=-1, keepdims=True)
        l_lo = pick(tile_lo_ref[t])
        l_hi = pick(tile_hi_ref[t])
        wts = (gweight / (1.0 + jnp.exp(l_hi - l_lo)), gweight / (1.0 + jnp.exp(l_lo - l_hi)))
        acc = None
        for half, wgu_ref, wdn_ref in ((0, wgu_lo_ref, wdn_lo_ref), (1, wgu_hi_ref, wdn_hi_ref)):
            gu = _dot(h, wgu_ref[...])
            g = gu[:, :D_EXPERT]
            he = (g * _sigmoid(g) * gu[:, D_EXPERT:] * wts[half]).astype(BF16)
            d = _dot(he, wdn_ref[...])
            acc = d if acc is None else acc + d
        _store_token_tiles(obuf.at[slot], acc)
        scatter_wait(1 - slot)


def _moe(x1t, order, tile_lo, tile_hi, tile_j, tile_n, fnw, wr, br, wgu, wdn):
    t = x1t.shape[0]
    n_tiles = tile_lo.shape[0]
    tile = MOE_TILE
    wspec = lambda shape, which: pl.BlockSpec(
        (None,) + shape, lambda i, lo, hi, tj, tn, od: ((lo, hi)[which][i], 0, 0))
    grid_spec = pltpu.PrefetchScalarGridSpec(
        num_scalar_prefetch=5,
        grid=(n_tiles,),
        in_specs=[pl.BlockSpec(memory_space=pl.ANY),
                  pl.BlockSpec((1, D_MODEL), lambda i, *_: (0, 0)),
                  pl.BlockSpec((D_MODEL, LANES), lambda i, *_: (0, 0)),
                  pl.BlockSpec((1, LANES), lambda i, *_: (0, 0)),
                  wspec((D_MODEL, 2 * D_EXPERT), 0), wspec((D_EXPERT, D_MODEL), 0),
                  wspec((D_MODEL, 2 * D_EXPERT), 1), wspec((D_EXPERT, D_MODEL), 1)],
        out_specs=pl.BlockSpec(memory_space=pl.ANY),
        scratch_shapes=[pltpu.VMEM((2, tile * ROW_TILES, LANES), F32),
                        pltpu.VMEM((2, tile * ROW_TILES, LANES), F32),
                        pltpu.SemaphoreType.DMA((2,)), pltpu.SemaphoreType.DMA((2,))],
    )
    return pl.pallas_call(
        _moe_kernel,
        grid_spec=grid_spec,
        out_shape=jax.ShapeDtypeStruct((t + 2 * tile, ROW_TILES, LANES), F32),
        compiler_params=pltpu.CompilerParams(dimension_semantics=("arbitrary",),
                                             vmem_limit_bytes=VMEM_LIMIT),
        name="moe",
    )(tile_lo, tile_hi, tile_j, tile_n, order, x1t, fnw, wr, br, wgu, wdn, wgu, wdn)


def _moe_plan(route, t):
    tile = MOE_TILE
    n_tiles = t // tile + N_CLASSES + 1
    cls = route[:, 0].astype(I32)
    _, order = lax.sort((cls, jnp.arange(t, dtype=I32)), num_keys=1, is_stable=True)
    order = jnp.concatenate([order, jnp.zeros((tile,), I32)])
    class_ids = jnp.arange(N_CLASSES, dtype=I32)
    counts = jnp.sum((cls[:, None] == class_ids[None, :]).astype(I32), axis=0)
    tiles_per = (counts + tile - 1) // tile
    tile_end = jnp.cumsum(tiles_per)
    tile_begin = tile_end - tiles_per
    starts = jnp.cumsum(counts) - counts
    t_idx = jnp.arange(n_tiles, dtype=I32)
    tile_cls = jnp.sum((tile_end[None, :] <= t_idx[:, None]).astype(I32), axis=1)
    valid = t_idx < tile_end[-1]
    last_cls = jnp.max(jnp.where(counts > 0, class_ids, 0))
    tile_cls = jnp.where(valid, tile_cls, last_cls)
    onehot = (tile_cls[:, None] == class_ids[None, :]).astype(I32)
    pick = lambda v: jnp.sum(onehot * v[None, :], axis=1)
    k = t_idx - pick(tile_begin)
    tile_j = jnp.where(valid, pick(starts) + k * tile, 0)
    tile_n = jnp.where(valid, jnp.clip(pick(counts) - k * tile, 0, tile), 0)
    lo_of_pair, hi_of_pair = [], []
    for lo in range(EXPERTS_PER_GROUP):
        for hi in range(lo + 1, EXPERTS_PER_GROUP):
            lo_of_pair.append(lo)
            hi_of_pair.append(hi)
    grp = class_ids // PAIRS_PER_GROUP
    class_lo = grp * EXPERTS_PER_GROUP + jnp.asarray(lo_of_pair * N_EXPERT_GROUPS, I32)
    class_hi = grp * EXPERTS_PER_GROUP + jnp.asarray(hi_of_pair * N_EXPERT_GROUPS, I32)
    return order, pick(class_lo), pick(class_hi), tile_j.astype(I32), tile_n.astype(I32)


def _final_kernel(x1_ref, ym_ref, p_ref, pnw_ref, wg_ref, bg_ref, wp_ref, ppnw_ref, fnw_ref, o_ref):
    x2 = _load_token_tiles(x1_ref) + _load_token_tiles(ym_ref)
    gate = _sigmoid(_dot(_rms(x2, pnw_ref[...]).astype(BF16), wg_ref[...]) + bg_ref[...])
    ple = _rms(_dot(p_ref[...].astype(BF16), wp_ref[...]), ppnw_ref[...])
    o_ref[...] = _rms(x2 + gate * ple, fnw_ref[...])


def _final(x1, ym, p2, pnw, wg, bg, wp, ppnw, fnw):
    t = p2.shape[0]
    tm = min(TOKEN_TILE, t)
    row = lambda w: pl.BlockSpec((tm, w), lambda i: (i, 0))
    weights = (pnw, wg, bg, wp, ppnw, fnw)
    return pl.pallas_call(
        _final_kernel,
        grid=(t // tm,),
        in_specs=[_token_tile_spec(tm), _token_tile_spec(tm), row(PLE_DIM)]
        + [_full(w.shape) for w in weights],
        out_specs=row(D_MODEL),
        out_shape=jax.ShapeDtypeStruct((t, D_MODEL), F32),
        compiler_params=pltpu.CompilerParams(dimension_semantics=("parallel",),
                                             vmem_limit_bytes=VMEM_LIMIT),
        name="final",
    )(x1, ym, p2, *weights)


def _pad_cols(w, width, offset=0):
    out = jnp.zeros((w.shape[0], width), w.dtype)
    return out.at[:, offset:offset + w.shape[1]].set(w)


def _rope_swap(w):
    half = MLA_ROPE_DIM // 2
    return jnp.concatenate([-w[..., half:], w[..., :half]], axis=-1)


def _layer(x2, p2, cos_t, sin_t, b, s, attn_norm_w, w_in, q_norm_w, w_uq, kv_norm_w, w_ukv,
           attn_out_norm_w, conv_w, conv_b, dt_bias, a_log, ssd_d, ssd_norm_w, w_o, ffn_norm_w,
           w_router_group, b_router_group, w_router_expert, b_router_expert, w_exp_gate,
           w_exp_up, w_exp_down, ple_norm_w, w_ple_gate, b_ple_gate, w_ple_proj, ple_post_norm_w):
    t = x2.shape[0]
    r1 = lambda v: v.reshape(1, -1).astype(F32)
    o1 = MLA_Q_RANK
    o2 = o1 + MLA_KV_RANK
    o3 = o2 + MLA_ROPE_DIM
    o4 = o3 + SSD_INNER
    o5 = o4 + SSD_XBC
    w_kr = w_in[:, o2:o3]
    wa = jnp.concatenate([w_in[:, :o2], _pad_cols(w_kr, HEAD_PAD, MLA_NOPE_DIM),
                          _pad_cols(_rope_swap(w_kr), HEAD_PAD, MLA_NOPE_DIM),
                          _pad_cols(w_in[:, o5:], LANES)], axis=1).astype(BF16)
    wz = w_in[:, o3:o4].astype(BF16)
    wxbc = w_in[:, o4:o5].astype(BF16)
    uq = w_uq.reshape(MLA_Q_RANK, MLA_HEADS, MLA_NOPE_DIM + MLA_ROPE_DIM)
    zq = jnp.zeros((MLA_Q_RANK, MLA_HEADS, HEAD_PAD - MLA_NOPE_DIM - MLA_ROPE_DIM), F32)
    wuq = jnp.concatenate([uq, zq], axis=-1).reshape(MLA_Q_RANK, -1).astype(BF16)
    wuqs = jnp.concatenate([jnp.zeros_like(uq[..., :MLA_NOPE_DIM]), _rope_swap(uq[..., MLA_NOPE_DIM:]),
                            zq], axis=-1).reshape(MLA_Q_RANK, -1).astype(BF16)
    ukv = w_ukv.reshape(MLA_KV_RANK, MLA_HEADS, MLA_NOPE_DIM + MLA_V_DIM)
    zk = jnp.zeros((MLA_KV_RANK, MLA_HEADS, HEAD_PAD - MLA_NOPE_DIM), F32)
    wuk = jnp.concatenate([ukv[..., :MLA_NOPE_DIM], zk], axis=-1).reshape(MLA_KV_RANK, -1).astype(BF16)
    wuv = ukv[..., MLA_NOPE_DIM:].reshape(MLA_KV_RANK, -1).T.astype(BF16)

    q, k, v, z, act, dt_raw = _inproj(x2, cos_t, sin_t, s, r1(attn_norm_w), wa, wz, wxbc,
                                      conv_w.astype(F32), r1(conv_b), r1(q_norm_w), wuq, wuqs,
                                      r1(kv_norm_w), wuk, wuv)
    attn = _attention(q, k, v, b, s)

    head_of_lane = jnp.arange(SSD_INNER) // SSD_HEAD_DIM
    rows = jnp.arange(LANES)[:, None]
    e_mat = jnp.stack([(rows == d * SSD_HEADS + head_of_lane[None, :]) for d in range(SSD_DIRECTIONS)]
                      ).astype(BF16)
    skip = jnp.repeat(ssd_d.astype(F32), SSD_HEAD_DIM).reshape(1, -1)
    yf, yb = _ssd(act, dt_raw, _pad_cols(r1(dt_bias), LANES), _pad_cols(r1(a_log), LANES), skip,
                  e_mat, b, s)

    wr = _pad_cols(jnp.concatenate([w_router_group, w_router_expert], axis=1).astype(F32), LANES)
    wr_hi, wr_lo, _ = _split3(wr)
    wr2 = jnp.concatenate([wr_hi, wr_lo], axis=1)
    br = _pad_cols(jnp.concatenate([r1(b_router_group), r1(b_router_expert)], axis=1), LANES)
    x1, route = _mixout(x2, attn, yf, yb, z, r1(attn_out_norm_w), r1(ssd_norm_w), w_o.astype(BF16),
                        r1(ffn_norm_w), wr2, br)

    order, tile_lo, tile_hi, tile_j, tile_n = _moe_plan(route, t)
    wgu = jnp.concatenate([w_exp_gate, w_exp_up], axis=-1).astype(BF16)
    ym = _moe(x1.reshape(t, ROW_TILES, LANES), order, tile_lo, tile_hi, tile_j, tile_n,
              r1(ffn_norm_w), wr_hi, br, wgu, w_exp_down.astype(BF16))
    ym = ym.reshape(ym.shape[0] * ROW_TILES, LANES)
    return x1, ym, (r1(ple_norm_w), w_ple_gate.astype(BF16), r1(b_ple_gate), w_ple_proj.astype(BF16),
                    r1(ple_post_norm_w))


def kernel(x, p, positions, attn_norm_w, w_in, q_norm_w, w_uq, kv_norm_w, w_ukv, attn_out_norm_w, conv_w, conv_b, dt_bias, a_log, ssd_d, ssd_norm_w, w_o, ffn_norm_w, w_router_group, b_router_group, w_router_expert, b_router_expert, w_exp_gate, w_exp_up, w_exp_down, ple_norm_w, w_ple_gate, b_ple_gate, w_ple_proj, ple_post_norm_w, final_norm_w):
    b, s, d = x.shape
    depth = p.shape[0]
    assert depth == 1, "the fused final stage assumes a single layer"
    t = b * s
    inv_freq = 1.0 / (ROPE_THETA ** (jnp.arange(0, MLA_ROPE_DIM, 2, dtype=F32) / MLA_ROPE_DIM))
    half = MLA_ROPE_DIM // 2
    ang = (positions.astype(F32).reshape(t, 1) * inv_freq).reshape(t * half // LANES, LANES)
    cos, sin = lax.optimization_barrier((jnp.cos(ang), jnp.sin(ang)))
    cos = cos.reshape(t, half)
    sin = sin.reshape(t, half)
    ones = jnp.ones((t, MLA_NOPE_DIM), F32)
    zeros = jnp.zeros((t, HEAD_PAD - MLA_NOPE_DIM - MLA_ROPE_DIM), F32)
    cos_t = jnp.concatenate([ones, cos, cos, zeros], axis=1)
    sin_t = jnp.concatenate([0.0 * ones, sin, sin, zeros], axis=1)

    x2 = x.reshape(t, d)
    i = 0
    x1, ym, (pnw, wg, bg, wp, ppnw) = _layer(
        x2, p[i].reshape(t, -1), cos_t, sin_t, b, s, attn_norm_w[i], w_in[i], q_norm_w[i], w_uq[i],
        kv_norm_w[i], w_ukv[i], attn_out_norm_w[i], conv_w[i], conv_b[i], dt_bias[i], a_log[i],
        ssd_d[i], ssd_norm_w[i], w_o[i], ffn_norm_w[i], w_router_group[i], b_router_group[i],
        w_router_expert[i], b_router_expert[i], w_exp_gate[i], w_exp_up[i], w_exp_down[i],
        ple_norm_w[i], w_ple_gate[i], b_ple_gate[i], w_ple_proj[i], ple_post_norm_w[i])
    out = _final(x1, ym, p[i].reshape(t, -1), pnw, wg, bg, wp, ppnw, final_norm_w.reshape(1, -1).astype(F32))
    return out.reshape(b, s, d)
```

```python
import functools
import math

import jax
import jax.numpy as jnp
from jax import lax
from jax.experimental import pallas as pl
from jax.experimental.pallas import tpu as pltpu

F32 = jnp.float32
BF16 = jnp.bfloat16
I32 = jnp.int32

D_MODEL = 1024
PLE_DIM = 256
NORM_EPS = 1e-6

MLA_HEADS = 8
MLA_Q_RANK = 256
MLA_KV_RANK = 128
MLA_NOPE_DIM = 64
MLA_ROPE_DIM = 32
MLA_V_DIM = 64
MLA_OUT = MLA_HEADS * MLA_V_DIM
ROPE_THETA = 10000.0
HEAD_PAD = 128
ONES_ROWS = 16

SSD_HEADS = 8
SSD_HEAD_DIM = 64
SSD_GROUPS = 2
SSD_HEADS_PER_GROUP = SSD_HEADS // SSD_GROUPS
SSD_STATE = 64
SSD_CONV = 5
SSD_CHUNK = 128
SSD_INNER = SSD_HEADS * SSD_HEAD_DIM
SSD_XBC = SSD_INNER + 2 * SSD_GROUPS * SSD_STATE
SSD_DIRECTIONS = 2
CONV_HALO = 8

N_EXPERT_GROUPS = 4
EXPERTS_PER_GROUP = 8
N_EXPERTS = N_EXPERT_GROUPS * EXPERTS_PER_GROUP
D_EXPERT = 256
PAIRS_PER_GROUP = EXPERTS_PER_GROUP * (EXPERTS_PER_GROUP - 1) // 2
N_CLASSES = N_EXPERT_GROUPS * PAIRS_PER_GROUP
ROUTE_ROWS = 48

LANES = 128
ROW_TILES = D_MODEL // LANES
VMEM_LIMIT = 48 * 1024 * 1024

TOKEN_TILE = 512
ATTN_Q_TILE = 512
ATTN_Q_SUB = 256
ATTN_PIPELINE = {True: (512, 3), False: (256, 4)}
ATTN_LOGIT_BOUND = 80.0
SSD_BLOCK = 256
MOE_TILE = 128


def _rms(x, w):
    ms = jnp.mean(x * x, axis=-1, keepdims=True)
    return x * lax.rsqrt(ms + NORM_EPS) * w


def _dot(a, b):
    return jnp.dot(a, b, preferred_element_type=F32)


def _dot_nt(a, b):
    return lax.dot_general(a, b, (((1,), (1,)), ((), ())), preferred_element_type=F32)


def _split3(x):
    x1 = x.astype(BF16)
    r1 = x - x1.astype(F32)
    x2 = r1.astype(BF16)
    x3 = (r1 - x2.astype(F32)).astype(BF16)
    return x1, x2, x3


def _sigmoid(x):
    return 1.0 / (1.0 + jnp.exp(-x))


def _softplus(x):
    return jnp.maximum(x, 0.0) + jnp.log(1.0 + jnp.exp(-jnp.abs(x)))


def _full(shape):
    nd = len(shape)
    return pl.BlockSpec(shape, lambda *_: (0,) * nd)


def _token_tile_spec(tm):
    return pl.BlockSpec((tm * ROW_TILES, LANES), lambda i: (i, 0))


def _store_token_tiles(ref, x):
    n = x.shape[0]
    for s in range(ROW_TILES):
        ref[pl.ds(s, n, stride=ROW_TILES), :] = x[:, LANES * s:LANES * (s + 1)]


def _load_token_tiles(ref):
    n = ref.shape[0] // ROW_TILES
    return jnp.concatenate([ref[pl.ds(s, n, stride=ROW_TILES), :] for s in range(ROW_TILES)], axis=1)


def _inproj_kernel(x_ref, xprev_ref, xnext_ref, nw_ref, wa_ref, wz_ref, wxbc_ref, cw_ref, cb_ref,
                   qnw_ref, wuq_ref, wuqs_ref, kvnw_ref, wuk_ref, wuv_ref, cos_ref, sin_ref,
                   q_out, k_out, v_out, z_out, act_out, dt_out, conv_ref, *, scale, tiles_per_seq):
    tm = x_ref.shape[0]
    h = _rms(x_ref[...], nw_ref[...]).astype(BF16)
    pa = _dot(h, wa_ref[...])
    z_out[...] = _dot(h, wz_ref[...]).astype(z_out.dtype)
    dt_out[...] = pa[:, MLA_Q_RANK + MLA_KV_RANK + 2 * HEAD_PAD:]

    pos = pl.program_id(0) % tiles_per_seq
    halo = lambda ref: _dot(_rms(ref[...], nw_ref[...]).astype(BF16), wxbc_ref[...])
    conv_ref[0:CONV_HALO, :] = jnp.where(pos > 0, halo(xprev_ref), 0.0)
    conv_ref[CONV_HALO:CONV_HALO + tm, :] = _dot(h, wxbc_ref[...])
    conv_ref[CONV_HALO + tm:2 * CONV_HALO + tm, :] = jnp.where(pos < tiles_per_seq - 1,
                                                               halo(xnext_ref), 0.0)
    acc = jnp.zeros((tm, SSD_XBC), F32) + cb_ref[...]
    base = CONV_HALO - SSD_CONV // 2
    for kk in range(SSD_CONV):
        acc = acc + cw_ref[kk:kk + 1, :] * conv_ref[base + kk:base + kk + tm, :]
    act_out[...] = (acc * _sigmoid(acc)).astype(act_out.dtype)

    cos = cos_ref[...]
    sin = sin_ref[...]
    o1 = MLA_Q_RANK
    o2 = o1 + MLA_KV_RANK
    cqn = _rms(pa[:, :o1], qnw_ref[...]).astype(BF16)
    ckvn = _rms(pa[:, o1:o2], kvnw_ref[...]).astype(BF16)
    q = _dot(cqn, wuq_ref[...])
    qs = _dot(cqn, wuqs_ref[...])
    kn = _dot(ckvn, wuk_ref[...])
    v_out[...] = _dot_nt(wuv_ref[...], ckvn).astype(v_out.dtype)
    kr = pa[:, o2:o2 + HEAD_PAD] * cos + pa[:, o2 + HEAD_PAD:o2 + 2 * HEAD_PAD] * sin
    for hh in range(MLA_HEADS):
        sl = slice(HEAD_PAD * hh, HEAD_PAD * (hh + 1))
        q_out[:, sl] = ((q[:, sl] * cos + qs[:, sl] * sin) * scale).astype(q_out.dtype)
        k_out[:, sl] = (kn[:, sl] + kr).astype(k_out.dtype)


def _inproj(x2, cos_t, sin_t, s, nw, wa, wz, wxbc, cw, cb, qnw, wuq, wuqs, kvnw, wuk, wuv):
    t = x2.shape[0]
    tm = min(TOKEN_TILE, s)
    hb = tm // CONV_HALO
    last = t // CONV_HALO - 1
    scale = (MLA_NOPE_DIM + MLA_ROPE_DIM) ** -0.5 * math.log2(math.e)
    row = lambda w: pl.BlockSpec((tm, w), lambda i: (i, 0))
    weights = (nw, wa, wz, wxbc, cw, cb, qnw, wuq, wuqs, kvnw, wuk, wuv)
    return pl.pallas_call(
        functools.partial(_inproj_kernel, scale=scale, tiles_per_seq=s // tm),
        grid=(t // tm,),
        in_specs=[row(D_MODEL),
                  pl.BlockSpec((CONV_HALO, D_MODEL), lambda i: (jnp.maximum(i * hb - 1, 0), 0)),
                  pl.BlockSpec((CONV_HALO, D_MODEL), lambda i: (jnp.minimum((i + 1) * hb, last), 0))]
        + [_full(w.shape) for w in weights] + [row(HEAD_PAD), row(HEAD_PAD)],
        out_specs=[row(MLA_HEADS * HEAD_PAD), row(MLA_HEADS * HEAD_PAD),
                   pl.BlockSpec((None, MLA_OUT, tm), lambda i: (i, 0, 0)),
                   row(SSD_INNER), row(SSD_XBC), row(LANES)],
        out_shape=[jax.ShapeDtypeStruct((t, MLA_HEADS * HEAD_PAD), BF16),
                   jax.ShapeDtypeStruct((t, MLA_HEADS * HEAD_PAD), BF16),
                   jax.ShapeDtypeStruct((t // tm, MLA_OUT, tm), BF16),
                   jax.ShapeDtypeStruct((t, SSD_INNER), BF16),
                   jax.ShapeDtypeStruct((t, SSD_XBC), BF16),
                   jax.ShapeDtypeStruct((t, LANES), F32)],
        scratch_shapes=[pltpu.VMEM((tm + 2 * CONV_HALO, SSD_XBC), F32)],
        compiler_params=pltpu.CompilerParams(dimension_semantics=("parallel",),
                                             vmem_limit_bytes=VMEM_LIMIT),
        name="inproj",
    )(x2, x2, x2, *weights, cos_t, sin_t)


def _attn_body(q_ref, k_ref, vt_ref, o_ref, bounded):
    tq = q_ref.shape[0]
    nc, _, tv = vt_ref.shape
    tk, ahead = ATTN_PIPELINE[bounded]
    ts = min(ATTN_Q_SUB, tq)
    ones = jnp.ones((ONES_ROWS, tk), BF16)
    streams = []
    for j in range(tq // ts):
        for a in range(2):
            hsl = slice(HEAD_PAD * a, HEAD_PAD * (a + 1))
            vsl = slice(MLA_V_DIM * a, MLA_V_DIM * (a + 1))
            streams.append([hsl, vsl, q_ref[j * ts:(j + 1) * ts, hsl],
                            jnp.full((1, ts), -1e30, F32),
                            jnp.zeros((MLA_V_DIM + ONES_ROWS, ts), F32)])
    units = [(c, i) for c in range(nc * tv // tk) for i in range(len(streams))]
    scores = {}

    def issue(u):
        c, i = units[u]
        scores[u] = _dot_nt(k_ref[c * tk:(c + 1) * tk, streams[i][0]], streams[i][2])

    for u in range(min(ahead, len(units))):
        issue(u)
    for u, (c, i) in enumerate(units):
        if u + ahead < len(units):
            issue(u + ahead)
        st = streams[i]
        _, vsl, _, m, acc = st
        s = scores.pop(u)
        blk, off = divmod(c * tk, tv)
        vt = jnp.concatenate([vt_ref[blk, vsl, off:off + tk], ones], axis=0)
        if bounded:
            st[4] = acc + _dot(vt, jnp.exp2(s).astype(BF16))
        else:
            m_new = jnp.maximum(m, jnp.max(s, axis=0, keepdims=True))
            p = jnp.exp2((s - m_new).astype(BF16))
            st[3] = m_new
            st[4] = acc * jnp.exp2(m - m_new) + _dot(vt, p)
    for j in range(tq // ts):
        halves = [st[4][:MLA_V_DIM] / st[4][MLA_V_DIM:MLA_V_DIM + 1] for st in streams[2 * j:2 * j + 2]]
        o_ref[j * ts:(j + 1) * ts, :] = jnp.concatenate(halves, axis=0).T.astype(o_ref.dtype)


def _attn_kernel(q_ref, k_ref, vt_ref, o_ref, kmax_ref):
    @pl.when(pl.program_id(2) == 0)
    def _():
        for a in range(2):
            kmax_ref[a] = jnp.max(jnp.abs(k_ref[:, HEAD_PAD * a:HEAD_PAD * (a + 1)].astype(F32)))

    bounded = None
    for a in range(2):
        q1 = jnp.sum(jnp.abs(q_ref[:, HEAD_PAD * a:HEAD_PAD * (a + 1)].astype(F32)), axis=1, keepdims=True)
        inside = jnp.max(q1) * kmax_ref[a] <= ATTN_LOGIT_BOUND
        bounded = inside if bounded is None else bounded & inside

    @pl.when(bounded)
    def _():
        _attn_body(q_ref, k_ref, vt_ref, o_ref, True)

    @pl.when(jnp.logical_not(bounded))
    def _():
        _attn_body(q_ref, k_ref, vt_ref, o_ref, False)


def _attention(q, k, vt, b, s):
    tq = min(ATTN_Q_TILE, s)
    nq = s // tq
    pairs = MLA_HEADS // 2
    tk = vt.shape[2]
    nc = s // tk
    return pl.pallas_call(
        _attn_kernel,
        grid=(b, pairs, nq),
        in_specs=[pl.BlockSpec((tq, 2 * HEAD_PAD), lambda bi, pi, qi: (bi * nq + qi, pi)),
                  pl.BlockSpec((s, 2 * HEAD_PAD), lambda bi, pi, qi: (bi, pi)),
                  pl.BlockSpec((nc, 2 * MLA_V_DIM, tk), lambda bi, pi, qi: (bi, pi, 0))],
        out_specs=pl.BlockSpec((tq, 2 * MLA_V_DIM), lambda bi, pi, qi: (bi * nq + qi, pi)),
        out_shape=jax.ShapeDtypeStruct((b * s, MLA_OUT), BF16),
        scratch_shapes=[pltpu.SMEM((2,), F32)],
        compiler_params=pltpu.CompilerParams(
            dimension_semantics=("parallel", "parallel", "arbitrary"),
            vmem_limit_bytes=VMEM_LIMIT),
        name="attention",
    )(q, k, vt)


def _ssd_chunk(act, dt_raw, direction, h_ref, dtbias, a_all, e_mat, skip):
    n = SSD_CHUNK
    gs = SSD_GROUPS * SSD_STATE
    xs16 = act[:, :SSD_INNER]
    bm16 = act[:, SSD_INNER:SSD_INNER + gs]
    cm16 = act[:, SSD_INNER + gs:SSD_INNER + 2 * gs]
    xs = xs16.astype(F32)
    bm = bm16.astype(F32)

    dt_all = _softplus(dt_raw + dtbias)
    a_mat = dt_all * a_all
    ri = lax.broadcasted_iota(I32, (n, n), 0)
    ci = lax.broadcasted_iota(I32, (n, n), 1)
    mask = (ci <= ri) if direction == 0 else (ci >= ri)
    tri = jnp.where(mask, 1.0, 0.0).astype(BF16)
    a1, a2, a3 = _split3(a_mat)
    cs = _dot(tri, a1) + _dot(tri, a2) + _dot(tri, a3)
    end = n - 1 if direction == 0 else 0
    cs_end = cs[end:end + 1, :]
    w_state = dt_all * jnp.exp(cs_end - cs)
    e_off = jnp.exp(cs)
    c_dec = jnp.broadcast_to(jnp.exp(cs_end), (8, LANES))
    stack = jnp.concatenate([w_state, e_off, c_dec], axis=0).astype(BF16)
    expd = _dot(stack, e_mat)
    ws_x = expd[0:n]
    eo_x = expd[n:2 * n]
    cd_x = expd[2 * n:2 * n + 1]

    cs_t = cs.T
    dt_t = dt_all.T
    bm_t = bm.T.astype(BF16)

    pieces = []
    for g in range(SSD_GROUPS):
        gsl = slice(SSD_STATE * g, SSD_STATE * (g + 1))
        cg = cm16[:, gsl]
        gmat = _dot_nt(cg, bm16[:, gsl])
        for r in range(SSD_HEADS_PER_GROUP):
            hh = g * SSD_HEADS_PER_GROUP + r
            c = direction * SSD_HEADS + hh
            seg = cs[:, c:c + 1] - cs_t[c:c + 1, :]
            lm = jnp.where(mask, jnp.exp(jnp.where(mask, seg, 0.0)), 0.0) * dt_t[c:c + 1, :]
            mh = (gmat * lm).astype(BF16)
            pieces.append(_dot(mh, xs16[:, SSD_HEAD_DIM * hh:SSD_HEAD_DIM * (hh + 1)]))
    y = jnp.concatenate(pieces, axis=1)

    w = SSD_HEADS_PER_GROUP * SSD_HEAD_DIM
    offs = []
    for g in range(SSD_GROUPS):
        lsl = slice(w * g, w * (g + 1))
        gsl = slice(SSD_STATE * g, SSD_STATE * (g + 1))
        h_g = h_ref[:, lsl]
        offs.append(_dot(cm16[:, gsl], h_g.astype(BF16)) * eo_x[:, lsl])
        xd = (xs[:, lsl] * ws_x[:, lsl]).astype(BF16)
        h_ref[:, lsl] = h_g * cd_x[:, lsl] + _dot(bm_t[gsl, :], xd)
    y = y + jnp.concatenate(offs, axis=1)
    if skip is not None:
        y = y + xs * skip
    return y


def _ssd_kernel(xf_ref, dtf_ref, xb_ref, dtb_ref, dtbias_ref, alog_ref, skip_ref, e_ref,
                yf_ref, yb_ref, hf_ref, hb_ref):
    i = pl.program_id(1)

    @pl.when(i == 0)
    def _():
        hf_ref[...] = jnp.zeros_like(hf_ref)
        hb_ref[...] = jnp.zeros_like(hb_ref)

    lane = lax.broadcasted_iota(I32, (1, LANES), 1)
    a_all = jnp.where(lane < SSD_DIRECTIONS * SSD_HEADS, -jnp.exp(alog_ref[...]), 0.0)
    dtbias = dtbias_ref[...]
    skip = skip_ref[...]
    nch = xf_ref.shape[0] // SSD_CHUNK

    for c in range(nch):
        rows = slice(SSD_CHUNK * c, SSD_CHUNK * (c + 1))
        y = _ssd_chunk(xf_ref[rows, :], dtf_ref[rows, :], 0, hf_ref, dtbias, a_all, e_ref[0], skip)
        yf_ref[rows, :] = y.astype(yf_ref.dtype)

    for c in reversed(range(nch)):
        rows = slice(SSD_CHUNK * c, SSD_CHUNK * (c + 1))
        y = _ssd_chunk(xb_ref[rows, :], dtb_ref[rows, :], 1, hb_ref, dtbias, a_all, e_ref[1], None)
        yb_ref[rows, :] = y.astype(yb_ref.dtype)


def _ssd(act, dt_raw, dtbias, alog, skip, e_mat, b, s):
    r = min(SSD_BLOCK, s)
    nb = s // r

    def cur(rev):
        return (lambda bi, i: (bi * nb + (nb - 1 - i), 0)) if rev else (lambda bi, i: (bi * nb + i, 0))

    def role(rev):
        return [pl.BlockSpec((r, SSD_XBC), cur(rev)), pl.BlockSpec((r, LANES), cur(rev))]

    consts = (dtbias, alog, skip, e_mat)
    return pl.pallas_call(
        _ssd_kernel,
        grid=(b, nb),
        in_specs=role(False) + role(True) + [_full(c.shape) for c in consts],
        out_specs=[pl.BlockSpec((r, SSD_INNER), cur(False)), pl.BlockSpec((r, SSD_INNER), cur(True))],
        out_shape=[jax.ShapeDtypeStruct((b * s, SSD_INNER), F32)] * 2,
        scratch_shapes=[pltpu.VMEM((SSD_STATE, SSD_INNER), F32),
                        pltpu.VMEM((SSD_STATE, SSD_INNER), F32)],
        compiler_params=pltpu.CompilerParams(dimension_semantics=("parallel", "arbitrary"),
                                             vmem_limit_bytes=VMEM_LIMIT),
        name="ssd",
    )(act, dt_raw, act, dt_raw, *consts)


def _mixout_kernel(x_ref, attn_ref, yf_ref, yb_ref, z_ref, anw_ref, snw_ref, wo_ref, fnw_ref,
                   wr_ref, br_ref, x1_out, route_out):
    attn = _rms(attn_ref[...].astype(F32), anw_ref[...])
    z = z_ref[...].astype(F32)
    y = (yf_ref[...] + yb_ref[...]) * (z * _sigmoid(z))
    y = _rms(y, snw_ref[...])
    mix = jnp.concatenate([attn, y], axis=1).astype(BF16)
    x1 = x_ref[...] + _dot(mix, wo_ref[...])
    _store_token_tiles(x1_out, x1)

    h = _rms(x1, fnw_ref[...])
    h1, h2, _ = _split3(h)
    two = _dot_nt(wr_ref[...], h1) + _dot_nt(wr_ref[...], h2)
    lt = two[:ROUTE_ROWS] + two[ROUTE_ROWS:] + br_ref[...]

    epg = EXPERTS_PER_GROUP
    sub = lax.broadcasted_iota(I32, (epg, lt.shape[1]), 0)
    ninf = -jnp.inf

    def argmax_first(vals):
        vmax = jnp.max(vals, axis=0, keepdims=True)
        return jnp.min(jnp.where(vals == vmax, sub, epg), axis=0, keepdims=True)

    gidx = argmax_first(jnp.where(sub < N_EXPERT_GROUPS, lt[:epg], ninf))
    sel = lt[epg:2 * epg]
    for g in range(1, N_EXPERT_GROUPS):
        sel = jnp.where(gidx == g, lt[epg * (g + 1):epg * (g + 2)], sel)
    i1 = argmax_first(sel)
    i2 = argmax_first(jnp.where(sub == i1, ninf, sel))
    elo = jnp.minimum(i1, i2)
    ehi = jnp.maximum(i1, i2)
    pair = (elo * (2 * epg - 1 - elo)) // 2 + (ehi - elo - 1)
    cls = gidx * PAIRS_PER_GROUP + pair
    route_out[...] = jnp.broadcast_to(cls.astype(F32), route_out.shape)


def _mixout(x2, attn, yf, yb, z, anw, snw, wo, fnw, wr, br):
    t = x2.shape[0]
    tm = min(TOKEN_TILE, t)
    row = lambda w: pl.BlockSpec((tm, w), lambda i: (i, 0))
    br = jnp.broadcast_to(br, (ROUTE_ROWS, tm))
    weights = (anw, snw, wo, fnw, wr, br)
    return pl.pallas_call(
        _mixout_kernel,
        grid=(t // tm,),
        in_specs=[row(D_MODEL), row(MLA_OUT), row(SSD_INNER), row(SSD_INNER), row(SSD_INNER)]
        + [_full(w.shape) for w in weights],
        out_specs=[_token_tile_spec(tm), pl.BlockSpec((8, tm), lambda i: (i, 0))],
        out_shape=[jax.ShapeDtypeStruct((t * ROW_TILES, LANES), F32),
                   jax.ShapeDtypeStruct((t // tm * 8, tm), F32)],
        compiler_params=pltpu.CompilerParams(dimension_semantics=("parallel",),
                                             vmem_limit_bytes=VMEM_LIMIT),
        name="mixout",
    )(x2, attn, yf, yb, z, *weights)


def _moe_kernel(tile_lo_ref, tile_hi_ref, tile_j_ref, tile_n_ref, order_ref,
                x1_hbm, fnw_ref, wr_ref, br_ref, wgu_lo_ref, wdn_lo_ref, wgu_hi_ref, wdn_hi_ref,
                y_hbm, xbuf, obuf, sem_in, sem_out):
    t = pl.program_id(0)
    tile = xbuf.shape[1] // ROW_TILES
    n_tok = y_hbm.shape[0] - 2 * tile
    slot = t % 2

    def token_rows(r):
        return pl.ds(pl.multiple_of(r * ROW_TILES, ROW_TILES), ROW_TILES)

    def gather_start(tt, sl, inline):
        j0 = tile_j_ref[tt]

        def one(r, priority=0):
            tok = order_ref[j0 + r]
            pltpu.make_async_copy(x1_hbm.at[tok], xbuf.at[sl, token_rows(r)],
                                  sem_in.at[sl]).start(priority=priority)

        if inline:
            for r in range(tile):
                one(r, 1)
        else:
            def body(r, c):
                one(r)
                return c
            lax.fori_loop(0, tile, body, 0, unroll=8)

    def gather_wait(sl):
        pltpu.make_async_copy(xbuf.at[sl], xbuf.at[sl], sem_in.at[sl]).wait()

    def scatter_start(j0, n, sl, inline):
        def one(r, priority=0):
            tok = jnp.where(r < n, order_ref[j0 + r], n_tok + sl * tile + r)
            pltpu.make_async_copy(obuf.at[sl, token_rows(r)], y_hbm.at[tok],
                                  sem_out.at[sl]).start(priority=priority)

        if inline:
            for r in range(tile):
                one(r, r % 2)
        else:
            def body(r, c):
                one(r)
                return c
            lax.fori_loop(0, tile, body, 0, unroll=8)

    def scatter_wait(sl):
        pltpu.make_async_copy(obuf.at[sl], obuf.at[sl], sem_out.at[sl]).wait()

    prev = jnp.maximum(t - 1, 0)
    valid = tile_n_ref[t] > 0
    prev_valid = (t > 0) & (tile_n_ref[prev] > 0)

    @pl.when(t == 0)
    def _():
        obuf[...] = jnp.zeros_like(obuf)
        scatter_start(0, 0, 0, inline=False)
        scatter_wait(0)

        @pl.when(valid)
        def _():
            gather_start(t, slot, inline=False)

    @pl.when(jnp.logical_not(valid) & prev_valid)
    def _():
        gather_wait(slot)
        scatter_wait(slot)
        scatter_start(tile_j_ref[prev], tile_n_ref[prev], 1 - slot, inline=False)
        scatter_wait(1 - slot)

    @pl.when(valid)
    def _():
        gather_wait(slot)
        gather_start(t + 1, 1 - slot, inline=True)
        scatter_start(tile_j_ref[prev], jnp.where(t > 0, tile_n_ref[prev], 0), 1 - slot, inline=True)

        h = _rms(_load_token_tiles(xbuf.at[slot]), fnw_ref[...]).astype(BF16)
        logits = _dot(h, wr_ref[...]) + br_ref[...]
        lane = lax.broadcasted_iota(I32, logits.shape, 1)
        gl = jnp.where(lane < N_EXPERT_GROUPS, logits, -jnp.inf)
        gweight = 1.0 / jnp.sum(jnp.exp(gl - jnp.max(gl, axis=-1, keepdims=True)), axis=-1, keepdims=True)
        pick = lambda e: jnp.sum(jnp.where(lane == N_EXPERT_GROUPS + e, logits, 0.0), axis=-1, keepdims=True)
        l_lo = pick(tile_lo_ref[t])
        l_hi = pick(tile_hi_ref[t])
        wts = (gweight / (1.0 + jnp.exp(l_hi - l_lo)), gweight / (1.0 + jnp.exp(l_lo - l_hi)))
        acc = None
        for half, wgu_ref, wdn_ref in ((0, wgu_lo_ref, wdn_lo_ref), (1, wgu_hi_ref, wdn_hi_ref)):
            gu = _dot(h, wgu_ref[...])
            g = gu[:, :D_EXPERT]
            he = (g * _sigmoid(g) * gu[:, D_EXPERT:] * wts[half]).astype(BF16)
            d = _dot(he, wdn_ref[...])
            acc = d if acc is None else acc + d

        @pl.when(t > 0)
        def _():
            scatter_wait(slot)
        _store_token_tiles(obuf.at[slot], acc)


def _moe(x1t, order, tile_lo, tile_hi, tile_j, tile_n, fnw, wr, br, wgu, wdn):
    t = x1t.shape[0]
    n_tiles = tile_lo.shape[0]
    tile = MOE_TILE
    wspec = lambda shape, which: pl.BlockSpec(
        (None,) + shape, lambda i, lo, hi, tj, tn, od: ((lo, hi)[which][i], 0, 0))
    grid_spec = pltpu.PrefetchScalarGridSpec(
        num_scalar_prefetch=5,
        grid=(n_tiles,),
        in_specs=[pl.BlockSpec(memory_space=pl.ANY),
                  pl.BlockSpec((1, D_MODEL), lambda i, *_: (0, 0)),
                  pl.BlockSpec((D_MODEL, LANES), lambda i, *_: (0, 0)),
                  pl.BlockSpec((1, LANES), lambda i, *_: (0, 0)),
                  wspec((D_MODEL, 2 * D_EXPERT), 0), wspec((D_EXPERT, D_MODEL), 0),
                  wspec((D_MODEL, 2 * D_EXPERT), 1), wspec((D_EXPERT, D_MODEL), 1)],
        out_specs=pl.BlockSpec(memory_space=pl.ANY),
        scratch_shapes=[pltpu.VMEM((2, tile * ROW_TILES, LANES), F32),
                        pltpu.VMEM((2, tile * ROW_TILES, LANES), F32),
                        pltpu.SemaphoreType.DMA((2,)), pltpu.SemaphoreType.DMA((2,))],
    )
    return pl.pallas_call(
        _moe_kernel,
        grid_spec=grid_spec,
        out_shape=jax.ShapeDtypeStruct((t + 2 * tile, ROW_TILES, LANES), F32),
        compiler_params=pltpu.CompilerParams(dimension_semantics=("arbitrary",),
                                             vmem_limit_bytes=VMEM_LIMIT),
        name="moe",
    )(tile_lo, tile_hi, tile_j, tile_n, order, x1t, fnw, wr, br, wgu, wdn, wgu, wdn)


def _moe_plan(route, t):
    tile = MOE_TILE
    n_tiles = t // tile + N_CLASSES + 1
    cls = route.reshape(-1, 8, route.shape[1])[:, 0, :].reshape(t).astype(I32)
    _, order = lax.sort((cls, jnp.arange(t, dtype=I32)), num_keys=1, is_stable=True)
    order = jnp.concatenate([order, jnp.zeros((tile,), I32)])
    class_ids = jnp.arange(N_CLASSES, dtype=I32)
    counts = jnp.sum((cls[:, None] == class_ids[None, :]).astype(I32), axis=0)
    tiles_per = (counts + tile - 1) // tile
    tile_end = jnp.cumsum(tiles_per)
    tile_begin = tile_end - tiles_per
    starts = jnp.cumsum(counts) - counts
    t_idx = jnp.arange(n_tiles, dtype=I32)
    tile_cls = jnp.sum((tile_end[None, :] <= t_idx[:, None]).astype(I32), axis=1)
    valid = t_idx < tile_end[-1]
    last_cls = jnp.max(jnp.where(counts > 0, class_ids, 0))
    tile_cls = jnp.where(valid, tile_cls, last_cls)
    onehot = (tile_cls[:, None] == class_ids[None, :]).astype(I32)
    pick = lambda v: jnp.sum(onehot * v[None, :], axis=1)
    k = t_idx - pick(tile_begin)
    tile_j = jnp.where(valid, pick(starts) + k * tile, 0)
    tile_n = jnp.where(valid, jnp.clip(pick(counts) - k * tile, 0, tile), 0)
    lo_of_pair, hi_of_pair = [], []
    for lo in range(EXPERTS_PER_GROUP):
        for hi in range(lo + 1, EXPERTS_PER_GROUP):
            lo_of_pair.append(lo)
            hi_of_pair.append(hi)
    grp = class_ids // PAIRS_PER_GROUP
    class_lo = grp * EXPERTS_PER_GROUP + jnp.asarray(lo_of_pair * N_EXPERT_GROUPS, I32)
    class_hi = grp * EXPERTS_PER_GROUP + jnp.asarray(hi_of_pair * N_EXPERT_GROUPS, I32)
    return order, pick(class_lo), pick(class_hi), tile_j.astype(I32), tile_n.astype(I32)


def _final_kernel(x1_ref, ym_ref, p_ref, pnw_ref, wg_ref, bg_ref, wp_ref, ppnw_ref, fnw_ref, o_ref):
    x2 = _load_token_tiles(x1_ref) + _load_token_tiles(ym_ref)
    gate = _sigmoid(_dot(_rms(x2, pnw_ref[...]).astype(BF16), wg_ref[...]) + bg_ref[...])
    ple = _rms(_dot(p_ref[...].astype(BF16), wp_ref[...]), ppnw_ref[...])
    o_ref[...] = _rms(x2 + gate * ple, fnw_ref[...])


def _final(x1, ym, p2, pnw, wg, bg, wp, ppnw, fnw):
    t = p2.shape[0]
    tm = min(TOKEN_TILE, t)
    row = lambda w: pl.BlockSpec((tm, w), lambda i: (i, 0))
    weights = (pnw, wg, bg, wp, ppnw, fnw)
    return pl.pallas_call(
        _final_kernel,
        grid=(t // tm,),
        in_specs=[_token_tile_spec(tm), _token_tile_spec(tm), row(PLE_DIM)]
        + [_full(w.shape) for w in weights],
        out_specs=row(D_MODEL),
        out_shape=jax.ShapeDtypeStruct((t, D_MODEL), F32),
        compiler_params=pltpu.CompilerParams(dimension_semantics=("parallel",),
                                             vmem_limit_bytes=VMEM_LIMIT),
        name="final",
    )(x1, ym, p2, *weights)


def _pad_cols(w, width, offset=0):
    out = jnp.zeros((w.shape[0], width), w.dtype)
    return out.at[:, offset:offset + w.shape[1]].set(w)


def _rope_swap(w):
    half = MLA_ROPE_DIM // 2
    return jnp.concatenate([-w[..., half:], w[..., :half]], axis=-1)


def _layer(x2, p2, cos_t, sin_t, b, s, attn_norm_w, w_in, q_norm_w, w_uq, kv_norm_w, w_ukv,
           attn_out_norm_w, conv_w, conv_b, dt_bias, a_log, ssd_d, ssd_norm_w, w_o, ffn_norm_w,
           w_router_group, b_router_group, w_router_expert, b_router_expert, w_exp_gate,
           w_exp_up, w_exp_down, ple_norm_w, w_ple_gate, b_ple_gate, w_ple_proj, ple_post_norm_w):
    t = x2.shape[0]
    r1 = lambda v: v.reshape(1, -1).astype(F32)
    o1 = MLA_Q_RANK
    o2 = o1 + MLA_KV_RANK
    o3 = o2 + MLA_ROPE_DIM
    o4 = o3 + SSD_INNER
    o5 = o4 + SSD_XBC
    w_kr = w_in[:, o2:o3]
    wa = jnp.concatenate([w_in[:, :o2], _pad_cols(w_kr, HEAD_PAD, MLA_NOPE_DIM),
                          _pad_cols(_rope_swap(w_kr), HEAD_PAD, MLA_NOPE_DIM),
                          _pad_cols(w_in[:, o5:], LANES)], axis=1).astype(BF16)
    wz = w_in[:, o3:o4].astype(BF16)
    wxbc = w_in[:, o4:o5].astype(BF16)
    uq = w_uq.reshape(MLA_Q_RANK, MLA_HEADS, MLA_NOPE_DIM + MLA_ROPE_DIM)
    zq = jnp.zeros((MLA_Q_RANK, MLA_HEADS, HEAD_PAD - MLA_NOPE_DIM - MLA_ROPE_DIM), F32)
    wuq = jnp.concatenate([uq, zq], axis=-1).reshape(MLA_Q_RANK, -1).astype(BF16)
    wuqs = jnp.concatenate([jnp.zeros_like(uq[..., :MLA_NOPE_DIM]), _rope_swap(uq[..., MLA_NOPE_DIM:]),
                            zq], axis=-1).reshape(MLA_Q_RANK, -1).astype(BF16)
    ukv = w_ukv.reshape(MLA_KV_RANK, MLA_HEADS, MLA_NOPE_DIM + MLA_V_DIM)
    zk = jnp.zeros((MLA_KV_RANK, MLA_HEADS, HEAD_PAD - MLA_NOPE_DIM), F32)
    wuk = jnp.concatenate([ukv[..., :MLA_NOPE_DIM], zk], axis=-1).reshape(MLA_KV_RANK, -1).astype(BF16)
    wuv = ukv[..., MLA_NOPE_DIM:].reshape(MLA_KV_RANK, -1).T.astype(BF16)

    q, k, v, z, act, dt_raw = _inproj(x2, cos_t, sin_t, s, r1(attn_norm_w), wa, wz, wxbc,
                                      conv_w.astype(F32), r1(conv_b), r1(q_norm_w), wuq, wuqs,
                                      r1(kv_norm_w), wuk, wuv)
    attn = _attention(q, k, v, b, s)

    head_of_lane = jnp.arange(SSD_INNER) // SSD_HEAD_DIM
    rows = jnp.arange(LANES)[:, None]
    e_mat = jnp.stack([(rows == d * SSD_HEADS + head_of_lane[None, :]) for d in range(SSD_DIRECTIONS)]
                      ).astype(BF16)
    skip = jnp.repeat(ssd_d.astype(F32), SSD_HEAD_DIM).reshape(1, -1)
    yf, yb = _ssd(act, dt_raw, _pad_cols(r1(dt_bias), LANES), _pad_cols(r1(a_log), LANES), skip,
                  e_mat, b, s)

    wr = _pad_cols(jnp.concatenate([w_router_group, w_router_expert], axis=1).astype(F32), LANES)
    wr_hi = wr.astype(BF16)
    br = _pad_cols(jnp.concatenate([r1(b_router_group), r1(b_router_expert)], axis=1), LANES)
    gpad = jnp.zeros((EXPERTS_PER_GROUP - N_EXPERT_GROUPS, D_MODEL), F32)
    wrt = jnp.concatenate([w_router_group.T.astype(F32), gpad, w_router_expert.T.astype(F32),
                           jnp.zeros((ROUTE_ROWS - EXPERTS_PER_GROUP - N_EXPERTS, D_MODEL), F32)])
    wrt_hi, wrt_lo, _ = _split3(wrt)
    brt = jnp.concatenate([b_router_group.astype(F32), gpad[:, 0], b_router_expert.astype(F32),
                           jnp.zeros((ROUTE_ROWS - EXPERTS_PER_GROUP - N_EXPERTS,), F32)])
    x1, route = _mixout(x2, attn, yf, yb, z, r1(attn_out_norm_w), r1(ssd_norm_w), w_o.astype(BF16),
                        r1(ffn_norm_w), jnp.concatenate([wrt_hi, wrt_lo]), brt.reshape(-1, 1))

    order, tile_lo, tile_hi, tile_j, tile_n = _moe_plan(route, t)
    wgu = jnp.concatenate([w_exp_gate, w_exp_up], axis=-1).astype(BF16)
    ym = _moe(x1.reshape(t, ROW_TILES, LANES), order, tile_lo, tile_hi, tile_j, tile_n,
              r1(ffn_norm_w), wr_hi, br, wgu, w_exp_down.astype(BF16))
    ym = ym.reshape(ym.shape[0] * ROW_TILES, LANES)
    return x1, ym, (r1(ple_norm_w), w_ple_gate.astype(BF16), r1(b_ple_gate), w_ple_proj.astype(BF16),
                    r1(ple_post_norm_w))


def kernel(x, p, positions, attn_norm_w, w_in, q_norm_w, w_uq, kv_norm_w, w_ukv, attn_out_norm_w, conv_w, conv_b, dt_bias, a_log, ssd_d, ssd_norm_w, w_o, ffn_norm_w, w_router_group, b_router_group, w_router_expert, b_router_expert, w_exp_gate, w_exp_up, w_exp_down, ple_norm_w, w_ple_gate, b_ple_gate, w_ple_proj, ple_post_norm_w, final_norm_w):
    b, s, d = x.shape
    depth = p.shape[0]
    assert depth == 1, "the fused final stage assumes a single layer"
    t = b * s
    inv_freq = 1.0 / (ROPE_THETA ** (jnp.arange(0, MLA_ROPE_DIM, 2, dtype=F32) / MLA_ROPE_DIM))
    half = MLA_ROPE_DIM // 2
    ang = (positions.astype(F32).reshape(t, 1) * inv_freq).reshape(t * half // LANES, LANES)
    cos, sin = lax.optimization_barrier((jnp.cos(ang), jnp.sin(ang)))
    cos = cos.reshape(t, half)
    sin = sin.reshape(t, half)
    ones = jnp.ones((t, MLA_NOPE_DIM), F32)
    zeros = jnp.zeros((t, HEAD_PAD - MLA_NOPE_DIM - MLA_ROPE_DIM), F32)
    cos_t = jnp.concatenate([ones, cos, cos, zeros], axis=1)
    sin_t = jnp.concatenate([0.0 * ones, sin, sin, zeros], axis=1)

    x2 = x.reshape(t, d)
    i = 0
    x1, ym, (pnw, wg, bg, wp, ppnw) = _layer(
        x2, p[i].reshape(t, -1), cos_t, sin_t, b, s, attn_norm_w[i], w_in[i], q_norm_w[i], w_uq[i],
        kv_norm_w[i], w_ukv[i], attn_out_norm_w[i], conv_w[i], conv_b[i], dt_bias[i], a_log[i],
        ssd_d[i], ssd_norm_w[i], w_o[i], ffn_norm_w[i], w_router_group[i], b_router_group[i],
        w_router_expert[i], b_router_expert[i], w_exp_gate[i], w_exp_up[i], w_exp_down[i],
        ple_norm_w[i], w_ple_gate[i], b_ple_gate[i], w_ple_proj[i], ple_post_norm_w[i])
    out = _final(x1, ym, p[i].reshape(t, -1), pnw, wg, bg, wp, ppnw, final_norm_w.reshape(1, -1).astype(F32))
    return out.reshape(b, s, d)
```

```python
import functools
import math

import jax
import jax.numpy as jnp
from jax import lax
from jax.experimental import pallas as pl
from jax.experimental.pallas import tpu as pltpu

F32 = jnp.float32
BF16 = jnp.bfloat16
I32 = jnp.int32

D_MODEL = 1024
PLE_DIM = 256
NORM_EPS = 1e-6

MLA_HEADS = 8
MLA_Q_RANK = 256
MLA_KV_RANK = 128
MLA_NOPE_DIM = 64
MLA_ROPE_DIM = 32
MLA_V_DIM = 64
MLA_OUT = MLA_HEADS * MLA_V_DIM
ROPE_THETA = 10000.0
HEAD_PAD = 128
ONES_ROWS = 16

SSD_HEADS = 8
SSD_HEAD_DIM = 64
SSD_GROUPS = 2
SSD_HEADS_PER_GROUP = SSD_HEADS // SSD_GROUPS
SSD_STATE = 64
SSD_CONV = 5
SSD_CHUNK = 128
SSD_INNER = SSD_HEADS * SSD_HEAD_DIM
SSD_XBC = SSD_INNER + 2 * SSD_GROUPS * SSD_STATE
SSD_DIRECTIONS = 2
CONV_HALO = 8

N_EXPERT_GROUPS = 4
EXPERTS_PER_GROUP = 8
N_EXPERTS = N_EXPERT_GROUPS * EXPERTS_PER_GROUP
D_EXPERT = 256
PAIRS_PER_GROUP = EXPERTS_PER_GROUP * (EXPERTS_PER_GROUP - 1) // 2
N_CLASSES = N_EXPERT_GROUPS * PAIRS_PER_GROUP
ROUTE_ROWS = 48

LANES = 128
ROW_TILES = D_MODEL // LANES
VMEM_LIMIT = 48 * 1024 * 1024

TOKEN_TILE = 512
ATTN_Q_TILE = 512
ATTN_Q_SUB = 256
ATTN_PIPELINE = {True: (512, 3), False: (256, 4)}
ATTN_LOGIT_BOUND = 80.0
SSD_BLOCK = 512
MOE_TILE = 128


def _rms(x, w):
    ms = jnp.mean(x * x, axis=-1, keepdims=True)
    return x * lax.rsqrt(ms + NORM_EPS) * w


def _dot(a, b):
    return jnp.dot(a, b, preferred_element_type=F32)


def _dot_nt(a, b):
    return lax.dot_general(a, b, (((1,), (1,)), ((), ())), preferred_element_type=F32)


def _split3(x):
    x1 = x.astype(BF16)
    r1 = x - x1.astype(F32)
    x2 = r1.astype(BF16)
    x3 = (r1 - x2.astype(F32)).astype(BF16)
    return x1, x2, x3


def _sigmoid(x):
    return 1.0 / (1.0 + jnp.exp(-x))


def _softplus(x):
    return jnp.maximum(x, 0.0) + jnp.log(1.0 + jnp.exp(-jnp.abs(x)))


def _full(shape):
    nd = len(shape)
    return pl.BlockSpec(shape, lambda *_: (0,) * nd)


def _token_tile_spec(tm):
    return pl.BlockSpec((tm * ROW_TILES, LANES), lambda i: (i, 0))


def _store_token_tiles(ref, x):
    n = x.shape[0]
    for s in range(ROW_TILES):
        ref[pl.ds(s, n, stride=ROW_TILES), :] = x[:, LANES * s:LANES * (s + 1)]


def _load_token_tiles(ref):
    n = ref.shape[0] // ROW_TILES
    return jnp.concatenate([ref[pl.ds(s, n, stride=ROW_TILES), :] for s in range(ROW_TILES)], axis=1)


def _inproj_kernel(x_ref, xprev_ref, xnext_ref, nw_ref, wa_ref, wz_ref, wxbc_ref, cw_ref, cb_ref,
                   qnw_ref, wuq_ref, wuqs_ref, kvnw_ref, wuk_ref, wuv_ref, cos_ref, sin_ref,
                   q_out, k_out, v_out, z_out, act_out, dt_out, conv_ref, *, scale, tiles_per_seq):
    tm = x_ref.shape[0]
    h = _rms(x_ref[...], nw_ref[...]).astype(BF16)
    pa = _dot(h, wa_ref[...])
    z_out[...] = _dot(h, wz_ref[...]).astype(z_out.dtype)
    dt_out[...] = pa[:, MLA_Q_RANK + MLA_KV_RANK + 2 * HEAD_PAD:]

    pos = pl.program_id(0) % tiles_per_seq
    halo = lambda ref: _dot(_rms(ref[...], nw_ref[...]).astype(BF16), wxbc_ref[...])
    conv_ref[0:CONV_HALO, :] = jnp.where(pos > 0, halo(xprev_ref), 0.0)
    conv_ref[CONV_HALO:CONV_HALO + tm, :] = _dot(h, wxbc_ref[...])
    conv_ref[CONV_HALO + tm:2 * CONV_HALO + tm, :] = jnp.where(pos < tiles_per_seq - 1,
                                                               halo(xnext_ref), 0.0)
    acc = jnp.zeros((tm, SSD_XBC), F32) + cb_ref[...]
    base = CONV_HALO - SSD_CONV // 2
    for kk in range(SSD_CONV):
        acc = acc + cw_ref[kk:kk + 1, :] * conv_ref[base + kk:base + kk + tm, :]
    act_out[...] = (acc * _sigmoid(acc)).astype(act_out.dtype)

    cos = cos_ref[...]
    sin = sin_ref[...]
    o1 = MLA_Q_RANK
    o2 = o1 + MLA_KV_RANK
    cqn = _rms(pa[:, :o1], qnw_ref[...]).astype(BF16)
    ckvn = _rms(pa[:, o1:o2], kvnw_ref[...]).astype(BF16)
    q = _dot(cqn, wuq_ref[...])
    qs = _dot(cqn, wuqs_ref[...])
    kn = _dot(ckvn, wuk_ref[...])
    v_out[...] = _dot_nt(wuv_ref[...], ckvn).astype(v_out.dtype)
    kr = pa[:, o2:o2 + HEAD_PAD] * cos + pa[:, o2 + HEAD_PAD:o2 + 2 * HEAD_PAD] * sin
    for hh in range(MLA_HEADS):
        sl = slice(HEAD_PAD * hh, HEAD_PAD * (hh + 1))
        q_out[:, sl] = ((q[:, sl] * cos + qs[:, sl] * sin) * scale).astype(q_out.dtype)
        k_out[:, sl] = (kn[:, sl] + kr).astype(k_out.dtype)


def _inproj(x2, cos_t, sin_t, s, nw, wa, wz, wxbc, cw, cb, qnw, wuq, wuqs, kvnw, wuk, wuv):
    t = x2.shape[0]
    tm = min(TOKEN_TILE, s)
    hb = tm // CONV_HALO
    last = t // CONV_HALO - 1
    scale = (MLA_NOPE_DIM + MLA_ROPE_DIM) ** -0.5 * math.log2(math.e)
    row = lambda w: pl.BlockSpec((tm, w), lambda i: (i, 0))
    weights = (nw, wa, wz, wxbc, cw, cb, qnw, wuq, wuqs, kvnw, wuk, wuv)
    return pl.pallas_call(
        functools.partial(_inproj_kernel, scale=scale, tiles_per_seq=s // tm),
        grid=(t // tm,),
        in_specs=[row(D_MODEL),
                  pl.BlockSpec((CONV_HALO, D_MODEL), lambda i: (jnp.maximum(i * hb - 1, 0), 0)),
                  pl.BlockSpec((CONV_HALO, D_MODEL), lambda i: (jnp.minimum((i + 1) * hb, last), 0))]
        + [_full(w.shape) for w in weights] + [row(HEAD_PAD), row(HEAD_PAD)],
        out_specs=[row(MLA_HEADS * HEAD_PAD), row(MLA_HEADS * HEAD_PAD),
                   pl.BlockSpec((None, MLA_OUT, tm), lambda i: (i, 0, 0)),
                   row(SSD_INNER), row(SSD_XBC), row(LANES)],
        out_shape=[jax.ShapeDtypeStruct((t, MLA_HEADS * HEAD_PAD), BF16),
                   jax.ShapeDtypeStruct((t, MLA_HEADS * HEAD_PAD), BF16),
                   jax.ShapeDtypeStruct((t // tm, MLA_OUT, tm), BF16),
                   jax.ShapeDtypeStruct((t, SSD_INNER), BF16),
                   jax.ShapeDtypeStruct((t, SSD_XBC), BF16),
                   jax.ShapeDtypeStruct((t, LANES), F32)],
        scratch_shapes=[pltpu.VMEM((tm + 2 * CONV_HALO, SSD_XBC), F32)],
        compiler_params=pltpu.CompilerParams(dimension_semantics=("parallel",),
                                             vmem_limit_bytes=VMEM_LIMIT),
        name="inproj",
    )(x2, x2, x2, *weights, cos_t, sin_t)


def _attn_body(q_ref, k_ref, vt_ref, o_ref, bounded):
    tq = q_ref.shape[0]
    nc, _, tv = vt_ref.shape
    tk, ahead = ATTN_PIPELINE[bounded]
    ts = min(ATTN_Q_SUB, tq)
    ones = jnp.ones((ONES_ROWS, tk), BF16)
    streams = []
    for j in range(tq // ts):
        for a in range(2):
            hsl = slice(HEAD_PAD * a, HEAD_PAD * (a + 1))
            vsl = slice(MLA_V_DIM * a, MLA_V_DIM * (a + 1))
            streams.append([hsl, vsl, q_ref[j * ts:(j + 1) * ts, hsl],
                            jnp.full((1, ts), -1e30, F32),
                            jnp.zeros((MLA_V_DIM + ONES_ROWS, ts), F32)])
    units = [(c, i) for c in range(nc * tv // tk) for i in range(len(streams))]
    scores = {}

    def issue(u):
        c, i = units[u]
        scores[u] = _dot_nt(k_ref[c * tk:(c + 1) * tk, streams[i][0]], streams[i][2])

    for u in range(min(ahead, len(units))):
        issue(u)
    for u, (c, i) in enumerate(units):
        if u + ahead < len(units):
            issue(u + ahead)
        st = streams[i]
        _, vsl, _, m, acc = st
        s = scores.pop(u)
        blk, off = divmod(c * tk, tv)
        vt = jnp.concatenate([vt_ref[blk, vsl, off:off + tk], ones], axis=0)
        if bounded:
            st[4] = acc + _dot(vt, jnp.exp2(s).astype(BF16))
        else:
            m_new = jnp.maximum(m, jnp.max(s, axis=0, keepdims=True))
            p = jnp.exp2((s - m_new).astype(BF16))
            st[3] = m_new
            st[4] = acc * jnp.exp2(m - m_new) + _dot(vt, p)
    for j in range(tq // ts):
        halves = [st[4][:MLA_V_DIM] / st[4][MLA_V_DIM:MLA_V_DIM + 1] for st in streams[2 * j:2 * j + 2]]
        o_ref[j * ts:(j + 1) * ts, :] = jnp.concatenate(halves, axis=0).T.astype(o_ref.dtype)


def _attn_kernel(q_ref, k_ref, vt_ref, o_ref, kmax_ref):
    @pl.when(pl.program_id(2) == 0)
    def _():
        for a in range(2):
            kmax_ref[a] = jnp.max(jnp.abs(k_ref[:, HEAD_PAD * a:HEAD_PAD * (a + 1)].astype(F32)))

    bounded = None
    for a in range(2):
        q1 = jnp.sum(jnp.abs(q_ref[:, HEAD_PAD * a:HEAD_PAD * (a + 1)].astype(F32)), axis=1, keepdims=True)
        inside = jnp.max(q1) * kmax_ref[a] <= ATTN_LOGIT_BOUND
        bounded = inside if bounded is None else bounded & inside

    @pl.when(bounded)
    def _():
        _attn_body(q_ref, k_ref, vt_ref, o_ref, True)

    @pl.when(jnp.logical_not(bounded))
    def _():
        _attn_body(q_ref, k_ref, vt_ref, o_ref, False)


def _attention(q, k, vt, b, s):
    tq = min(ATTN_Q_TILE, s)
    nq = s // tq
    pairs = MLA_HEADS // 2
    tk = vt.shape[2]
    nc = s // tk
    return pl.pallas_call(
        _attn_kernel,
        grid=(b, pairs, nq),
        in_specs=[pl.BlockSpec((tq, 2 * HEAD_PAD), lambda bi, pi, qi: (bi * nq + qi, pi)),
                  pl.BlockSpec((s, 2 * HEAD_PAD), lambda bi, pi, qi: (bi, pi)),
                  pl.BlockSpec((nc, 2 * MLA_V_DIM, tk), lambda bi, pi, qi: (bi, pi, 0))],
        out_specs=pl.BlockSpec((tq, 2 * MLA_V_DIM), lambda bi, pi, qi: (bi * nq + qi, pi)),
        out_shape=jax.ShapeDtypeStruct((b * s, MLA_OUT), BF16),
        scratch_shapes=[pltpu.SMEM((2,), F32)],
        compiler_params=pltpu.CompilerParams(
            dimension_semantics=("parallel", "parallel", "arbitrary"),
            vmem_limit_bytes=VMEM_LIMIT),
        name="attention",
    )(q, k, vt)


def _ssd_chunk(act, dt_raw, direction, h_ref, dtbias, a_all, e_mat, skip):
    n = SSD_CHUNK
    gs = SSD_GROUPS * SSD_STATE
    xs16 = act[:, :SSD_INNER]
    bm16 = act[:, SSD_INNER:SSD_INNER + gs]
    cm16 = act[:, SSD_INNER + gs:SSD_INNER + 2 * gs]
    xs = xs16.astype(F32)
    bm = bm16.astype(F32)

    dt_all = _softplus(dt_raw + dtbias)
    a_mat = dt_all * a_all
    ri = lax.broadcasted_iota(I32, (n, n), 0)
    ci = lax.broadcasted_iota(I32, (n, n), 1)
    mask = (ci <= ri) if direction == 0 else (ci >= ri)
    tri = jnp.where(mask, 1.0, 0.0).astype(BF16)
    a1, a2, a3 = _split3(a_mat)
    cs = _dot(tri, a1) + _dot(tri, a2) + _dot(tri, a3)
    end = n - 1 if direction == 0 else 0
    cs_end = cs[end:end + 1, :]
    w_state = dt_all * jnp.exp(cs_end - cs)
    e_off = jnp.exp(cs)
    c_dec = jnp.broadcast_to(jnp.exp(cs_end), (8, LANES))
    stack = jnp.concatenate([w_state, e_off, c_dec], axis=0).astype(BF16)
    expd = _dot(stack, e_mat)
    ws_x = expd[0:n]
    eo_x = expd[n:2 * n]
    cd_x = expd[2 * n:2 * n + 1]

    cs_t = cs.T
    dt_t = dt_all.T
    bm_t = bm.T.astype(BF16)

    pieces = []
    for g in range(SSD_GROUPS):
        gsl = slice(SSD_STATE * g, SSD_STATE * (g + 1))
        cg = cm16[:, gsl]
        gmat = _dot_nt(cg, bm16[:, gsl])
        for r in range(SSD_HEADS_PER_GROUP):
            hh = g * SSD_HEADS_PER_GROUP + r
            c = direction * SSD_HEADS + hh
            seg = cs[:, c:c + 1] - cs_t[c:c + 1, :]
            lm = jnp.where(mask, jnp.exp(jnp.where(mask, seg, 0.0)), 0.0) * dt_t[c:c + 1, :]
            mh = (gmat * lm).astype(BF16)
            pieces.append(_dot(mh, xs16[:, SSD_HEAD_DIM * hh:SSD_HEAD_DIM * (hh + 1)]))
    y = jnp.concatenate(pieces, axis=1)

    w = SSD_HEADS_PER_GROUP * SSD_HEAD_DIM
    offs = []
    for g in range(SSD_GROUPS):
        lsl = slice(w * g, w * (g + 1))
        gsl = slice(SSD_STATE * g, SSD_STATE * (g + 1))
        h_g = h_ref[:, lsl]
        offs.append(_dot(cm16[:, gsl], h_g.astype(BF16)) * eo_x[:, lsl])
        xd = (xs[:, lsl] * ws_x[:, lsl]).astype(BF16)
        h_ref[:, lsl] = h_g * cd_x[:, lsl] + _dot(bm_t[gsl, :], xd)
    y = y + jnp.concatenate(offs, axis=1)
    if skip is not None:
        y = y + xs * skip
    return y


def _ssd_kernel(xf_ref, dtf_ref, xb_ref, dtb_ref, dtbias_ref, alog_ref, skip_ref, e_ref,
                yf_ref, yb_ref, hf_ref, hb_ref):
    i = pl.program_id(1)

    @pl.when(i == 0)
    def _():
        hf_ref[...] = jnp.zeros_like(hf_ref)
        hb_ref[...] = jnp.zeros_like(hb_ref)

    lane = lax.broadcasted_iota(I32, (1, LANES), 1)
    a_all = jnp.where(lane < SSD_DIRECTIONS * SSD_HEADS, -jnp.exp(alog_ref[...]), 0.0)
    dtbias = dtbias_ref[...]
    skip = skip_ref[...]
    nch = xf_ref.shape[0] // SSD_CHUNK

    for c in range(nch):
        rows = slice(SSD_CHUNK * c, SSD_CHUNK * (c + 1))
        y = _ssd_chunk(xf_ref[rows, :], dtf_ref[rows, :], 0, hf_ref, dtbias, a_all, e_ref[0], skip)
        yf_ref[rows, :] = y.astype(yf_ref.dtype)

    for c in reversed(range(nch)):
        rows = slice(SSD_CHUNK * c, SSD_CHUNK * (c + 1))
        y = _ssd_chunk(xb_ref[rows, :], dtb_ref[rows, :], 1, hb_ref, dtbias, a_all, e_ref[1], None)
        yb_ref[rows, :] = y.astype(yb_ref.dtype)


def _ssd(act, dt_raw, dtbias, alog, skip, e_mat, b, s):
    r = min(SSD_BLOCK, s)
    nb = s // r

    def cur(rev):
        return (lambda bi, i: (bi * nb + (nb - 1 - i), 0)) if rev else (lambda bi, i: (bi * nb + i, 0))

    def role(rev):
        return [pl.BlockSpec((r, SSD_XBC), cur(rev)), pl.BlockSpec((r, LANES), cur(rev))]

    consts = (dtbias, alog, skip, e_mat)
    return pl.pallas_call(
        _ssd_kernel,
        grid=(b, nb),
        in_specs=role(False) + role(True) + [_full(c.shape) for c in consts],
        out_specs=[pl.BlockSpec((r, SSD_INNER), cur(False)), pl.BlockSpec((r, SSD_INNER), cur(True))],
        out_shape=[jax.ShapeDtypeStruct((b * s, SSD_INNER), F32)] * 2,
        scratch_shapes=[pltpu.VMEM((SSD_STATE, SSD_INNER), F32),
                        pltpu.VMEM((SSD_STATE, SSD_INNER), F32)],
        compiler_params=pltpu.CompilerParams(dimension_semantics=("parallel", "arbitrary"),
                                             vmem_limit_bytes=VMEM_LIMIT),
        name="ssd",
    )(act, dt_raw, act, dt_raw, *consts)


def _mixout_kernel(x_ref, attn_ref, yf_ref, yb_ref, z_ref, anw_ref, snw_ref, wo_ref, fnw_ref,
                   wr_ref, br_ref, x1_out, route_out):
    attn = _rms(attn_ref[...].astype(F32), anw_ref[...])
    z = z_ref[...].astype(F32)
    y = (yf_ref[...] + yb_ref[...]) * (z * _sigmoid(z))
    y = _rms(y, snw_ref[...])
    mix = jnp.concatenate([attn, y], axis=1).astype(BF16)
    x1 = x_ref[...] + _dot(mix, wo_ref[...])
    _store_token_tiles(x1_out, x1)

    h = _rms(x1, fnw_ref[...])
    h1, h2, _ = _split3(h)
    two = _dot_nt(wr_ref[...], h1) + _dot_nt(wr_ref[...], h2)
    lt = two[:ROUTE_ROWS] + two[ROUTE_ROWS:] + br_ref[...]

    epg = EXPERTS_PER_GROUP
    sub = lax.broadcasted_iota(I32, (epg, lt.shape[1]), 0)
    ninf = -jnp.inf

    def argmax_first(vals):
        vmax = jnp.max(vals, axis=0, keepdims=True)
        return jnp.min(jnp.where(vals == vmax, sub, epg), axis=0, keepdims=True)

    gidx = argmax_first(jnp.where(sub < N_EXPERT_GROUPS, lt[:epg], ninf))
    sel = lt[epg:2 * epg]
    for g in range(1, N_EXPERT_GROUPS):
        sel = jnp.where(gidx == g, lt[epg * (g + 1):epg * (g + 2)], sel)
    i1 = argmax_first(sel)
    i2 = argmax_first(jnp.where(sub == i1, ninf, sel))
    elo = jnp.minimum(i1, i2)
    ehi = jnp.maximum(i1, i2)
    pair = (elo * (2 * epg - 1 - elo)) // 2 + (ehi - elo - 1)
    cls = gidx * PAIRS_PER_GROUP + pair
    route_out[...] = jnp.broadcast_to(cls.astype(F32), route_out.shape)


def _mixout(x2, attn, yf, yb, z, anw, snw, wo, fnw, wr, br):
    t = x2.shape[0]
    tm = min(TOKEN_TILE, t)
    row = lambda w: pl.BlockSpec((tm, w), lambda i: (i, 0))
    br = jnp.broadcast_to(br, (ROUTE_ROWS, tm))
    weights = (anw, snw, wo, fnw, wr, br)
    return pl.pallas_call(
        _mixout_kernel,
        grid=(t // tm,),
        in_specs=[row(D_MODEL), row(MLA_OUT), row(SSD_INNER), row(SSD_INNER), row(SSD_INNER)]
        + [_full(w.shape) for w in weights],
        out_specs=[_token_tile_spec(tm), pl.BlockSpec((8, tm), lambda i: (i, 0))],
        out_shape=[jax.ShapeDtypeStruct((t * ROW_TILES, LANES), F32),
                   jax.ShapeDtypeStruct((t // tm * 8, tm), F32)],
        compiler_params=pltpu.CompilerParams(dimension_semantics=("parallel",),
                                             vmem_limit_bytes=VMEM_LIMIT),
        name="mixout",
    )(x2, attn, yf, yb, z, *weights)


def _moe_kernel(tile_lo_ref, tile_hi_ref, tile_j_ref, tile_n_ref, order_ref,
                x1_hbm, fnw_ref, wr_ref, br_ref, *rest):
    weight_refs = rest[:8]
    y_hbm, xbuf, obuf, sem_in, sem_out = rest[8:]
    tile = xbuf.shape[1] // ROW_TILES
    n_tok = y_hbm.shape[0] - 2 * tile

    def token_rows(r):
        return pl.ds(pl.multiple_of(r * ROW_TILES, ROW_TILES), ROW_TILES)

    def gather_start(tt, sl, inline):
        j0 = tile_j_ref[tt]

        def one(r, priority=0):
            tok = order_ref[j0 + r]
            pltpu.make_async_copy(x1_hbm.at[tok], xbuf.at[sl, token_rows(r)],
                                  sem_in.at[sl]).start(priority=priority)

        if inline:
            for r in range(tile):
                one(r, 1)
        else:
            def body(r, c):
                one(r)
                return c
            lax.fori_loop(0, tile, body, 0, unroll=8)

    def gather_wait(sl):
        pltpu.make_async_copy(xbuf.at[sl], xbuf.at[sl], sem_in.at[sl]).wait()

    def scatter_start(j0, n, sl, inline):
        def one(r, priority=0):
            tok = jnp.where(r < n, order_ref[j0 + r], n_tok + sl * tile + r)
            pltpu.make_async_copy(obuf.at[sl, token_rows(r)], y_hbm.at[tok],
                                  sem_out.at[sl]).start(priority=priority)

        if inline:
            for r in range(tile):
                one(r, r % 2)
        else:
            def body(r, c):
                one(r)
                return c
            lax.fori_loop(0, tile, body, 0, unroll=8)

    def scatter_wait(sl):
        pltpu.make_async_copy(obuf.at[sl], obuf.at[sl], sem_out.at[sl]).wait()

    def one_tile(t, slot, wgu_lo_ref, wdn_lo_ref, wgu_hi_ref, wdn_hi_ref):
        prev = jnp.maximum(t - 1, 0)
        valid = tile_n_ref[t] > 0
        prev_valid = (t > 0) & (tile_n_ref[prev] > 0)

        @pl.when(t == 0)
        def _():
            obuf[...] = jnp.zeros_like(obuf)
            scatter_start(0, 0, 0, inline=False)
            scatter_wait(0)

            @pl.when(valid)
            def _():
                gather_start(t, slot, inline=False)

        @pl.when(jnp.logical_not(valid) & prev_valid)
        def _():
            gather_wait(slot)
            scatter_wait(slot)
            scatter_start(tile_j_ref[prev], tile_n_ref[prev], 1 - slot, inline=False)
            scatter_wait(1 - slot)

        @pl.when(valid)
        def _():
            gather_wait(slot)
            gather_start(t + 1, 1 - slot, inline=True)
            scatter_start(tile_j_ref[prev], jnp.where(t > 0, tile_n_ref[prev], 0), 1 - slot,
                          inline=True)

            h = _rms(_load_token_tiles(xbuf.at[slot]), fnw_ref[...]).astype(BF16)
            logits = _dot(h, wr_ref[...]) + br_ref[...]
            lane = lax.broadcasted_iota(I32, logits.shape, 1)
            gl = jnp.where(lane < N_EXPERT_GROUPS, logits, -jnp.inf)
            gweight = 1.0 / jnp.sum(jnp.exp(gl - jnp.max(gl, axis=-1, keepdims=True)),
                                    axis=-1, keepdims=True)
            pick = lambda e: jnp.sum(jnp.where(lane == N_EXPERT_GROUPS + e, logits, 0.0),
                                     axis=-1, keepdims=True)
            l_lo = pick(tile_lo_ref[t])
            l_hi = pick(tile_hi_ref[t])
            wts = (gweight / (1.0 + jnp.exp(l_hi - l_lo)), gweight / (1.0 + jnp.exp(l_lo - l_hi)))
            acc = None
            for half, wgu_ref, wdn_ref in ((0, wgu_lo_ref, wdn_lo_ref), (1, wgu_hi_ref, wdn_hi_ref)):
                gu = _dot(h, wgu_ref[...])
                g = gu[:, :D_EXPERT]
                he = (g * _sigmoid(g) * gu[:, D_EXPERT:] * wts[half]).astype(BF16)
                d = _dot(he, wdn_ref[...])
                acc = d if acc is None else acc + d

            @pl.when(t > 0)
            def _():
                scatter_wait(slot)
            _store_token_tiles(obuf.at[slot], acc)

    for slot in range(2):
        one_tile(2 * pl.program_id(0) + slot, slot, *weight_refs[4 * slot:4 * slot + 4])


def _moe(x1t, order, tile_lo, tile_hi, tile_j, tile_n, fnw, wr, br, wgu, wdn):
    t = x1t.shape[0]
    n_tiles = tile_lo.shape[0]
    tile = MOE_TILE
    wspec = lambda shape, which, off: pl.BlockSpec(
        (None,) + shape, lambda i, lo, hi, tj, tn, od: ((lo, hi)[which][2 * i + off], 0, 0))
    tile_weights = lambda off: [wspec((D_MODEL, 2 * D_EXPERT), 0, off), wspec((D_EXPERT, D_MODEL), 0, off),
                                wspec((D_MODEL, 2 * D_EXPERT), 1, off), wspec((D_EXPERT, D_MODEL), 1, off)]
    grid_spec = pltpu.PrefetchScalarGridSpec(
        num_scalar_prefetch=5,
        grid=(n_tiles // 2,),
        in_specs=[pl.BlockSpec(memory_space=pl.ANY),
                  pl.BlockSpec((1, D_MODEL), lambda i, *_: (0, 0)),
                  pl.BlockSpec((D_MODEL, LANES), lambda i, *_: (0, 0)),
                  pl.BlockSpec((1, LANES), lambda i, *_: (0, 0))] + tile_weights(0) + tile_weights(1),
        out_specs=pl.BlockSpec(memory_space=pl.ANY),
        scratch_shapes=[pltpu.VMEM((2, tile * ROW_TILES, LANES), F32),
                        pltpu.VMEM((2, tile * ROW_TILES, LANES), F32),
                        pltpu.SemaphoreType.DMA((2,)), pltpu.SemaphoreType.DMA((2,))],
    )
    return pl.pallas_call(
        _moe_kernel,
        grid_spec=grid_spec,
        out_shape=jax.ShapeDtypeStruct((t + 2 * tile, ROW_TILES, LANES), F32),
        compiler_params=pltpu.CompilerParams(dimension_semantics=("arbitrary",),
                                             vmem_limit_bytes=VMEM_LIMIT),
        name="moe",
    )(tile_lo, tile_hi, tile_j, tile_n, order, x1t, fnw, wr, br, *([wgu, wdn] * 4))


def _moe_plan(route, t):
    tile = MOE_TILE
    n_tiles = 2 * ((t // tile + N_CLASSES + 2) // 2)
    cls = route.reshape(-1, 8, route.shape[1])[:, 0, :].reshape(t).astype(I32)
    _, order = lax.sort((cls, jnp.arange(t, dtype=I32)), num_keys=1, is_stable=True)
    order = jnp.concatenate([order, jnp.zeros((tile,), I32)])
    class_ids = jnp.arange(N_CLASSES, dtype=I32)
    counts = jnp.sum((cls[:, None] == class_ids[None, :]).astype(I32), axis=0)
    tiles_per = (counts + tile - 1) // tile
    tile_end = jnp.cumsum(tiles_per)
    tile_begin = tile_end - tiles_per
    starts = jnp.cumsum(counts) - counts
    t_idx = jnp.arange(n_tiles, dtype=I32)
    tile_cls = jnp.sum((tile_end[None, :] <= t_idx[:, None]).astype(I32), axis=1)
    valid = t_idx < tile_end[-1]
    last_cls = jnp.max(jnp.where(counts > 0, class_ids, 0))
    tile_cls = jnp.where(valid, tile_cls, last_cls)
    onehot = (tile_cls[:, None] == class_ids[None, :]).astype(I32)
    pick = lambda v: jnp.sum(onehot * v[None, :], axis=1)
    k = t_idx - pick(tile_begin)
    tile_j = jnp.where(valid, pick(starts) + k * tile, 0)
    tile_n = jnp.where(valid, jnp.clip(pick(counts) - k * tile, 0, tile), 0)
    lo_of_pair, hi_of_pair = [], []
    for lo in range(EXPERTS_PER_GROUP):
        for hi in range(lo + 1, EXPERTS_PER_GROUP):
            lo_of_pair.append(lo)
            hi_of_pair.append(hi)
    grp = class_ids // PAIRS_PER_GROUP
    class_lo = grp * EXPERTS_PER_GROUP + jnp.asarray(lo_of_pair * N_EXPERT_GROUPS, I32)
    class_hi = grp * EXPERTS_PER_GROUP + jnp.asarray(hi_of_pair * N_EXPERT_GROUPS, I32)
    return order, pick(class_lo), pick(class_hi), tile_j.astype(I32), tile_n.astype(I32)


def _final_kernel(x1_ref, ym_ref, p_ref, pnw_ref, wg_ref, bg_ref, wp_ref, ppnw_ref, fnw_ref, o_ref):
    x2 = _load_token_tiles(x1_ref) + _load_token_tiles(ym_ref)
    gate = _sigmoid(_dot(_rms(x2, pnw_ref[...]).astype(BF16), wg_ref[...]) + bg_ref[...])
    ple = _rms(_dot(p_ref[...].astype(BF16), wp_ref[...]), ppnw_ref[...])
    o_ref[...] = _rms(x2 + gate * ple, fnw_ref[...])


def _final(x1, ym, p2, pnw, wg, bg, wp, ppnw, fnw):
    t = p2.shape[0]
    tm = min(TOKEN_TILE, t)
    row = lambda w: pl.BlockSpec((tm, w), lambda i: (i, 0))
    weights = (pnw, wg, bg, wp, ppnw, fnw)
    return pl.pallas_call(
        _final_kernel,
        grid=(t // tm,),
        in_specs=[_token_tile_spec(tm), _token_tile_spec(tm), row(PLE_DIM)]
        + [_full(w.shape) for w in weights],
        out_specs=row(D_MODEL),
        out_shape=jax.ShapeDtypeStruct((t, D_MODEL), F32),
        compiler_params=pltpu.CompilerParams(dimension_semantics=("parallel",),
                                             vmem_limit_bytes=VMEM_LIMIT),
        name="final",
    )(x1, ym, p2, *weights)


def _pad_cols(w, width, offset=0):
    out = jnp.zeros((w.shape[0], width), w.dtype)
    return out.at[:, offset:offset + w.shape[1]].set(w)


def _rope_swap(w):
    half = MLA_ROPE_DIM // 2
    return jnp.concatenate([-w[..., half:], w[..., :half]], axis=-1)


def _layer(x2, p2, cos_t, sin_t, b, s, attn_norm_w, w_in, q_norm_w, w_uq, kv_norm_w, w_ukv,
           attn_out_norm_w, conv_w, conv_b, dt_bias, a_log, ssd_d, ssd_norm_w, w_o, ffn_norm_w,
           w_router_group, b_router_group, w_router_expert, b_router_expert, w_exp_gate,
           w_exp_up, w_exp_down, ple_norm_w, w_ple_gate, b_ple_gate, w_ple_proj, ple_post_norm_w):
    t = x2.shape[0]
    r1 = lambda v: v.reshape(1, -1).astype(F32)
    o1 = MLA_Q_RANK
    o2 = o1 + MLA_KV_RANK
    o3 = o2 + MLA_ROPE_DIM
    o4 = o3 + SSD_INNER
    o5 = o4 + SSD_XBC
    w_kr = w_in[:, o2:o3]
    wa = jnp.concatenate([w_in[:, :o2], _pad_cols(w_kr, HEAD_PAD, MLA_NOPE_DIM),
                          _pad_cols(_rope_swap(w_kr), HEAD_PAD, MLA_NOPE_DIM),
                          _pad_cols(w_in[:, o5:], LANES)], axis=1).astype(BF16)
    wz = w_in[:, o3:o4].astype(BF16)
    wxbc = w_in[:, o4:o5].astype(BF16)
    uq = w_uq.reshape(MLA_Q_RANK, MLA_HEADS, MLA_NOPE_DIM + MLA_ROPE_DIM)
    zq = jnp.zeros((MLA_Q_RANK, MLA_HEADS, HEAD_PAD - MLA_NOPE_DIM - MLA_ROPE_DIM), F32)
    wuq = jnp.concatenate([uq, zq], axis=-1).reshape(MLA_Q_RANK, -1).astype(BF16)
    wuqs = jnp.concatenate([jnp.zeros_like(uq[..., :MLA_NOPE_DIM]), _rope_swap(uq[..., MLA_NOPE_DIM:]),
                            zq], axis=-1).reshape(MLA_Q_RANK, -1).astype(BF16)
    ukv = w_ukv.reshape(MLA_KV_RANK, MLA_HEADS, MLA_NOPE_DIM + MLA_V_DIM)
    zk = jnp.zeros((MLA_KV_RANK, MLA_HEADS, HEAD_PAD - MLA_NOPE_DIM), F32)
    wuk = jnp.concatenate([ukv[..., :MLA_NOPE_DIM], zk], axis=-1).reshape(MLA_KV_RANK, -1).astype(BF16)
    wuv = ukv[..., MLA_NOPE_DIM:].reshape(MLA_KV_RANK, -1).T.astype(BF16)

    q, k, v, z, act, dt_raw = _inproj(x2, cos_t, sin_t, s, r1(attn_norm_w), wa, wz, wxbc,
                                      conv_w.astype(F32), r1(conv_b), r1(q_norm_w), wuq, wuqs,
                                      r1(kv_norm_w), wuk, wuv)
    attn = _attention(q, k, v, b, s)

    head_of_lane = jnp.arange(SSD_INNER) // SSD_HEAD_DIM
    rows = jnp.arange(LANES)[:, None]
    e_mat = jnp.stack([(rows == d * SSD_HEADS + head_of_lane[None, :]) for d in range(SSD_DIRECTIONS)]
                      ).astype(BF16)
    skip = jnp.repeat(ssd_d.astype(F32), SSD_HEAD_DIM).reshape(1, -1)
    yf, yb = _ssd(act, dt_raw, _pad_cols(r1(dt_bias), LANES), _pad_cols(r1(a_log), LANES), skip,
                  e_mat, b, s)

    wr = _pad_cols(jnp.concatenate([w_router_group, w_router_expert], axis=1).astype(F32), LANES)
    wr_hi = wr.astype(BF16)
    br = _pad_cols(jnp.concatenate([r1(b_router_group), r1(b_router_expert)], axis=1), LANES)
    gpad = jnp.zeros((EXPERTS_PER_GROUP - N_EXPERT_GROUPS, D_MODEL), F32)
    wrt = jnp.concatenate([w_router_group.T.astype(F32), gpad, w_router_expert.T.astype(F32),
                           jnp.zeros((ROUTE_ROWS - EXPERTS_PER_GROUP - N_EXPERTS, D_MODEL), F32)])
    wrt_hi, wrt_lo, _ = _split3(wrt)
    brt = jnp.concatenate([b_router_group.astype(F32), gpad[:, 0], b_router_expert.astype(F32),
                           jnp.zeros((ROUTE_ROWS - EXPERTS_PER_GROUP - N_EXPERTS,), F32)])
    x1, route = _mixout(x2, attn, yf, yb, z, r1(attn_out_norm_w), r1(ssd_norm_w), w_o.astype(BF16),
                        r1(ffn_norm_w), jnp.concatenate([wrt_hi, wrt_lo]), brt.reshape(-1, 1))

    order, tile_lo, tile_hi, tile_j, tile_n = _moe_plan(route, t)
    wgu = jnp.concatenate([w_exp_gate, w_exp_up], axis=-1).astype(BF16)
    ym = _moe(x1.reshape(t, ROW_TILES, LANES), order, tile_lo, tile_hi, tile_j, tile_n,
              r1(ffn_norm_w), wr_hi, br, wgu, w_exp_down.astype(BF16))
    ym = ym.reshape(ym.shape[0] * ROW_TILES, LANES)
    return x1, ym, (r1(ple_norm_w), w_ple_gate.astype(BF16), r1(b_ple_gate), w_ple_proj.astype(BF16),
                    r1(ple_post_norm_w))


def kernel(x, p, positions, attn_norm_w, w_in, q_norm_w, w_uq, kv_norm_w, w_ukv, attn_out_norm_w, conv_w, conv_b, dt_bias, a_log, ssd_d, ssd_norm_w, w_o, ffn_norm_w, w_router_group, b_router_group, w_router_expert, b_router_expert, w_exp_gate, w_exp_up, w_exp_down, ple_norm_w, w_ple_gate, b_ple_gate, w_ple_proj, ple_post_norm_w, final_norm_w):
    b, s, d = x.shape
    depth = p.shape[0]
    assert depth == 1, "the fused final stage assumes a single layer"
    t = b * s
    inv_freq = 1.0 / (ROPE_THETA ** (jnp.arange(0, MLA_ROPE_DIM, 2, dtype=F32) / MLA_ROPE_DIM))
    half = MLA_ROPE_DIM // 2
    ang = (positions.astype(F32).reshape(t, 1) * inv_freq).reshape(t * half // LANES, LANES)
    cos, sin = lax.optimization_barrier((jnp.cos(ang), jnp.sin(ang)))
    cos = cos.reshape(t, half)
    sin = sin.reshape(t, half)
    ones = jnp.ones((t, MLA_NOPE_DIM), F32)
    zeros = jnp.zeros((t, HEAD_PAD - MLA_NOPE_DIM - MLA_ROPE_DIM), F32)
    cos_t = jnp.concatenate([ones, cos, cos, zeros], axis=1)
    sin_t = jnp.concatenate([0.0 * ones, sin, sin, zeros], axis=1)

    x2 = x.reshape(t, d)
    i = 0
    x1, ym, (pnw, wg, bg, wp, ppnw) = _layer(
        x2, p[i].reshape(t, -1), cos_t, sin_t, b, s, attn_norm_w[i], w_in[i], q_norm_w[i], w_uq[i],
        kv_norm_w[i], w_ukv[i], attn_out_norm_w[i], conv_w[i], conv_b[i], dt_bias[i], a_log[i],
        ssd_d[i], ssd_norm_w[i], w_o[i], ffn_norm_w[i], w_router_group[i], b_router_group[i],
        w_router_expert[i], b_router_expert[i], w_exp_gate[i], w_exp_up[i], w_exp_down[i],
        ple_norm_w[i], w_ple_gate[i], b_ple_gate[i], w_ple_proj[i], ple_post_norm_w[i])
    out = _final(x1, ym, p[i].reshape(t, -1), pnw, wg, bg, wp, ppnw, final_norm_w.reshape(1, -1).astype(F32))
    return out.reshape(b, s, d)
```

```python
import functools
import math

import jax
import jax.numpy as jnp
from jax import lax
from jax.experimental import pallas as pl
from jax.experimental.pallas import tpu as pltpu

F32 = jnp.float32
BF16 = jnp.bfloat16
I32 = jnp.int32

D_MODEL = 1024
PLE_DIM = 256
NORM_EPS = 1e-6

MLA_HEADS = 8
MLA_Q_RANK = 256
MLA_KV_RANK = 128
MLA_NOPE_DIM = 64
MLA_ROPE_DIM = 32
MLA_V_DIM = 64
MLA_OUT = MLA_HEADS * MLA_V_DIM
ROPE_THETA = 10000.0
HEAD_PAD = 128
ONES_ROWS = 16

SSD_HEADS = 8
SSD_HEAD_DIM = 64
SSD_GROUPS = 2
SSD_HEADS_PER_GROUP = SSD_HEADS // SSD_GROUPS
SSD_STATE = 64
SSD_CONV = 5
SSD_CHUNK = 128
SSD_INNER = SSD_HEADS * SSD_HEAD_DIM
SSD_XBC = SSD_INNER + 2 * SSD_GROUPS * SSD_STATE
SSD_DIRECTIONS = 2
CONV_HALO = 8

N_EXPERT_GROUPS = 4
EXPERTS_PER_GROUP = 8
N_EXPERTS = N_EXPERT_GROUPS * EXPERTS_PER_GROUP
D_EXPERT = 256
PAIRS_PER_GROUP = EXPERTS_PER_GROUP * (EXPERTS_PER_GROUP - 1) // 2
N_CLASSES = N_EXPERT_GROUPS * PAIRS_PER_GROUP
ROUTE_ROWS = 48

LANES = 128
ROW_TILES = D_MODEL // LANES
VMEM_LIMIT = 48 * 1024 * 1024

TOKEN_TILE = 512
ATTN_Q_TILE = 1024
ATTN_Q_SUB = 256
ATTN_PIPELINE = {True: (512, 3), False: (256, 4)}
ATTN_LOGIT_BOUND = 80.0
SSD_BLOCK = 512
MOE_TILE = 128


def _rms(x, w):
    ms = jnp.mean(x * x, axis=-1, keepdims=True)
    return x * lax.rsqrt(ms + NORM_EPS) * w


def _dot(a, b):
    return jnp.dot(a, b, preferred_element_type=F32)


def _dot_nt(a, b):
    return lax.dot_general(a, b, (((1,), (1,)), ((), ())), preferred_element_type=F32)


def _split3(x):
    x1 = x.astype(BF16)
    r1 = x - x1.astype(F32)
    x2 = r1.astype(BF16)
    x3 = (r1 - x2.astype(F32)).astype(BF16)
    return x1, x2, x3


def _sigmoid(x):
    return 1.0 / (1.0 + jnp.exp(-x))


def _softplus(x):
    return jnp.maximum(x, 0.0) + jnp.log(1.0 + jnp.exp(-jnp.abs(x)))


def _full(shape):
    nd = len(shape)
    return pl.BlockSpec(shape, lambda *_: (0,) * nd)


def _token_tile_spec(tm):
    return pl.BlockSpec((tm * ROW_TILES, LANES), lambda i: (i, 0))


def _store_token_tiles(ref, x):
    n = x.shape[0]
    for s in range(ROW_TILES):
        ref[pl.ds(s, n, stride=ROW_TILES), :] = x[:, LANES * s:LANES * (s + 1)]


def _load_token_tiles(ref):
    n = ref.shape[0] // ROW_TILES
    return jnp.concatenate([ref[pl.ds(s, n, stride=ROW_TILES), :] for s in range(ROW_TILES)], axis=1)


def _inproj_kernel(x_ref, xprev_ref, xnext_ref, nw_ref, wa_ref, wz_ref, wxbc_ref, cw_ref, cb_ref,
                   qnw_ref, wuq_ref, wuqs_ref, kvnw_ref, wuk_ref, wuv_ref, cos_ref, sin_ref,
                   q_out, k_out, v_out, z_out, act_out, dt_out, conv_ref, *, scale, tiles_per_seq):
    tm = x_ref.shape[0]
    h = _rms(x_ref[...], nw_ref[...]).astype(BF16)
    pa = _dot(h, wa_ref[...])
    z_out[...] = _dot(h, wz_ref[...]).astype(z_out.dtype)
    dt_out[...] = pa[:, MLA_Q_RANK + MLA_KV_RANK + 2 * HEAD_PAD:]

    pos = pl.program_id(0) % tiles_per_seq
    halo = lambda ref: _dot(_rms(ref[...], nw_ref[...]).astype(BF16), wxbc_ref[...])
    conv_ref[0:CONV_HALO, :] = jnp.where(pos > 0, halo(xprev_ref), 0.0)
    conv_ref[CONV_HALO:CONV_HALO + tm, :] = _dot(h, wxbc_ref[...])
    conv_ref[CONV_HALO + tm:2 * CONV_HALO + tm, :] = jnp.where(pos < tiles_per_seq - 1,
                                                               halo(xnext_ref), 0.0)
    acc = jnp.zeros((tm, SSD_XBC), F32) + cb_ref[...]
    base = CONV_HALO - SSD_CONV // 2
    for kk in range(SSD_CONV):
        acc = acc + cw_ref[kk:kk + 1, :] * conv_ref[base + kk:base + kk + tm, :]
    act_out[...] = (acc * _sigmoid(acc)).astype(act_out.dtype)

    cos = cos_ref[...]
    sin = sin_ref[...]
    o1 = MLA_Q_RANK
    o2 = o1 + MLA_KV_RANK
    cqn = _rms(pa[:, :o1], qnw_ref[...]).astype(BF16)
    ckvn = _rms(pa[:, o1:o2], kvnw_ref[...]).astype(BF16)
    q = _dot(cqn, wuq_ref[...])
    qs = _dot(cqn, wuqs_ref[...])
    kn = _dot(ckvn, wuk_ref[...])
    v_out[...] = _dot_nt(wuv_ref[...], ckvn).astype(v_out.dtype)
    kr = pa[:, o2:o2 + HEAD_PAD] * cos + pa[:, o2 + HEAD_PAD:o2 + 2 * HEAD_PAD] * sin
    for hh in range(MLA_HEADS):
        sl = slice(HEAD_PAD * hh, HEAD_PAD * (hh + 1))
        q_out[:, sl] = ((q[:, sl] * cos + qs[:, sl] * sin) * scale).astype(q_out.dtype)
        k_out[:, sl] = (kn[:, sl] + kr).astype(k_out.dtype)


def _inproj(x2, cos_t, sin_t, s, nw, wa, wz, wxbc, cw, cb, qnw, wuq, wuqs, kvnw, wuk, wuv):
    t = x2.shape[0]
    tm = min(TOKEN_TILE, s)
    hb = tm // CONV_HALO
    last = t // CONV_HALO - 1
    scale = (MLA_NOPE_DIM + MLA_ROPE_DIM) ** -0.5 * math.log2(math.e)
    row = lambda w: pl.BlockSpec((tm, w), lambda i: (i, 0))
    weights = (nw, wa, wz, wxbc, cw, cb, qnw, wuq, wuqs, kvnw, wuk, wuv)
    return pl.pallas_call(
        functools.partial(_inproj_kernel, scale=scale, tiles_per_seq=s // tm),
        grid=(t // tm,),
        in_specs=[row(D_MODEL),
                  pl.BlockSpec((CONV_HALO, D_MODEL), lambda i: (jnp.maximum(i * hb - 1, 0), 0)),
                  pl.BlockSpec((CONV_HALO, D_MODEL), lambda i: (jnp.minimum((i + 1) * hb, last), 0))]
        + [_full(w.shape) for w in weights] + [row(HEAD_PAD), row(HEAD_PAD)],
        out_specs=[row(MLA_HEADS * HEAD_PAD), row(MLA_HEADS * HEAD_PAD),
                   pl.BlockSpec((None, MLA_OUT, tm), lambda i: (i, 0, 0)),
                   row(SSD_INNER), row(SSD_XBC), row(LANES)],
        out_shape=[jax.ShapeDtypeStruct((t, MLA_HEADS * HEAD_PAD), BF16),
                   jax.ShapeDtypeStruct((t, MLA_HEADS * HEAD_PAD), BF16),
                   jax.ShapeDtypeStruct((t // tm, MLA_OUT, tm), BF16),
                   jax.ShapeDtypeStruct((t, SSD_INNER), BF16),
                   jax.ShapeDtypeStruct((t, SSD_XBC), BF16),
                   jax.ShapeDtypeStruct((t, LANES), F32)],
        scratch_shapes=[pltpu.VMEM((tm + 2 * CONV_HALO, SSD_XBC), F32)],
        compiler_params=pltpu.CompilerParams(dimension_semantics=("parallel",),
                                             vmem_limit_bytes=VMEM_LIMIT),
        name="inproj",
    )(x2, x2, x2, *weights, cos_t, sin_t)


def _attn_body(q_ref, k_ref, vt_ref, o_ref, bounded):
    tq = q_ref.shape[0]
    nc, _, tv = vt_ref.shape
    tk, ahead = ATTN_PIPELINE[bounded]
    ts = min(ATTN_Q_SUB, tq)
    ones = jnp.ones((ONES_ROWS, tk), BF16)
    streams = []
    for j in range(tq // ts):
        for a in range(2):
            hsl = slice(HEAD_PAD * a, HEAD_PAD * (a + 1))
            vsl = slice(MLA_V_DIM * a, MLA_V_DIM * (a + 1))
            streams.append([hsl, vsl, q_ref[j * ts:(j + 1) * ts, hsl],
                            jnp.full((1, ts), -1e30, F32),
                            jnp.zeros((MLA_V_DIM + ONES_ROWS, ts), F32)])
    units = [(c, i) for c in range(nc * tv // tk) for i in range(len(streams))]
    scores = {}

    def issue(u):
        c, i = units[u]
        scores[u] = _dot_nt(k_ref[c * tk:(c + 1) * tk, streams[i][0]], streams[i][2])

    for u in range(min(ahead, len(units))):
        issue(u)
    for u, (c, i) in enumerate(units):
        if u + ahead < len(units):
            issue(u + ahead)
        st = streams[i]
        _, vsl, _, m, acc = st
        s = scores.pop(u)
        blk, off = divmod(c * tk, tv)
        vt = jnp.concatenate([vt_ref[blk, vsl, off:off + tk], ones], axis=0)
        if bounded:
            st[4] = acc + _dot(vt, jnp.exp2(s).astype(BF16))
        else:
            m_new = jnp.maximum(m, jnp.max(s, axis=0, keepdims=True))
            p = jnp.exp2((s - m_new).astype(BF16))
            st[3] = m_new
            st[4] = acc * jnp.exp2(m - m_new) + _dot(vt, p)
    for j in range(tq // ts):
        halves = [st[4][:MLA_V_DIM] / st[4][MLA_V_DIM:MLA_V_DIM + 1] for st in streams[2 * j:2 * j + 2]]
        o_ref[j * ts:(j + 1) * ts, :] = jnp.concatenate(halves, axis=0).T.astype(o_ref.dtype)


def _attn_kernel(q_ref, k_ref, vt_ref, o_ref, kmax_ref):
    @pl.when(pl.program_id(2) == 0)
    def _():
        for a in range(2):
            kmax_ref[a] = jnp.max(jnp.abs(k_ref[:, HEAD_PAD * a:HEAD_PAD * (a + 1)].astype(F32)))

    bounded = None
    for a in range(2):
        q1 = jnp.sum(jnp.abs(q_ref[:, HEAD_PAD * a:HEAD_PAD * (a + 1)].astype(F32)), axis=1, keepdims=True)
        inside = jnp.max(q1) * kmax_ref[a] <= ATTN_LOGIT_BOUND
        bounded = inside if bounded is None else bounded & inside

    @pl.when(bounded)
    def _():
        _attn_body(q_ref, k_ref, vt_ref, o_ref, True)

    @pl.when(jnp.logical_not(bounded))
    def _():
        _attn_body(q_ref, k_ref, vt_ref, o_ref, False)


def _attention(q, k, vt, b, s):
    tq = min(ATTN_Q_TILE, s)
    nq = s // tq
    pairs = MLA_HEADS // 2
    tk = vt.shape[2]
    nc = s // tk
    return pl.pallas_call(
        _attn_kernel,
        grid=(b, pairs, nq),
        in_specs=[pl.BlockSpec((tq, 2 * HEAD_PAD), lambda bi, pi, qi: (bi * nq + qi, pi)),
                  pl.BlockSpec((s, 2 * HEAD_PAD), lambda bi, pi, qi: (bi, pi)),
                  pl.BlockSpec((nc, 2 * MLA_V_DIM, tk), lambda bi, pi, qi: (bi, pi, 0))],
        out_specs=pl.BlockSpec((tq, 2 * MLA_V_DIM), lambda bi, pi, qi: (bi * nq + qi, pi)),
        out_shape=jax.ShapeDtypeStruct((b * s, MLA_OUT), BF16),
        scratch_shapes=[pltpu.SMEM((2,), F32)],
        compiler_params=pltpu.CompilerParams(
            dimension_semantics=("parallel", "parallel", "arbitrary"),
            vmem_limit_bytes=VMEM_LIMIT),
        name="attention",
    )(q, k, vt)


def _ssd_chunk(act, dt_raw, direction, h_ref, dtbias, a_all, e_mat, skip):
    n = SSD_CHUNK
    gs = SSD_GROUPS * SSD_STATE
    xs16 = act[:, :SSD_INNER]
    bm16 = act[:, SSD_INNER:SSD_INNER + gs]
    cm16 = act[:, SSD_INNER + gs:SSD_INNER + 2 * gs]
    xs = xs16.astype(F32)
    bm = bm16.astype(F32)

    dt_all = _softplus(dt_raw + dtbias)
    a_mat = dt_all * a_all
    ri = lax.broadcasted_iota(I32, (n, n), 0)
    ci = lax.broadcasted_iota(I32, (n, n), 1)
    mask = (ci <= ri) if direction == 0 else (ci >= ri)
    tri = jnp.where(mask, 1.0, 0.0).astype(BF16)
    a1, a2, a3 = _split3(a_mat)
    cs = _dot(tri, a1) + _dot(tri, a2) + _dot(tri, a3)
    end = n - 1 if direction == 0 else 0
    cs_end = cs[end:end + 1, :]
    w_state = dt_all * jnp.exp(cs_end - cs)
    e_off = jnp.exp(cs)
    c_dec = jnp.broadcast_to(jnp.exp(cs_end), (8, LANES))
    stack = jnp.concatenate([w_state, e_off, c_dec], axis=0).astype(BF16)
    expd = _dot(stack, e_mat)
    ws_x = expd[0:n]
    eo_x = expd[n:2 * n]
    cd_x = expd[2 * n:2 * n + 1]

    cs_t = cs.T
    dt_t = dt_all.T
    bm_t = bm.T.astype(BF16)

    pieces = []
    for g in range(SSD_GROUPS):
        gsl = slice(SSD_STATE * g, SSD_STATE * (g + 1))
        cg = cm16[:, gsl]
        gmat = _dot_nt(cg, bm16[:, gsl])
        for r in range(SSD_HEADS_PER_GROUP):
            hh = g * SSD_HEADS_PER_GROUP + r
            c = direction * SSD_HEADS + hh
            seg = cs[:, c:c + 1] - cs_t[c:c + 1, :]
            lm = jnp.where(mask, jnp.exp(jnp.where(mask, seg, 0.0)), 0.0) * dt_t[c:c + 1, :]
            mh = (gmat * lm).astype(BF16)
            pieces.append(_dot(mh, xs16[:, SSD_HEAD_DIM * hh:SSD_HEAD_DIM * (hh + 1)]))
    y = jnp.concatenate(pieces, axis=1)

    w = SSD_HEADS_PER_GROUP * SSD_HEAD_DIM
    offs = []
    for g in range(SSD_GROUPS):
        lsl = slice(w * g, w * (g + 1))
        gsl = slice(SSD_STATE * g, SSD_STATE * (g + 1))
        h_g = h_ref[:, lsl]
        offs.append(_dot(cm16[:, gsl], h_g.astype(BF16)) * eo_x[:, lsl])
        xd = (xs[:, lsl] * ws_x[:, lsl]).astype(BF16)
        h_ref[:, lsl] = h_g * cd_x[:, lsl] + _dot(bm_t[gsl, :], xd)
    y = y + jnp.concatenate(offs, axis=1)
    if skip is not None:
        y = y + xs * skip
    return y


def _ssd_kernel(xf_ref, dtf_ref, xb_ref, dtb_ref, dtbias_ref, alog_ref, skip_ref, e_ref,
                yf_ref, yb_ref, hf_ref, hb_ref):
    i = pl.program_id(1)

    @pl.when(i == 0)
    def _():
        hf_ref[...] = jnp.zeros_like(hf_ref)
        hb_ref[...] = jnp.zeros_like(hb_ref)

    lane = lax.broadcasted_iota(I32, (1, LANES), 1)
    a_all = jnp.where(lane < SSD_DIRECTIONS * SSD_HEADS, -jnp.exp(alog_ref[...]), 0.0)
    dtbias = dtbias_ref[...]
    skip = skip_ref[...]
    nch = xf_ref.shape[0] // SSD_CHUNK

    for c in range(nch):
        rows = slice(SSD_CHUNK * c, SSD_CHUNK * (c + 1))
        y = _ssd_chunk(xf_ref[rows, :], dtf_ref[rows, :], 0, hf_ref, dtbias, a_all, e_ref[0], skip)
        yf_ref[rows, :] = y.astype(yf_ref.dtype)

    for c in reversed(range(nch)):
        rows = slice(SSD_CHUNK * c, SSD_CHUNK * (c + 1))
        y = _ssd_chunk(xb_ref[rows, :], dtb_ref[rows, :], 1, hb_ref, dtbias, a_all, e_ref[1], None)
        yb_ref[rows, :] = y.astype(yb_ref.dtype)


def _ssd(act, dt_raw, dtbias, alog, skip, e_mat, b, s):
    r = min(SSD_BLOCK, s)
    nb = s // r

    def cur(rev):
        return (lambda bi, i: (bi * nb + (nb - 1 - i), 0)) if rev else (lambda bi, i: (bi * nb + i, 0))

    def role(rev):
        return [pl.BlockSpec((r, SSD_XBC), cur(rev)), pl.BlockSpec((r, LANES), cur(rev))]

    consts = (dtbias, alog, skip, e_mat)
    return pl.pallas_call(
        _ssd_kernel,
        grid=(b, nb),
        in_specs=role(False) + role(True) + [_full(c.shape) for c in consts],
        out_specs=[pl.BlockSpec((r, SSD_INNER), cur(False)), pl.BlockSpec((r, SSD_INNER), cur(True))],
        out_shape=[jax.ShapeDtypeStruct((b * s, SSD_INNER), F32)] * 2,
        scratch_shapes=[pltpu.VMEM((SSD_STATE, SSD_INNER), F32),
                        pltpu.VMEM((SSD_STATE, SSD_INNER), F32)],
        compiler_params=pltpu.CompilerParams(dimension_semantics=("parallel", "arbitrary"),
                                             vmem_limit_bytes=VMEM_LIMIT),
        name="ssd",
    )(act, dt_raw, act, dt_raw, *consts)


def _mixout_kernel(x_ref, attn_ref, yf_ref, yb_ref, z_ref, anw_ref, snw_ref, wo_ref, fnw_ref,
                   wr_ref, br_ref, x1_out, route_out):
    attn = _rms(attn_ref[...].astype(F32), anw_ref[...])
    z = z_ref[...].astype(F32)
    y = (yf_ref[...] + yb_ref[...]) * (z * _sigmoid(z))
    y = _rms(y, snw_ref[...])
    mix = jnp.concatenate([attn, y], axis=1).astype(BF16)
    x1 = x_ref[...] + _dot(mix, wo_ref[...])
    _store_token_tiles(x1_out, x1)

    h = _rms(x1, fnw_ref[...])
    h1, h2, _ = _split3(h)
    two = _dot_nt(wr_ref[...], h1) + _dot_nt(wr_ref[...], h2)
    lt = two[:ROUTE_ROWS] + two[ROUTE_ROWS:] + br_ref[...]

    epg = EXPERTS_PER_GROUP
    sub = lax.broadcasted_iota(I32, (epg, lt.shape[1]), 0)
    ninf = -jnp.inf

    def argmax_first(vals):
        vmax = jnp.max(vals, axis=0, keepdims=True)
        return jnp.min(jnp.where(vals == vmax, sub, epg), axis=0, keepdims=True)

    gidx = argmax_first(jnp.where(sub < N_EXPERT_GROUPS, lt[:epg], ninf))
    sel = lt[epg:2 * epg]
    for g in range(1, N_EXPERT_GROUPS):
        sel = jnp.where(gidx == g, lt[epg * (g + 1):epg * (g + 2)], sel)
    i1 = argmax_first(sel)
    i2 = argmax_first(jnp.where(sub == i1, ninf, sel))
    elo = jnp.minimum(i1, i2)
    ehi = jnp.maximum(i1, i2)
    pair = (elo * (2 * epg - 1 - elo)) // 2 + (ehi - elo - 1)
    cls = gidx * PAIRS_PER_GROUP + pair
    route_out[...] = jnp.broadcast_to(cls.astype(F32), route_out.shape)


def _mixout(x2, attn, yf, yb, z, anw, snw, wo, fnw, wr, br):
    t = x2.shape[0]
    tm = min(TOKEN_TILE, t)
    row = lambda w: pl.BlockSpec((tm, w), lambda i: (i, 0))
    br = jnp.broadcast_to(br, (ROUTE_ROWS, tm))
    weights = (anw, snw, wo, fnw, wr, br)
    return pl.pallas_call(
        _mixout_kernel,
        grid=(t // tm,),
        in_specs=[row(D_MODEL), row(MLA_OUT), row(SSD_INNER), row(SSD_INNER), row(SSD_INNER)]
        + [_full(w.shape) for w in weights],
        out_specs=[_token_tile_spec(tm), pl.BlockSpec((8, tm), lambda i: (i, 0))],
        out_shape=[jax.ShapeDtypeStruct((t * ROW_TILES, LANES), F32),
                   jax.ShapeDtypeStruct((t // tm * 8, tm), F32)],
        compiler_params=pltpu.CompilerParams(dimension_semantics=("parallel",),
                                             vmem_limit_bytes=VMEM_LIMIT),
        name="mixout",
    )(x2, attn, yf, yb, z, *weights)


def _moe_kernel(tile_lo_ref, tile_hi_ref, tile_j_ref, tile_n_ref, order_ref,
                x1_hbm, fnw_ref, wr_ref, br_ref, *rest):
    weight_refs = rest[:4]
    y_hbm, xbuf, obuf, sem_in, sem_out = rest[4:]
    tile = xbuf.shape[1] // ROW_TILES
    n_tok = y_hbm.shape[0] - 2 * tile

    def token_rows(r):
        return pl.ds(pl.multiple_of(r * ROW_TILES, ROW_TILES), ROW_TILES)

    def gather_start(tt, sl, inline):
        j0 = tile_j_ref[tt]

        def one(r, priority=0):
            tok = order_ref[j0 + r]
            pltpu.make_async_copy(x1_hbm.at[tok], xbuf.at[sl, token_rows(r)],
                                  sem_in.at[sl]).start(priority=priority)

        if inline:
            for r in range(tile):
                one(r, 1)
        else:
            def body(r, c):
                one(r)
                return c
            lax.fori_loop(0, tile, body, 0, unroll=8)

    def gather_wait(sl):
        pltpu.make_async_copy(xbuf.at[sl], xbuf.at[sl], sem_in.at[sl]).wait()

    def scatter_start(j0, n, sl, inline):
        def one(r, priority=0):
            tok = jnp.where(r < n, order_ref[j0 + r], n_tok + sl * tile + r)
            pltpu.make_async_copy(obuf.at[sl, token_rows(r)], y_hbm.at[tok],
                                  sem_out.at[sl]).start(priority=priority)

        if inline:
            for r in range(tile):
                one(r, r % 2)
        else:
            def body(r, c):
                one(r)
                return c
            lax.fori_loop(0, tile, body, 0, unroll=8)

    def scatter_wait(sl):
        pltpu.make_async_copy(obuf.at[sl], obuf.at[sl], sem_out.at[sl]).wait()

    def one_tile(t, slot, wgu_lo_ref, wdn_lo_ref, wgu_hi_ref, wdn_hi_ref):
        prev = jnp.maximum(t - 1, 0)
        valid = tile_n_ref[t] > 0
        prev_valid = (t > 0) & (tile_n_ref[prev] > 0)

        @pl.when(t == 0)
        def _():
            obuf[...] = jnp.zeros_like(obuf)
            scatter_start(0, 0, 0, inline=False)
            scatter_wait(0)

            @pl.when(valid)
            def _():
                gather_start(t, slot, inline=False)

        @pl.when(jnp.logical_not(valid) & prev_valid)
        def _():
            gather_wait(slot)
            scatter_wait(slot)
            scatter_start(tile_j_ref[prev], tile_n_ref[prev], 1 - slot, inline=False)
            scatter_wait(1 - slot)

        @pl.when(valid)
        def _():
            gather_wait(slot)
            gather_start(t + 1, 1 - slot, inline=True)
            scatter_start(tile_j_ref[prev], jnp.where(t > 0, tile_n_ref[prev], 0), 1 - slot,
                          inline=True)

            h = _rms(_load_token_tiles(xbuf.at[slot]), fnw_ref[...]).astype(BF16)
            logits = _dot(h, wr_ref[...]) + br_ref[...]
            lane = lax.broadcasted_iota(I32, logits.shape, 1)
            gl = jnp.where(lane < N_EXPERT_GROUPS, logits, -jnp.inf)
            gweight = 1.0 / jnp.sum(jnp.exp(gl - jnp.max(gl, axis=-1, keepdims=True)),
                                    axis=-1, keepdims=True)
            pick = lambda e: jnp.sum(jnp.where(lane == N_EXPERT_GROUPS + e, logits, 0.0),
                                     axis=-1, keepdims=True)
            l_lo = pick(tile_lo_ref[t])
            l_hi = pick(tile_hi_ref[t])
            wts = (gweight / (1.0 + jnp.exp(l_hi - l_lo)), gweight / (1.0 + jnp.exp(l_lo - l_hi)))
            acc = None
            for half, wgu_ref, wdn_ref in ((0, wgu_lo_ref, wdn_lo_ref), (1, wgu_hi_ref, wdn_hi_ref)):
                gu = _dot(h, wgu_ref[...])
                g = gu[:, :D_EXPERT]
                he = (g * _sigmoid(g) * gu[:, D_EXPERT:] * wts[half]).astype(BF16)
                d = _dot(he, wdn_ref[...])
                acc = d if acc is None else acc + d

            @pl.when(t > 0)
            def _():
                scatter_wait(slot)
            _store_token_tiles(obuf.at[slot], acc)

    one_tile(pl.program_id(0), pl.program_id(0) % 2, *weight_refs)


def _moe(x1t, order, tile_lo, tile_hi, tile_j, tile_n, fnw, wr, br, wgu, wdn):
    t = x1t.shape[0]
    n_tiles = tile_lo.shape[0]
    tile = MOE_TILE
    wspec = lambda shape, which: pl.BlockSpec(
        (None,) + shape, lambda i, lo, hi, tj, tn, od: ((lo, hi)[which][i], 0, 0))
    grid_spec = pltpu.PrefetchScalarGridSpec(
        num_scalar_prefetch=5,
        grid=(n_tiles,),
        in_specs=[pl.BlockSpec(memory_space=pl.ANY),
                  pl.BlockSpec((1, D_MODEL), lambda i, *_: (0, 0)),
                  pl.BlockSpec((D_MODEL, LANES), lambda i, *_: (0, 0)),
                  pl.BlockSpec((1, LANES), lambda i, *_: (0, 0)),
                  wspec((D_MODEL, 2 * D_EXPERT), 0), wspec((D_EXPERT, D_MODEL), 0),
                  wspec((D_MODEL, 2 * D_EXPERT), 1), wspec((D_EXPERT, D_MODEL), 1)],
        out_specs=pl.BlockSpec(memory_space=pl.ANY),
        scratch_shapes=[pltpu.VMEM((2, tile * ROW_TILES, LANES), F32),
                        pltpu.VMEM((2, tile * ROW_TILES, LANES), F32),
                        pltpu.SemaphoreType.DMA((2,)), pltpu.SemaphoreType.DMA((2,))],
    )
    return pl.pallas_call(
        _moe_kernel,
        grid_spec=grid_spec,
        out_shape=jax.ShapeDtypeStruct((t + 2 * tile, ROW_TILES, LANES), F32),
        compiler_params=pltpu.CompilerParams(dimension_semantics=("arbitrary",),
                                             vmem_limit_bytes=VMEM_LIMIT),
        name="moe",
    )(tile_lo, tile_hi, tile_j, tile_n, order, x1t, fnw, wr, br, wgu, wdn, wgu, wdn)


def _moe_plan(route, t):
    tile = MOE_TILE
    n_tiles = t // tile + N_CLASSES + 1
    cls = route.reshape(-1, 8, route.shape[1])[:, 0, :].reshape(t).astype(I32)
    _, order = lax.sort((cls, jnp.arange(t, dtype=I32)), num_keys=1, is_stable=True)
    order = jnp.concatenate([order, jnp.zeros((tile,), I32)])
    class_ids = jnp.arange(N_CLASSES, dtype=I32)
    counts = jnp.sum((cls[:, None] == class_ids[None, :]).astype(I32), axis=0)
    tiles_per = (counts + tile - 1) // tile
    tile_end = jnp.cumsum(tiles_per)
    tile_begin = tile_end - tiles_per
    starts = jnp.cumsum(counts) - counts
    t_idx = jnp.arange(n_tiles, dtype=I32)
    tile_cls = jnp.sum((tile_end[None, :] <= t_idx[:, None]).astype(I32), axis=1)
    valid = t_idx < tile_end[-1]
    last_cls = jnp.max(jnp.where(counts > 0, class_ids, 0))
    tile_cls = jnp.where(valid, tile_cls, last_cls)
    onehot = (tile_cls[:, None] == class_ids[None, :]).astype(I32)
    pick = lambda v: jnp.sum(onehot * v[None, :], axis=1)
    k = t_idx - pick(tile_begin)
    tile_j = jnp.where(valid, pick(starts) + k * tile, 0)
    tile_n = jnp.where(valid, jnp.clip(pick(counts) - k * tile, 0, tile), 0)
    lo_of_pair, hi_of_pair = [], []
    for lo in range(EXPERTS_PER_GROUP):
        for hi in range(lo + 1, EXPERTS_PER_GROUP):
            lo_of_pair.append(lo)
            hi_of_pair.append(hi)
    grp = class_ids // PAIRS_PER_GROUP
    class_lo = grp * EXPERTS_PER_GROUP + jnp.asarray(lo_of_pair * N_EXPERT_GROUPS, I32)
    class_hi = grp * EXPERTS_PER_GROUP + jnp.asarray(hi_of_pair * N_EXPERT_GROUPS, I32)
    return order, pick(class_lo), pick(class_hi), tile_j.astype(I32), tile_n.astype(I32)


def _final_kernel(x1_ref, ym_ref, p_ref, pnw_ref, wg_ref, bg_ref, wp_ref, ppnw_ref, fnw_ref, o_ref):
    x2 = _load_token_tiles(x1_ref) + _load_token_tiles(ym_ref)
    gate = _sigmoid(_dot(_rms(x2, pnw_ref[...]).astype(BF16), wg_ref[...]) + bg_ref[...])
    ple = _rms(_dot(p_ref[...].astype(BF16), wp_ref[...]), ppnw_ref[...])
    o_ref[...] = _rms(x2 + gate * ple, fnw_ref[...])


def _final(x1, ym, p2, pnw, wg, bg, wp, ppnw, fnw):
    t = p2.shape[0]
    tm = min(TOKEN_TILE, t)
    row = lambda w: pl.BlockSpec((tm, w), lambda i: (i, 0))
    weights = (pnw, wg, bg, wp, ppnw, fnw)
    return pl.pallas_call(
        _final_kernel,
        grid=(t // tm,),
        in_specs=[_token_tile_spec(tm), _token_tile_spec(tm), row(PLE_DIM)]
        + [_full(w.shape) for w in weights],
        out_specs=row(D_MODEL),
        out_shape=jax.ShapeDtypeStruct((t, D_MODEL), F32),
        compiler_params=pltpu.CompilerParams(dimension_semantics=("parallel",),
                                             vmem_limit_bytes=VMEM_LIMIT),
        name="final",
    )(x1, ym, p2, *weights)


def _pad_cols(w, width, offset=0):
    out = jnp.zeros((w.shape[0], width), w.dtype)
    return out.at[:, offset:offset + w.shape[1]].set(w)


def _rope_swap(w):
    half = MLA_ROPE_DIM // 2
    return jnp.concatenate([-w[..., half:], w[..., :half]], axis=-1)


def _layer(x2, p2, cos_t, sin_t, b, s, attn_norm_w, w_in, q_norm_w, w_uq, kv_norm_w, w_ukv,
           attn_out_norm_w, conv_w, conv_b, dt_bias, a_log, ssd_d, ssd_norm_w, w_o, ffn_norm_w,
           w_router_group, b_router_group, w_router_expert, b_router_expert, w_exp_gate,
           w_exp_up, w_exp_down, ple_norm_w, w_ple_gate, b_ple_gate, w_ple_proj, ple_post_norm_w):
    t = x2.shape[0]
    r1 = lambda v: v.reshape(1, -1).astype(F32)
    o1 = MLA_Q_RANK
    o2 = o1 + MLA_KV_RANK
    o3 = o2 + MLA_ROPE_DIM
    o4 = o3 + SSD_INNER
    o5 = o4 + SSD_XBC
    w_kr = w_in[:, o2:o3]
    wa = jnp.concatenate([w_in[:, :o2], _pad_cols(w_kr, HEAD_PAD, MLA_NOPE_DIM),
                          _pad_cols(_rope_swap(w_kr), HEAD_PAD, MLA_NOPE_DIM),
                          _pad_cols(w_in[:, o5:], LANES)], axis=1).astype(BF16)
    wz = w_in[:, o3:o4].astype(BF16)
    wxbc = w_in[:, o4:o5].astype(BF16)
    uq = w_uq.reshape(MLA_Q_RANK, MLA_HEADS, MLA_NOPE_DIM + MLA_ROPE_DIM)
    zq = jnp.zeros((MLA_Q_RANK, MLA_HEADS, HEAD_PAD - MLA_NOPE_DIM - MLA_ROPE_DIM), F32)
    wuq = jnp.concatenate([uq, zq], axis=-1).reshape(MLA_Q_RANK, -1).astype(BF16)
    wuqs = jnp.concatenate([jnp.zeros_like(uq[..., :MLA_NOPE_DIM]), _rope_swap(uq[..., MLA_NOPE_DIM:]),
                            zq], axis=-1).reshape(MLA_Q_RANK, -1).astype(BF16)
    ukv = w_ukv.reshape(MLA_KV_RANK, MLA_HEADS, MLA_NOPE_DIM + MLA_V_DIM)
    zk = jnp.zeros((MLA_KV_RANK, MLA_HEADS, HEAD_PAD - MLA_NOPE_DIM), F32)
    wuk = jnp.concatenate([ukv[..., :MLA_NOPE_DIM], zk], axis=-1).reshape(MLA_KV_RANK, -1).astype(BF16)
    wuv = ukv[..., MLA_NOPE_DIM:].reshape(MLA_KV_RANK, -1).T.astype(BF16)

    q, k, v, z, act, dt_raw = _inproj(x2, cos_t, sin_t, s, r1(attn_norm_w), wa, wz, wxbc,
                                      conv_w.astype(F32), r1(conv_b), r1(q_norm_w), wuq, wuqs,
                                      r1(kv_norm_w), wuk, wuv)
    attn = _attention(q, k, v, b, s)

    head_of_lane = jnp.arange(SSD_INNER) // SSD_HEAD_DIM
    rows = jnp.arange(LANES)[:, None]
    e_mat = jnp.stack([(rows == d * SSD_HEADS + head_of_lane[None, :]) for d in range(SSD_DIRECTIONS)]
                      ).astype(BF16)
    skip = jnp.repeat(ssd_d.astype(F32), SSD_HEAD_DIM).reshape(1, -1)
    yf, yb = _ssd(act, dt_raw, _pad_cols(r1(dt_bias), LANES), _pad_cols(r1(a_log), LANES), skip,
                  e_mat, b, s)

    wr = _pad_cols(jnp.concatenate([w_router_group, w_router_expert], axis=1).astype(F32), LANES)
    wr_hi = wr.astype(BF16)
    br = _pad_cols(jnp.concatenate([r1(b_router_group), r1(b_router_expert)], axis=1), LANES)
    gpad = jnp.zeros((EXPERTS_PER_GROUP - N_EXPERT_GROUPS, D_MODEL), F32)
    wrt = jnp.concatenate([w_router_group.T.astype(F32), gpad, w_router_expert.T.astype(F32),
                           jnp.zeros((ROUTE_ROWS - EXPERTS_PER_GROUP - N_EXPERTS, D_MODEL), F32)])
    wrt_hi, wrt_lo, _ = _split3(wrt)
    brt = jnp.concatenate([b_router_group.astype(F32), gpad[:, 0], b_router_expert.astype(F32),
                           jnp.zeros((ROUTE_ROWS - EXPERTS_PER_GROUP - N_EXPERTS,), F32)])
    x1, route = _mixout(x2, attn, yf, yb, z, r1(attn_out_norm_w), r1(ssd_norm_w), w_o.astype(BF16),
                        r1(ffn_norm_w), jnp.concatenate([wrt_hi, wrt_lo]), brt.reshape(-1, 1))

    order, tile_lo, tile_hi, tile_j, tile_n = _moe_plan(route, t)
    wgu = jnp.concatenate([w_exp_gate, w_exp_up], axis=-1).astype(BF16)
    ym = _moe(x1.reshape(t, ROW_TILES, LANES), order, tile_lo, tile_hi, tile_j, tile_n,
              r1(ffn_norm_w), wr_hi, br, wgu, w_exp_down.astype(BF16))
    ym = ym.reshape(ym.shape[0] * ROW_TILES, LANES)
    return x1, ym, (r1(ple_norm_w), w_ple_gate.astype(BF16), r1(b_ple_gate), w_ple_proj.astype(BF16),
                    r1(ple_post_norm_w))


def kernel(x, p, positions, attn_norm_w, w_in, q_norm_w, w_uq, kv_norm_w, w_ukv, attn_out_norm_w, conv_w, conv_b, dt_bias, a_log, ssd_d, ssd_norm_w, w_o, ffn_norm_w, w_router_group, b_router_group, w_router_expert, b_router_expert, w_exp_gate, w_exp_up, w_exp_down, ple_norm_w, w_ple_gate, b_ple_gate, w_ple_proj, ple_post_norm_w, final_norm_w):
    b, s, d = x.shape
    depth = p.shape[0]
    assert depth == 1, "the fused final stage assumes a single layer"
    t = b * s
    inv_freq = 1.0 / (ROPE_THETA ** (jnp.arange(0, MLA_ROPE_DIM, 2, dtype=F32) / MLA_ROPE_DIM))
    half = MLA_ROPE_DIM // 2
    ang = (positions.astype(F32).reshape(t, 1) * inv_freq).reshape(t * half // LANES, LANES)
    cos, sin = lax.optimization_barrier((jnp.cos(ang), jnp.sin(ang)))
    cos = cos.reshape(t, half)
    sin = sin.reshape(t, half)
    ones = jnp.ones((t, MLA_NOPE_DIM), F32)
    zeros = jnp.zeros((t, HEAD_PAD - MLA_NOPE_DIM - MLA_ROPE_DIM), F32)
    cos_t = jnp.concatenate([ones, cos, cos, zeros], axis=1)
    sin_t = jnp.concatenate([0.0 * ones, sin, sin, zeros], axis=1)

    x2 = x.reshape(t, d)
    i = 0
    x1, ym, (pnw, wg, bg, wp, ppnw) = _layer(
        x2, p[i].reshape(t, -1), cos_t, sin_t, b, s, attn_norm_w[i], w_in[i], q_norm_w[i], w_uq[i],
        kv_norm_w[i], w_ukv[i], attn_out_norm_w[i], conv_w[i], conv_b[i], dt_bias[i], a_log[i],
        ssd_d[i], ssd_norm_w[i], w_o[i], ffn_norm_w[i], w_router_group[i], b_router_group[i],
        w_router_expert[i], b_router_expert[i], w_exp_gate[i], w_exp_up[i], w_exp_down[i],
        ple_norm_w[i], w_ple_gate[i], b_ple_gate[i], w_ple_proj[i], ple_post_norm_w[i])
    out = _final(x1, ym, p[i].reshape(t, -1), pnw, wg, bg, wp, ppnw, final_norm_w.reshape(1, -1).astype(F32))
    return out.reshape(b, s, d)
```

```python
import functools
import math

import jax
import jax.numpy as jnp
from jax import lax
from jax.experimental import pallas as pl
from jax.experimental.pallas import tpu as pltpu

F32 = jnp.float32
BF16 = jnp.bfloat16
I32 = jnp.int32

D_MODEL = 1024
PLE_DIM = 256
NORM_EPS = 1e-6

MLA_HEADS = 8
MLA_Q_RANK = 256
MLA_KV_RANK = 128
MLA_NOPE_DIM = 64
MLA_ROPE_DIM = 32
MLA_V_DIM = 64
MLA_OUT = MLA_HEADS * MLA_V_DIM
ROPE_THETA = 10000.0
HEAD_PAD = 128
ONES_ROWS = 16

SSD_HEADS = 8
SSD_HEAD_DIM = 64
SSD_GROUPS = 2
SSD_HEADS_PER_GROUP = SSD_HEADS // SSD_GROUPS
SSD_STATE = 64
SSD_CONV = 5
SSD_CHUNK = 128
SSD_INNER = SSD_HEADS * SSD_HEAD_DIM
SSD_XBC = SSD_INNER + 2 * SSD_GROUPS * SSD_STATE
SSD_DIRECTIONS = 2
CONV_HALO = 8

N_EXPERT_GROUPS = 4
EXPERTS_PER_GROUP = 8
N_EXPERTS = N_EXPERT_GROUPS * EXPERTS_PER_GROUP
D_EXPERT = 256
PAIRS_PER_GROUP = EXPERTS_PER_GROUP * (EXPERTS_PER_GROUP - 1) // 2
N_CLASSES = N_EXPERT_GROUPS * PAIRS_PER_GROUP
ROUTE_ROWS = 48

LANES = 128
ROW_TILES = D_MODEL // LANES
VMEM_LIMIT = 48 * 1024 * 1024

TOKEN_TILE = 512
ATTN_Q_TILE = 2048
ATTN_Q_SUB = 256
ATTN_PIPELINE = {True: (512, 3), False: (256, 4)}
ATTN_LOGIT_BOUND = 80.0
SSD_BLOCK = 512
MOE_TILE = 256


def _rms(x, w):
    ms = jnp.mean(x * x, axis=-1, keepdims=True)
    return x * lax.rsqrt(ms + NORM_EPS) * w


def _dot(a, b):
    return jnp.dot(a, b, preferred_element_type=F32)


def _dot_nt(a, b):
    return lax.dot_general(a, b, (((1,), (1,)), ((), ())), preferred_element_type=F32)


def _split3(x):
    x1 = x.astype(BF16)
    r1 = x - x1.astype(F32)
    x2 = r1.astype(BF16)
    x3 = (r1 - x2.astype(F32)).astype(BF16)
    return x1, x2, x3


def _sigmoid(x):
    return 1.0 / (1.0 + jnp.exp(-x))


def _softplus(x):
    return jnp.maximum(x, 0.0) + jnp.log(1.0 + jnp.exp(-jnp.abs(x)))


def _full(shape):
    nd = len(shape)
    return pl.BlockSpec(shape, lambda *_: (0,) * nd)


def _token_tile_spec(tm):
    return pl.BlockSpec((tm * ROW_TILES, LANES), lambda i: (i, 0))


def _store_token_tiles(ref, x):
    n = x.shape[0]
    for s in range(ROW_TILES):
        ref[pl.ds(s, n, stride=ROW_TILES), :] = x[:, LANES * s:LANES * (s + 1)]


def _load_token_tiles(ref):
    n = ref.shape[0] // ROW_TILES
    return jnp.concatenate([ref[pl.ds(s, n, stride=ROW_TILES), :] for s in range(ROW_TILES)], axis=1)


def _inproj_kernel(x_ref, xprev_ref, xnext_ref, nw_ref, wa_ref, wz_ref, wxbc_ref, cw_ref, cb_ref,
                   qnw_ref, wuq_ref, wuqs_ref, kvnw_ref, wuk_ref, wuv_ref, cos_ref, sin_ref,
                   q_out, k_out, v_out, z_out, act_out, dt_out, conv_ref, *, scale, tiles_per_seq):
    tm = x_ref.shape[0]
    h = _rms(x_ref[...], nw_ref[...]).astype(BF16)
    pa = _dot(h, wa_ref[...])
    z_out[...] = _dot(h, wz_ref[...]).astype(z_out.dtype)
    dt_out[...] = pa[:, MLA_Q_RANK + MLA_KV_RANK + 2 * HEAD_PAD:]

    pos = pl.program_id(0) % tiles_per_seq
    halo = lambda ref: _dot(_rms(ref[...], nw_ref[...]).astype(BF16), wxbc_ref[...])
    conv_ref[0:CONV_HALO, :] = jnp.where(pos > 0, halo(xprev_ref), 0.0)
    conv_ref[CONV_HALO:CONV_HALO + tm, :] = _dot(h, wxbc_ref[...])
    conv_ref[CONV_HALO + tm:2 * CONV_HALO + tm, :] = jnp.where(pos < tiles_per_seq - 1,
                                                               halo(xnext_ref), 0.0)
    acc = jnp.zeros((tm, SSD_XBC), F32) + cb_ref[...]
    base = CONV_HALO - SSD_CONV // 2
    for kk in range(SSD_CONV):
        acc = acc + cw_ref[kk:kk + 1, :] * conv_ref[base + kk:base + kk + tm, :]
    act_out[...] = (acc * _sigmoid(acc)).astype(act_out.dtype)

    cos = cos_ref[...]
    sin = sin_ref[...]
    o1 = MLA_Q_RANK
    o2 = o1 + MLA_KV_RANK
    cqn = _rms(pa[:, :o1], qnw_ref[...]).astype(BF16)
    ckvn = _rms(pa[:, o1:o2], kvnw_ref[...]).astype(BF16)
    q = _dot(cqn, wuq_ref[...])
    qs = _dot(cqn, wuqs_ref[...])
    kn = _dot(ckvn, wuk_ref[...])
    v_out[...] = _dot_nt(wuv_ref[...], ckvn).astype(v_out.dtype)
    kr = pa[:, o2:o2 + HEAD_PAD] * cos + pa[:, o2 + HEAD_PAD:o2 + 2 * HEAD_PAD] * sin
    for hh in range(MLA_HEADS):
        sl = slice(HEAD_PAD * hh, HEAD_PAD * (hh + 1))
        q_out[:, sl] = ((q[:, sl] * cos + qs[:, sl] * sin) * scale).astype(q_out.dtype)
        k_out[:, sl] = (kn[:, sl] + kr).astype(k_out.dtype)


def _inproj(x2, cos_t, sin_t, s, nw, wa, wz, wxbc, cw, cb, qnw, wuq, wuqs, kvnw, wuk, wuv):
    t = x2.shape[0]
    tm = min(TOKEN_TILE, s)
    hb = tm // CONV_HALO
    last = t // CONV_HALO - 1
    scale = (MLA_NOPE_DIM + MLA_ROPE_DIM) ** -0.5 * math.log2(math.e)
    row = lambda w: pl.BlockSpec((tm, w), lambda i: (i, 0))
    weights = (nw, wa, wz, wxbc, cw, cb, qnw, wuq, wuqs, kvnw, wuk, wuv)
    return pl.pallas_call(
        functools.partial(_inproj_kernel, scale=scale, tiles_per_seq=s // tm),
        grid=(t // tm,),
        in_specs=[row(D_MODEL),
                  pl.BlockSpec((CONV_HALO, D_MODEL), lambda i: (jnp.maximum(i * hb - 1, 0), 0)),
                  pl.BlockSpec((CONV_HALO, D_MODEL), lambda i: (jnp.minimum((i + 1) * hb, last), 0))]
        + [_full(w.shape) for w in weights] + [row(HEAD_PAD), row(HEAD_PAD)],
        out_specs=[row(MLA_HEADS * HEAD_PAD), row(MLA_HEADS * HEAD_PAD),
                   pl.BlockSpec((None, MLA_OUT, tm), lambda i: (i, 0, 0)),
                   row(SSD_INNER), row(SSD_XBC), row(LANES)],
        out_shape=[jax.ShapeDtypeStruct((t, MLA_HEADS * HEAD_PAD), BF16),
                   jax.ShapeDtypeStruct((t, MLA_HEADS * HEAD_PAD), BF16),
                   jax.ShapeDtypeStruct((t // tm, MLA_OUT, tm), BF16),
                   jax.ShapeDtypeStruct((t, SSD_INNER), BF16),
                   jax.ShapeDtypeStruct((t, SSD_XBC), BF16),
                   jax.ShapeDtypeStruct((t, LANES), F32)],
        scratch_shapes=[pltpu.VMEM((tm + 2 * CONV_HALO, SSD_XBC), F32)],
        compiler_params=pltpu.CompilerParams(dimension_semantics=("parallel",),
                                             vmem_limit_bytes=VMEM_LIMIT),
        name="inproj",
    )(x2, x2, x2, *weights, cos_t, sin_t)


def _attn_body(q_ref, k_ref, vt_ref, o_ref, bounded):
    tq = q_ref.shape[0]
    nc, _, tv = vt_ref.shape
    tk, ahead = ATTN_PIPELINE[bounded]
    ts = min(ATTN_Q_SUB, tq)
    ones = jnp.ones((ONES_ROWS, tk), BF16)
    streams = []
    for j in range(tq // ts):
        for a in range(2):
            hsl = slice(HEAD_PAD * a, HEAD_PAD * (a + 1))
            vsl = slice(MLA_V_DIM * a, MLA_V_DIM * (a + 1))
            streams.append([hsl, vsl, q_ref[j * ts:(j + 1) * ts, hsl],
                            jnp.full((1, ts), -1e30, F32),
                            jnp.zeros((MLA_V_DIM + ONES_ROWS, ts), F32)])
    units = [(c, i) for c in range(nc * tv // tk) for i in range(len(streams))]
    scores = {}

    def issue(u):
        c, i = units[u]
        scores[u] = _dot_nt(k_ref[c * tk:(c + 1) * tk, streams[i][0]], streams[i][2])

    for u in range(min(ahead, len(units))):
        issue(u)
    for u, (c, i) in enumerate(units):
        if u + ahead < len(units):
            issue(u + ahead)
        st = streams[i]
        _, vsl, _, m, acc = st
        s = scores.pop(u)
        blk, off = divmod(c * tk, tv)
        vt = jnp.concatenate([vt_ref[blk, vsl, off:off + tk], ones], axis=0)
        if bounded:
            st[4] = acc + _dot(vt, jnp.exp2(s).astype(BF16))
        else:
            m_new = jnp.maximum(m, jnp.max(s, axis=0, keepdims=True))
            p = jnp.exp2((s - m_new).astype(BF16))
            st[3] = m_new
            st[4] = acc * jnp.exp2(m - m_new) + _dot(vt, p)
    for j in range(tq // ts):
        halves = [st[4][:MLA_V_DIM] / st[4][MLA_V_DIM:MLA_V_DIM + 1] for st in streams[2 * j:2 * j + 2]]
        o_ref[j * ts:(j + 1) * ts, :] = jnp.concatenate(halves, axis=0).T.astype(o_ref.dtype)


def _attn_kernel(q_ref, k_ref, vt_ref, o_ref, kmax_ref):
    @pl.when(pl.program_id(2) == 0)
    def _():
        for a in range(2):
            kmax_ref[a] = jnp.max(jnp.abs(k_ref[:, HEAD_PAD * a:HEAD_PAD * (a + 1)].astype(F32)))

    bounded = None
    for a in range(2):
        q1 = jnp.sum(jnp.abs(q_ref[:, HEAD_PAD * a:HEAD_PAD * (a + 1)].astype(F32)), axis=1, keepdims=True)
        inside = jnp.max(q1) * kmax_ref[a] <= ATTN_LOGIT_BOUND
        bounded = inside if bounded is None else bounded & inside

    @pl.when(bounded)
    def _():
        _attn_body(q_ref, k_ref, vt_ref, o_ref, True)

    @pl.when(jnp.logical_not(bounded))
    def _():
        _attn_body(q_ref, k_ref, vt_ref, o_ref, False)


def _attention(q, k, vt, b, s):
    tq = min(ATTN_Q_TILE, s)
    nq = s // tq
    pairs = MLA_HEADS // 2
    tk = vt.shape[2]
    nc = s // tk
    return pl.pallas_call(
        _attn_kernel,
        grid=(b, pairs, nq),
        in_specs=[pl.BlockSpec((tq, 2 * HEAD_PAD), lambda bi, pi, qi: (bi * nq + qi, pi)),
                  pl.BlockSpec((s, 2 * HEAD_PAD), lambda bi, pi, qi: (bi, pi)),
                  pl.BlockSpec((nc, 2 * MLA_V_DIM, tk), lambda bi, pi, qi: (bi, pi, 0))],
        out_specs=pl.BlockSpec((tq, 2 * MLA_V_DIM), lambda bi, pi, qi: (bi * nq + qi, pi)),
        out_shape=jax.ShapeDtypeStruct((b * s, MLA_OUT), BF16),
        scratch_shapes=[pltpu.SMEM((2,), F32)],
        compiler_params=pltpu.CompilerParams(
            dimension_semantics=("parallel", "parallel", "arbitrary"),
            vmem_limit_bytes=VMEM_LIMIT),
        name="attention",
    )(q, k, vt)


def _ssd_chunk(act, dt_raw, direction, h_ref, dtbias, a_all, e_mat, skip):
    n = SSD_CHUNK
    gs = SSD_GROUPS * SSD_STATE
    xs16 = act[:, :SSD_INNER]
    bm16 = act[:, SSD_INNER:SSD_INNER + gs]
    cm16 = act[:, SSD_INNER + gs:SSD_INNER + 2 * gs]
    xs = xs16.astype(F32)
    bm = bm16.astype(F32)

    dt_all = _softplus(dt_raw + dtbias)
    a_mat = dt_all * a_all
    ri = lax.broadcasted_iota(I32, (n, n), 0)
    ci = lax.broadcasted_iota(I32, (n, n), 1)
    mask = (ci <= ri) if direction == 0 else (ci >= ri)
    tri = jnp.where(mask, 1.0, 0.0).astype(BF16)
    a1, a2, a3 = _split3(a_mat)
    cs = _dot(tri, a1) + _dot(tri, a2) + _dot(tri, a3)
    end = n - 1 if direction == 0 else 0
    cs_end = cs[end:end + 1, :]
    w_state = dt_all * jnp.exp(cs_end - cs)
    e_off = jnp.exp(cs)
    c_dec = jnp.broadcast_to(jnp.exp(cs_end), (8, LANES))
    stack = jnp.concatenate([w_state, e_off, c_dec], axis=0).astype(BF16)
    expd = _dot(stack, e_mat)
    ws_x = expd[0:n]
    eo_x = expd[n:2 * n]
    cd_x = expd[2 * n:2 * n + 1]

    cs_t = cs.T
    dt_t = dt_all.T
    bm_t = bm.T.astype(BF16)

    pieces = []
    for g in range(SSD_GROUPS):
        gsl = slice(SSD_STATE * g, SSD_STATE * (g + 1))
        cg = cm16[:, gsl]
        gmat = _dot_nt(cg, bm16[:, gsl])
        for r in range(SSD_HEADS_PER_GROUP):
            hh = g * SSD_HEADS_PER_GROUP + r
            c = direction * SSD_HEADS + hh
            seg = cs[:, c:c + 1] - cs_t[c:c + 1, :]
            lm = jnp.where(mask, jnp.exp(jnp.where(mask, seg, 0.0)), 0.0) * dt_t[c:c + 1, :]
            mh = (gmat * lm).astype(BF16)
            pieces.append(_dot(mh, xs16[:, SSD_HEAD_DIM * hh:SSD_HEAD_DIM * (hh + 1)]))
    y = jnp.concatenate(pieces, axis=1)

    w = SSD_HEADS_PER_GROUP * SSD_HEAD_DIM
    offs = []
    for g in range(SSD_GROUPS):
        lsl = slice(w * g, w * (g + 1))
        gsl = slice(SSD_STATE * g, SSD_STATE * (g + 1))
        h_g = h_ref[:, lsl]
        offs.append(_dot(cm16[:, gsl], h_g.astype(BF16)) * eo_x[:, lsl])
        xd = (xs[:, lsl] * ws_x[:, lsl]).astype(BF16)
        h_ref[:, lsl] = h_g * cd_x[:, lsl] + _dot(bm_t[gsl, :], xd)
    y = y + jnp.concatenate(offs, axis=1)
    if skip is not None:
        y = y + xs * skip
    return y


def _ssd_kernel(xf_ref, dtf_ref, xb_ref, dtb_ref, dtbias_ref, alog_ref, skip_ref, e_ref,
                yf_ref, yb_ref, hf_ref, hb_ref):
    i = pl.program_id(1)

    @pl.when(i == 0)
    def _():
        hf_ref[...] = jnp.zeros_like(hf_ref)
        hb_ref[...] = jnp.zeros_like(hb_ref)

    lane = lax.broadcasted_iota(I32, (1, LANES), 1)
    a_all = jnp.where(lane < SSD_DIRECTIONS * SSD_HEADS, -jnp.exp(alog_ref[...]), 0.0)
    dtbias = dtbias_ref[...]
    skip = skip_ref[...]
    nch = xf_ref.shape[0] // SSD_CHUNK

    for c in range(nch):
        rows = slice(SSD_CHUNK * c, SSD_CHUNK * (c + 1))
        y = _ssd_chunk(xf_ref[rows, :], dtf_ref[rows, :], 0, hf_ref, dtbias, a_all, e_ref[0], skip)
        yf_ref[rows, :] = y.astype(yf_ref.dtype)

    for c in reversed(range(nch)):
        rows = slice(SSD_CHUNK * c, SSD_CHUNK * (c + 1))
        y = _ssd_chunk(xb_ref[rows, :], dtb_ref[rows, :], 1, hb_ref, dtbias, a_all, e_ref[1], None)
        yb_ref[rows, :] = y.astype(yb_ref.dtype)


def _ssd(act, dt_raw, dtbias, alog, skip, e_mat, b, s):
    r = min(SSD_BLOCK, s)
    nb = s // r

    def cur(rev):
        return (lambda bi, i: (bi * nb + (nb - 1 - i), 0)) if rev else (lambda bi, i: (bi * nb + i, 0))

    def role(rev):
        return [pl.BlockSpec((r, SSD_XBC), cur(rev)), pl.BlockSpec((r, LANES), cur(rev))]

    consts = (dtbias, alog, skip, e_mat)
    return pl.pallas_call(
        _ssd_kernel,
        grid=(b, nb),
        in_specs=role(False) + role(True) + [_full(c.shape) for c in consts],
        out_specs=[pl.BlockSpec((r, SSD_INNER), cur(False)), pl.BlockSpec((r, SSD_INNER), cur(True))],
        out_shape=[jax.ShapeDtypeStruct((b * s, SSD_INNER), F32)] * 2,
        scratch_shapes=[pltpu.VMEM((SSD_STATE, SSD_INNER), F32),
                        pltpu.VMEM((SSD_STATE, SSD_INNER), F32)],
        compiler_params=pltpu.CompilerParams(dimension_semantics=("parallel", "arbitrary"),
                                             vmem_limit_bytes=VMEM_LIMIT),
        name="ssd",
    )(act, dt_raw, act, dt_raw, *consts)


def _mixout_kernel(x_ref, attn_ref, yf_ref, yb_ref, z_ref, anw_ref, snw_ref, wo_ref, fnw_ref,
                   wr_ref, br_ref, x1_out, route_out):
    attn = _rms(attn_ref[...].astype(F32), anw_ref[...])
    z = z_ref[...].astype(F32)
    y = (yf_ref[...] + yb_ref[...]) * (z * _sigmoid(z))
    y = _rms(y, snw_ref[...])
    mix = jnp.concatenate([attn, y], axis=1).astype(BF16)
    x1 = x_ref[...] + _dot(mix, wo_ref[...])
    _store_token_tiles(x1_out, x1)

    h = _rms(x1, fnw_ref[...])
    h1, h2, _ = _split3(h)
    two = _dot_nt(wr_ref[...], h1) + _dot_nt(wr_ref[...], h2)
    lt = two[:ROUTE_ROWS] + two[ROUTE_ROWS:] + br_ref[...]

    epg = EXPERTS_PER_GROUP
    sub = lax.broadcasted_iota(I32, (epg, lt.shape[1]), 0)
    ninf = -jnp.inf

    def argmax_first(vals):
        vmax = jnp.max(vals, axis=0, keepdims=True)
        return jnp.min(jnp.where(vals == vmax, sub, epg), axis=0, keepdims=True)

    gidx = argmax_first(jnp.where(sub < N_EXPERT_GROUPS, lt[:epg], ninf))
    sel = lt[epg:2 * epg]
    for g in range(1, N_EXPERT_GROUPS):
        sel = jnp.where(gidx == g, lt[epg * (g + 1):epg * (g + 2)], sel)
    i1 = argmax_first(sel)
    i2 = argmax_first(jnp.where(sub == i1, ninf, sel))
    elo = jnp.minimum(i1, i2)
    ehi = jnp.maximum(i1, i2)
    pair = (elo * (2 * epg - 1 - elo)) // 2 + (ehi - elo - 1)
    cls = gidx * PAIRS_PER_GROUP + pair
    route_out[...] = jnp.broadcast_to(cls.astype(F32), route_out.shape)


def _mixout(x2, attn, yf, yb, z, anw, snw, wo, fnw, wr, br):
    t = x2.shape[0]
    tm = min(TOKEN_TILE, t)
    row = lambda w: pl.BlockSpec((tm, w), lambda i: (i, 0))
    br = jnp.broadcast_to(br, (ROUTE_ROWS, tm))
    weights = (anw, snw, wo, fnw, wr, br)
    return pl.pallas_call(
        _mixout_kernel,
        grid=(t // tm,),
        in_specs=[row(D_MODEL), row(MLA_OUT), row(SSD_INNER), row(SSD_INNER), row(SSD_INNER)]
        + [_full(w.shape) for w in weights],
        out_specs=[_token_tile_spec(tm), pl.BlockSpec((8, tm), lambda i: (i, 0))],
        out_shape=[jax.ShapeDtypeStruct((t * ROW_TILES, LANES), F32),
                   jax.ShapeDtypeStruct((t // tm * 8, tm), F32)],
        compiler_params=pltpu.CompilerParams(dimension_semantics=("parallel",),
                                             vmem_limit_bytes=VMEM_LIMIT),
        name="mixout",
    )(x2, attn, yf, yb, z, *weights)


def _moe_kernel(tile_lo_ref, tile_hi_ref, tile_j_ref, tile_n_ref, order_ref,
                x1_hbm, fnw_ref, wr_ref, br_ref, *rest):
    weight_refs = rest[:4]
    y_hbm, xbuf, obuf, sem_in, sem_out = rest[4:]
    tile = xbuf.shape[1] // ROW_TILES
    n_tok = y_hbm.shape[0] - 2 * tile

    def token_rows(r):
        return pl.ds(pl.multiple_of(r * ROW_TILES, ROW_TILES), ROW_TILES)

    def gather_start(tt, sl, inline):
        j0 = tile_j_ref[tt]

        def one(r, priority=0):
            tok = order_ref[j0 + r]
            pltpu.make_async_copy(x1_hbm.at[tok], xbuf.at[sl, token_rows(r)],
                                  sem_in.at[sl]).start(priority=priority)

        if inline:
            for r in range(tile):
                one(r, 1)
        else:
            def body(r, c):
                one(r)
                return c
            lax.fori_loop(0, tile, body, 0, unroll=8)

    def gather_wait(sl):
        pltpu.make_async_copy(xbuf.at[sl], xbuf.at[sl], sem_in.at[sl]).wait()

    def scatter_start(j0, n, sl, inline):
        def one(r, priority=0):
            tok = jnp.where(r < n, order_ref[j0 + r], n_tok + sl * tile + r)
            pltpu.make_async_copy(obuf.at[sl, token_rows(r)], y_hbm.at[tok],
                                  sem_out.at[sl]).start(priority=priority)

        if inline:
            for r in range(tile):
                one(r, r % 2)
        else:
            def body(r, c):
                one(r)
                return c
            lax.fori_loop(0, tile, body, 0, unroll=8)

    def scatter_wait(sl):
        pltpu.make_async_copy(obuf.at[sl], obuf.at[sl], sem_out.at[sl]).wait()

    def one_tile(t, slot, wgu_lo_ref, wdn_lo_ref, wgu_hi_ref, wdn_hi_ref):
        prev = jnp.maximum(t - 1, 0)
        valid = tile_n_ref[t] > 0
        prev_valid = (t > 0) & (tile_n_ref[prev] > 0)

        @pl.when(t == 0)
        def _():
            obuf[...] = jnp.zeros_like(obuf)
            scatter_start(0, 0, 0, inline=False)
            scatter_wait(0)

            @pl.when(valid)
            def _():
                gather_start(t, slot, inline=False)

        @pl.when(jnp.logical_not(valid) & prev_valid)
        def _():
            gather_wait(slot)
            scatter_wait(slot)
            scatter_start(tile_j_ref[prev], tile_n_ref[prev], 1 - slot, inline=False)
            scatter_wait(1 - slot)

        @pl.when(valid)
        def _():
            gather_wait(slot)
            gather_start(t + 1, 1 - slot, inline=True)
            scatter_start(tile_j_ref[prev], jnp.where(t > 0, tile_n_ref[prev], 0), 1 - slot,
                          inline=True)

            h = _rms(_load_token_tiles(xbuf.at[slot]), fnw_ref[...]).astype(BF16)
            logits = _dot(h, wr_ref[...]) + br_ref[...]
            lane = lax.broadcasted_iota(I32, logits.shape, 1)
            gl = jnp.where(lane < N_EXPERT_GROUPS, logits, -jnp.inf)
            gweight = 1.0 / jnp.sum(jnp.exp(gl - jnp.max(gl, axis=-1, keepdims=True)),
                                    axis=-1, keepdims=True)
            pick = lambda e: jnp.sum(jnp.where(lane == N_EXPERT_GROUPS + e, logits, 0.0),
                                     axis=-1, keepdims=True)
            l_lo = pick(tile_lo_ref[t])
            l_hi = pick(tile_hi_ref[t])
            wts = (gweight / (1.0 + jnp.exp(l_hi - l_lo)), gweight / (1.0 + jnp.exp(l_lo - l_hi)))
            acc = None
            for half, wgu_ref, wdn_ref in ((0, wgu_lo_ref, wdn_lo_ref), (1, wgu_hi_ref, wdn_hi_ref)):
                gu = _dot(h, wgu_ref[...])
                g = gu[:, :D_EXPERT]
                he = (g * _sigmoid(g) * gu[:, D_EXPERT:] * wts[half]).astype(BF16)
                d = _dot(he, wdn_ref[...])
                acc = d if acc is None else acc + d

            @pl.when(t > 0)
            def _():
                scatter_wait(slot)
            _store_token_tiles(obuf.at[slot], acc)

    one_tile(pl.program_id(0), pl.program_id(0) % 2, *weight_refs)


def _moe(x1t, order, tile_lo, tile_hi, tile_j, tile_n, fnw, wr, br, wgu, wdn):
    t = x1t.shape[0]
    n_tiles = tile_lo.shape[0]
    tile = MOE_TILE
    wspec = lambda shape, which: pl.BlockSpec(
        (None,) + shape, lambda i, lo, hi, tj, tn, od: ((lo, hi)[which][i], 0, 0))
    grid_spec = pltpu.PrefetchScalarGridSpec(
        num_scalar_prefetch=5,
        grid=(n_tiles,),
        in_specs=[pl.BlockSpec(memory_space=pl.ANY),
                  pl.BlockSpec((1, D_MODEL), lambda i, *_: (0, 0)),
                  pl.BlockSpec((D_MODEL, LANES), lambda i, *_: (0, 0)),
                  pl.BlockSpec((1, LANES), lambda i, *_: (0, 0)),
                  wspec((D_MODEL, 2 * D_EXPERT), 0), wspec((D_EXPERT, D_MODEL), 0),
                  wspec((D_MODEL, 2 * D_EXPERT), 1), wspec((D_EXPERT, D_MODEL), 1)],
        out_specs=pl.BlockSpec(memory_space=pl.ANY),
        scratch_shapes=[pltpu.VMEM((2, tile * ROW_TILES, LANES), F32),
                        pltpu.VMEM((2, tile * ROW_TILES, LANES), F32),
                        pltpu.SemaphoreType.DMA((2,)), pltpu.SemaphoreType.DMA((2,))],
    )
    return pl.pallas_call(
        _moe_kernel,
        grid_spec=grid_spec,
        out_shape=jax.ShapeDtypeStruct((t + 2 * tile, ROW_TILES, LANES), F32),
        compiler_params=pltpu.CompilerParams(dimension_semantics=("arbitrary",),
                                             vmem_limit_bytes=VMEM_LIMIT),
        name="moe",
    )(tile_lo, tile_hi, tile_j, tile_n, order, x1t, fnw, wr, br, wgu, wdn, wgu, wdn)


def _moe_plan(route, t):
    tile = MOE_TILE
    n_tiles = t // tile + N_CLASSES + 1
    cls = route.reshape(-1, 8, route.shape[1])[:, 0, :].reshape(t).astype(I32)
    _, order = lax.sort((cls, jnp.arange(t, dtype=I32)), num_keys=1, is_stable=True)
    order = jnp.concatenate([order, jnp.zeros((tile,), I32)])
    class_ids = jnp.arange(N_CLASSES, dtype=I32)
    counts = jnp.sum((cls[:, None] == class_ids[None, :]).astype(I32), axis=0)
    tiles_per = (counts + tile - 1) // tile
    tile_end = jnp.cumsum(tiles_per)
    tile_begin = tile_end - tiles_per
    starts = jnp.cumsum(counts) - counts
    t_idx = jnp.arange(n_tiles, dtype=I32)
    tile_cls = jnp.sum((tile_end[None, :] <= t_idx[:, None]).astype(I32), axis=1)
    valid = t_idx < tile_end[-1]
    last_cls = jnp.max(jnp.where(counts > 0, class_ids, 0))
    tile_cls = jnp.where(valid, tile_cls, last_cls)
    onehot = (tile_cls[:, None] == class_ids[None, :]).astype(I32)
    pick = lambda v: jnp.sum(onehot * v[None, :], axis=1)
    k = t_idx - pick(tile_begin)
    tile_j = jnp.where(valid, pick(starts) + k * tile, 0)
    tile_n = jnp.where(valid, jnp.clip(pick(counts) - k * tile, 0, tile), 0)
    lo_of_pair, hi_of_pair = [], []
    for lo in range(EXPERTS_PER_GROUP):
        for hi in range(lo + 1, EXPERTS_PER_GROUP):
            lo_of_pair.append(lo)
            hi_of_pair.append(hi)
    grp = class_ids // PAIRS_PER_GROUP
    class_lo = grp * EXPERTS_PER_GROUP + jnp.asarray(lo_of_pair * N_EXPERT_GROUPS, I32)
    class_hi = grp * EXPERTS_PER_GROUP + jnp.asarray(hi_of_pair * N_EXPERT_GROUPS, I32)
    return order, pick(class_lo), pick(class_hi), tile_j.astype(I32), tile_n.astype(I32)


def _final_kernel(x1_ref, ym_ref, p_ref, pnw_ref, wg_ref, bg_ref, wp_ref, ppnw_ref, fnw_ref, o_ref):
    x2 = _load_token_tiles(x1_ref) + _load_token_tiles(ym_ref)
    gate = _sigmoid(_dot(_rms(x2, pnw_ref[...]).astype(BF16), wg_ref[...]) + bg_ref[...])
    ple = _rms(_dot(p_ref[...].astype(BF16), wp_ref[...]), ppnw_ref[...])
    o_ref[...] = _rms(x2 + gate * ple, fnw_ref[...])


def _final(x1, ym, p2, pnw, wg, bg, wp, ppnw, fnw):
    t = p2.shape[0]
    tm = min(TOKEN_TILE, t)
    row = lambda w: pl.BlockSpec((tm, w), lambda i: (i, 0))
    weights = (pnw, wg, bg, wp, ppnw, fnw)
    return pl.pallas_call(
        _final_kernel,
        grid=(t // tm,),
        in_specs=[_token_tile_spec(tm), _token_tile_spec(tm), row(PLE_DIM)]
        + [_full(w.shape) for w in weights],
        out_specs=row(D_MODEL),
        out_shape=jax.ShapeDtypeStruct((t, D_MODEL), F32),
        compiler_params=pltpu.CompilerParams(dimension_semantics=("parallel",),
                                             vmem_limit_bytes=VMEM_LIMIT),
        name="final",
    )(x1, ym, p2, *weights)


def _pad_cols(w, width, offset=0):
    out = jnp.zeros((w.shape[0], width), w.dtype)
    return out.at[:, offset:offset + w.shape[1]].set(w)


def _rope_swap(w):
    half = MLA_ROPE_DIM // 2
    return jnp.concatenate([-w[..., half:], w[..., :half]], axis=-1)


def _layer(x2, p2, cos_t, sin_t, b, s, attn_norm_w, w_in, q_norm_w, w_uq, kv_norm_w, w_ukv,
           attn_out_norm_w, conv_w, conv_b, dt_bias, a_log, ssd_d, ssd_norm_w, w_o, ffn_norm_w,
           w_router_group, b_router_group, w_router_expert, b_router_expert, w_exp_gate,
           w_exp_up, w_exp_down, ple_norm_w, w_ple_gate, b_ple_gate, w_ple_proj, ple_post_norm_w):
    t = x2.shape[0]
    r1 = lambda v: v.reshape(1, -1).astype(F32)
    o1 = MLA_Q_RANK
    o2 = o1 + MLA_KV_RANK
    o3 = o2 + MLA_ROPE_DIM
    o4 = o3 + SSD_INNER
    o5 = o4 + SSD_XBC
    w_kr = w_in[:, o2:o3]
    wa = jnp.concatenate([w_in[:, :o2], _pad_cols(w_kr, HEAD_PAD, MLA_NOPE_DIM),
                          _pad_cols(_rope_swap(w_kr), HEAD_PAD, MLA_NOPE_DIM),
                          _pad_cols(w_in[:, o5:], LANES)], axis=1).astype(BF16)
    wz = w_in[:, o3:o4].astype(BF16)
    wxbc = w_in[:, o4:o5].astype(BF16)
    uq = w_uq.reshape(MLA_Q_RANK, MLA_HEADS, MLA_NOPE_DIM + MLA_ROPE_DIM)
    zq = jnp.zeros((MLA_Q_RANK, MLA_HEADS, HEAD_PAD - MLA_NOPE_DIM - MLA_ROPE_DIM), F32)
    wuq = jnp.concatenate([uq, zq], axis=-1).reshape(MLA_Q_RANK, -1).astype(BF16)
    wuqs = jnp.concatenate([jnp.zeros_like(uq[..., :MLA_NOPE_DIM]), _rope_swap(uq[..., MLA_NOPE_DIM:]),
                            zq], axis=-1).reshape(MLA_Q_RANK, -1).astype(BF16)
    ukv = w_ukv.reshape(MLA_KV_RANK, MLA_HEADS, MLA_NOPE_DIM + MLA_V_DIM)
    zk = jnp.zeros((MLA_KV_RANK, MLA_HEADS, HEAD_PAD - MLA_NOPE_DIM), F32)
    wuk = jnp.concatenate([ukv[..., :MLA_NOPE_DIM], zk], axis=-1).reshape(MLA_KV_RANK, -1).astype(BF16)
    wuv = ukv[..., MLA_NOPE_DIM:].reshape(MLA_KV_RANK, -1).T.astype(BF16)

    q, k, v, z, act, dt_raw = _inproj(x2, cos_t, sin_t, s, r1(attn_norm_w), wa, wz, wxbc,
                                      conv_w.astype(F32), r1(conv_b), r1(q_norm_w), wuq, wuqs,
                                      r1(kv_norm_w), wuk, wuv)
    attn = _attention(q, k, v, b, s)

    head_of_lane = jnp.arange(SSD_INNER) // SSD_HEAD_DIM
    rows = jnp.arange(LANES)[:, None]
    e_mat = jnp.stack([(rows == d * SSD_HEADS + head_of_lane[None, :]) for d in range(SSD_DIRECTIONS)]
                      ).astype(BF16)
    skip = jnp.repeat(ssd_d.astype(F32), SSD_HEAD_DIM).reshape(1, -1)
    yf, yb = _ssd(act, dt_raw, _pad_cols(r1(dt_bias), LANES), _pad_cols(r1(a_log), LANES), skip,
                  e_mat, b, s)

    wr = _pad_cols(jnp.concatenate([w_router_group, w_router_expert], axis=1).astype(F32), LANES)
    wr_hi = wr.astype(BF16)
    br = _pad_cols(jnp.concatenate([r1(b_router_group), r1(b_router_expert)], axis=1), LANES)
    gpad = jnp.zeros((EXPERTS_PER_GROUP - N_EXPERT_GROUPS, D_MODEL), F32)
    wrt = jnp.concatenate([w_router_group.T.astype(F32), gpad, w_router_expert.T.astype(F32),
                           jnp.zeros((ROUTE_ROWS - EXPERTS_PER_GROUP - N_EXPERTS, D_MODEL), F32)])
    wrt_hi, wrt_lo, _ = _split3(wrt)
    brt = jnp.concatenate([b_router_group.astype(F32), gpad[:, 0], b_router_expert.astype(F32),
                           jnp.zeros((ROUTE_ROWS - EXPERTS_PER_GROUP - N_EXPERTS,), F32)])
    x1, route = _mixout(x2, attn, yf, yb, z, r1(attn_out_norm_w), r1(ssd_norm_w), w_o.astype(BF16),
                        r1(ffn_norm_w), jnp.concatenate([wrt_hi, wrt_lo]), brt.reshape(-1, 1))

    order, tile_lo, tile_hi, tile_j, tile_n = _moe_plan(route, t)
    wgu = jnp.concatenate([w_exp_gate, w_exp_up], axis=-1).astype(BF16)
    ym = _moe(x1.reshape(t, ROW_TILES, LANES), order, tile_lo, tile_hi, tile_j, tile_n,
              r1(ffn_norm_w), wr_hi, br, wgu, w_exp_down.astype(BF16))
    ym = ym.reshape(ym.shape[0] * ROW_TILES, LANES)
    return x1, ym, (r1(ple_norm_w), w_ple_gate.astype(BF16), r1(b_ple_gate), w_ple_proj.astype(BF16),
                    r1(ple_post_norm_w))


def kernel(x, p, positions, attn_norm_w, w_in, q_norm_w, w_uq, kv_norm_w, w_ukv, attn_out_norm_w, conv_w, conv_b, dt_bias, a_log, ssd_d, ssd_norm_w, w_o, ffn_norm_w, w_router_group, b_router_group, w_router_expert, b_router_expert, w_exp_gate, w_exp_up, w_exp_down, ple_norm_w, w_ple_gate, b_ple_gate, w_ple_proj, ple_post_norm_w, final_norm_w):
    b, s, d = x.shape
    depth = p.shape[0]
    assert depth == 1, "the fused final stage assumes a single layer"
    t = b * s
    inv_freq = 1.0 / (ROPE_THETA ** (jnp.arange(0, MLA_ROPE_DIM, 2, dtype=F32) / MLA_ROPE_DIM))
    half = MLA_ROPE_DIM // 2
    ang = (positions.astype(F32).reshape(t, 1) * inv_freq).reshape(t * half // LANES, LANES)
    cos, sin = lax.optimization_barrier((jnp.cos(ang), jnp.sin(ang)))
    cos = cos.reshape(t, half)
    sin = sin.reshape(t, half)
    ones = jnp.ones((t, MLA_NOPE_DIM), F32)
    zeros = jnp.zeros((t, HEAD_PAD - MLA_NOPE_DIM - MLA_ROPE_DIM), F32)
    cos_t = jnp.concatenate([ones, cos, cos, zeros], axis=1)
    sin_t = jnp.concatenate([0.0 * ones, sin, sin, zeros], axis=1)

    x2 = x.reshape(t, d)
    i = 0
    x1, ym, (pnw, wg, bg, wp, ppnw) = _layer(
        x2, p[i].reshape(t, -1), cos_t, sin_t, b, s, attn_norm_w[i], w_in[i], q_norm_w[i], w_uq[i],
        kv_norm_w[i], w_ukv[i], attn_out_norm_w[i], conv_w[i], conv_b[i], dt_bias[i], a_log[i],
        ssd_d[i], ssd_norm_w[i], w_o[i], ffn_norm_w[i], w_router_group[i], b_router_group[i],
        w_router_expert[i], b_router_expert[i], w_exp_gate[i], w_exp_up[i], w_exp_down[i],
        ple_norm_w[i], w_ple_gate[i], b_ple_gate[i], w_ple_proj[i], ple_post_norm_w[i])
    out = _final(x1, ym, p[i].reshape(t, -1), pnw, wg, bg, wp, ppnw, final_norm_w.reshape(1, -1).astype(F32))
    return out.reshape(b, s, d)
```

```python
import functools
import math

import jax
import jax.numpy as jnp
from jax import lax
from jax.experimental import pallas as pl
from jax.experimental.pallas import tpu as pltpu

F32 = jnp.float32
BF16 = jnp.bfloat16
I32 = jnp.int32

D_MODEL = 1024
PLE_DIM = 256
NORM_EPS = 1e-6

MLA_HEADS = 8
MLA_Q_RANK = 256
MLA_KV_RANK = 128
MLA_NOPE_DIM = 64
MLA_ROPE_DIM = 32
MLA_V_DIM = 64
MLA_OUT = MLA_HEADS * MLA_V_DIM
ROPE_THETA = 10000.0
HEAD_PAD = 128
ONES_ROWS = 16

SSD_HEADS = 8
SSD_HEAD_DIM = 64
SSD_GROUPS = 2
SSD_HEADS_PER_GROUP = SSD_HEADS // SSD_GROUPS
SSD_STATE = 64
SSD_CONV = 5
SSD_CHUNK = 128
SSD_INNER = SSD_HEADS * SSD_HEAD_DIM
SSD_XBC = SSD_INNER + 2 * SSD_GROUPS * SSD_STATE
SSD_DIRECTIONS = 2
CONV_HALO = 8

N_EXPERT_GROUPS = 4
EXPERTS_PER_GROUP = 8
N_EXPERTS = N_EXPERT_GROUPS * EXPERTS_PER_GROUP
D_EXPERT = 256
PAIRS_PER_GROUP = EXPERTS_PER_GROUP * (EXPERTS_PER_GROUP - 1) // 2
N_CLASSES = N_EXPERT_GROUPS * PAIRS_PER_GROUP
ROUTE_ROWS = 48

LANES = 128
ROW_TILES = D_MODEL // LANES
VMEM_LIMIT = 48 * 1024 * 1024

TOKEN_TILE = 1024
VT_TILE = 512
ATTN_Q_TILE = 1024
ATTN_Q_SUB = 256
ATTN_PIPELINE = {True: (512, 3), False: (256, 4)}
ATTN_LOGIT_BOUND = 80.0
SSD_BLOCK = 1024
MOE_TILE = 128


def _rms(x, w):
    ms = jnp.mean(x * x, axis=-1, keepdims=True)
    return x * lax.rsqrt(ms + NORM_EPS) * w


def _dot(a, b):
    return jnp.dot(a, b, preferred_element_type=F32)


def _dot_nt(a, b):
    return lax.dot_general(a, b, (((1,), (1,)), ((), ())), preferred_element_type=F32)


def _split3(x):
    x1 = x.astype(BF16)
    r1 = x - x1.astype(F32)
    x2 = r1.astype(BF16)
    x3 = (r1 - x2.astype(F32)).astype(BF16)
    return x1, x2, x3


def _sigmoid(x):
    return 1.0 / (1.0 + jnp.exp(-x))


def _softplus(x):
    return jnp.maximum(x, 0.0) + jnp.log(1.0 + jnp.exp(-jnp.abs(x)))


def _full(shape):
    nd = len(shape)
    return pl.BlockSpec(shape, lambda *_: (0,) * nd)


def _token_tile_spec(tm):
    return pl.BlockSpec((tm * ROW_TILES, LANES), lambda i: (i, 0))


def _store_token_tiles(ref, x):
    n = x.shape[0]
    for s in range(ROW_TILES):
        ref[pl.ds(s, n, stride=ROW_TILES), :] = x[:, LANES * s:LANES * (s + 1)]


def _load_token_tiles(ref):
    n = ref.shape[0] // ROW_TILES
    return jnp.concatenate([ref[pl.ds(s, n, stride=ROW_TILES), :] for s in range(ROW_TILES)], axis=1)


def _inproj_kernel(x_ref, xprev_ref, xnext_ref, nw_ref, wa_ref, wz_ref, wxbc_ref, cw_ref, cb_ref,
                   qnw_ref, wuq_ref, wuqs_ref, kvnw_ref, wuk_ref, wuv_ref, cos_ref, sin_ref,
                   q_out, k_out, v_out, z_out, act_out, dt_out, conv_ref, *, scale, tiles_per_seq):
    tm = x_ref.shape[0]
    h = _rms(x_ref[...], nw_ref[...]).astype(BF16)
    pa = _dot(h, wa_ref[...])
    z_out[...] = _dot(h, wz_ref[...]).astype(z_out.dtype)
    dt_out[...] = pa[:, MLA_Q_RANK + MLA_KV_RANK + 2 * HEAD_PAD:]

    pos = pl.program_id(0) % tiles_per_seq
    halo = lambda ref: _dot(_rms(ref[...], nw_ref[...]).astype(BF16), wxbc_ref[...])
    conv_ref[0:CONV_HALO, :] = jnp.where(pos > 0, halo(xprev_ref), 0.0)
    conv_ref[CONV_HALO:CONV_HALO + tm, :] = _dot(h, wxbc_ref[...])
    conv_ref[CONV_HALO + tm:2 * CONV_HALO + tm, :] = jnp.where(pos < tiles_per_seq - 1,
                                                               halo(xnext_ref), 0.0)
    acc = jnp.zeros((tm, SSD_XBC), F32) + cb_ref[...]
    base = CONV_HALO - SSD_CONV // 2
    for kk in range(SSD_CONV):
        acc = acc + cw_ref[kk:kk + 1, :] * conv_ref[base + kk:base + kk + tm, :]
    act_out[...] = (acc * _sigmoid(acc)).astype(act_out.dtype)

    cos = cos_ref[...]
    sin = sin_ref[...]
    o1 = MLA_Q_RANK
    o2 = o1 + MLA_KV_RANK
    cqn = _rms(pa[:, :o1], qnw_ref[...]).astype(BF16)
    ckvn = _rms(pa[:, o1:o2], kvnw_ref[...]).astype(BF16)
    q = _dot(cqn, wuq_ref[...])
    qs = _dot(cqn, wuqs_ref[...])
    kn = _dot(ckvn, wuk_ref[...])
    vt = _dot_nt(wuv_ref[...], ckvn).astype(v_out.dtype)
    for j in range(v_out.shape[0]):
        v_out[j] = vt[:, VT_TILE * j:VT_TILE * (j + 1)]
    kr = pa[:, o2:o2 + HEAD_PAD] * cos + pa[:, o2 + HEAD_PAD:o2 + 2 * HEAD_PAD] * sin
    for hh in range(MLA_HEADS):
        sl = slice(HEAD_PAD * hh, HEAD_PAD * (hh + 1))
        q_out[:, sl] = ((q[:, sl] * cos + qs[:, sl] * sin) * scale).astype(q_out.dtype)
        k_out[:, sl] = (kn[:, sl] + kr).astype(k_out.dtype)


def _inproj(x2, cos_t, sin_t, s, nw, wa, wz, wxbc, cw, cb, qnw, wuq, wuqs, kvnw, wuk, wuv):
    t = x2.shape[0]
    tm = min(TOKEN_TILE, s)
    hb = tm // CONV_HALO
    last = t // CONV_HALO - 1
    scale = (MLA_NOPE_DIM + MLA_ROPE_DIM) ** -0.5 * math.log2(math.e)
    row = lambda w: pl.BlockSpec((tm, w), lambda i: (i, 0))
    weights = (nw, wa, wz, wxbc, cw, cb, qnw, wuq, wuqs, kvnw, wuk, wuv)
    return pl.pallas_call(
        functools.partial(_inproj_kernel, scale=scale, tiles_per_seq=s // tm),
        grid=(t // tm,),
        in_specs=[row(D_MODEL),
                  pl.BlockSpec((CONV_HALO, D_MODEL), lambda i: (jnp.maximum(i * hb - 1, 0), 0)),
                  pl.BlockSpec((CONV_HALO, D_MODEL), lambda i: (jnp.minimum((i + 1) * hb, last), 0))]
        + [_full(w.shape) for w in weights] + [row(HEAD_PAD), row(HEAD_PAD)],
        out_specs=[row(MLA_HEADS * HEAD_PAD), row(MLA_HEADS * HEAD_PAD),
                   pl.BlockSpec((tm // VT_TILE, MLA_OUT, VT_TILE), lambda i: (i, 0, 0)),
                   row(SSD_INNER), row(SSD_XBC), row(LANES)],
        out_shape=[jax.ShapeDtypeStruct((t, MLA_HEADS * HEAD_PAD), BF16),
                   jax.ShapeDtypeStruct((t, MLA_HEADS * HEAD_PAD), BF16),
                   jax.ShapeDtypeStruct((t // VT_TILE, MLA_OUT, VT_TILE), BF16),
                   jax.ShapeDtypeStruct((t, SSD_INNER), BF16),
                   jax.ShapeDtypeStruct((t, SSD_XBC), BF16),
                   jax.ShapeDtypeStruct((t, LANES), F32)],
        scratch_shapes=[pltpu.VMEM((tm + 2 * CONV_HALO, SSD_XBC), F32)],
        compiler_params=pltpu.CompilerParams(dimension_semantics=("parallel",),
                                             vmem_limit_bytes=VMEM_LIMIT),
        name="inproj",
    )(x2, x2, x2, *weights, cos_t, sin_t)


def _attn_body(q_ref, k_ref, vt_ref, o_ref, bounded):
    tq = q_ref.shape[0]
    nc, _, tv = vt_ref.shape
    tk, ahead = ATTN_PIPELINE[bounded]
    ts = min(ATTN_Q_SUB, tq)
    ones = jnp.ones((ONES_ROWS, tk), BF16)
    streams = []
    for j in range(tq // ts):
        for a in range(2):
            hsl = slice(HEAD_PAD * a, HEAD_PAD * (a + 1))
            vsl = slice(MLA_V_DIM * a, MLA_V_DIM * (a + 1))
            streams.append([hsl, vsl, q_ref[j * ts:(j + 1) * ts, hsl],
                            jnp.full((1, ts), -1e30, F32),
                            jnp.zeros((MLA_V_DIM + ONES_ROWS, ts), F32)])
    units = [(c, i) for c in range(nc * tv // tk) for i in range(len(streams))]
    scores = {}

    def issue(u):
        c, i = units[u]
        scores[u] = _dot_nt(k_ref[c * tk:(c + 1) * tk, streams[i][0]], streams[i][2])

    for u in range(min(ahead, len(units))):
        issue(u)
    for u, (c, i) in enumerate(units):
        if u + ahead < len(units):
            issue(u + ahead)
        st = streams[i]
        _, vsl, _, m, acc = st
        s = scores.pop(u)
        blk, off = divmod(c * tk, tv)
        vt = jnp.concatenate([vt_ref[blk, vsl, off:off + tk], ones], axis=0)
        if bounded:
            st[4] = acc + _dot(vt, jnp.exp2(s).astype(BF16))
        else:
            m_new = jnp.maximum(m, jnp.max(s, axis=0, keepdims=True))
            p = jnp.exp2((s - m_new).astype(BF16))
            st[3] = m_new
            st[4] = acc * jnp.exp2(m - m_new) + _dot(vt, p)
    for j in range(tq // ts):
        halves = [st[4][:MLA_V_DIM] / st[4][MLA_V_DIM:MLA_V_DIM + 1] for st in streams[2 * j:2 * j + 2]]
        o_ref[j * ts:(j + 1) * ts, :] = jnp.concatenate(halves, axis=0).T.astype(o_ref.dtype)


def _attn_kernel(q_ref, k_ref, vt_ref, o_ref, kmax_ref):
    @pl.when(pl.program_id(2) == 0)
    def _():
        for a in range(2):
            kmax_ref[a] = jnp.max(jnp.abs(k_ref[:, HEAD_PAD * a:HEAD_PAD * (a + 1)].astype(F32)))

    bounded = None
    for a in range(2):
        q1 = jnp.sum(jnp.abs(q_ref[:, HEAD_PAD * a:HEAD_PAD * (a + 1)].astype(F32)), axis=1, keepdims=True)
        inside = jnp.max(q1) * kmax_ref[a] <= ATTN_LOGIT_BOUND
        bounded = inside if bounded is None else bounded & inside

    @pl.when(bounded)
    def _():
        _attn_body(q_ref, k_ref, vt_ref, o_ref, True)

    @pl.when(jnp.logical_not(bounded))
    def _():
        _attn_body(q_ref, k_ref, vt_ref, o_ref, False)


def _attention(q, k, vt, b, s):
    tq = min(ATTN_Q_TILE, s)
    nq = s // tq
    pairs = MLA_HEADS // 2
    tk = vt.shape[2]
    nc = s // tk
    return pl.pallas_call(
        _attn_kernel,
        grid=(b, pairs, nq),
        in_specs=[pl.BlockSpec((tq, 2 * HEAD_PAD), lambda bi, pi, qi: (bi * nq + qi, pi)),
                  pl.BlockSpec((s, 2 * HEAD_PAD), lambda bi, pi, qi: (bi, pi)),
                  pl.BlockSpec((nc, 2 * MLA_V_DIM, tk), lambda bi, pi, qi: (bi, pi, 0))],
        out_specs=pl.BlockSpec((tq, 2 * MLA_V_DIM), lambda bi, pi, qi: (bi * nq + qi, pi)),
        out_shape=jax.ShapeDtypeStruct((b * s, MLA_OUT), BF16),
        scratch_shapes=[pltpu.SMEM((2,), F32)],
        compiler_params=pltpu.CompilerParams(
            dimension_semantics=("parallel", "parallel", "arbitrary"),
            vmem_limit_bytes=VMEM_LIMIT),
        name="attention",
    )(q, k, vt)


def _ssd_chunk(act, dt_raw, direction, h_ref, dtbias, a_all, e_mat, skip):
    n = SSD_CHUNK
    gs = SSD_GROUPS * SSD_STATE
    xs16 = act[:, :SSD_INNER]
    bm16 = act[:, SSD_INNER:SSD_INNER + gs]
    cm16 = act[:, SSD_INNER + gs:SSD_INNER + 2 * gs]
    xs = xs16.astype(F32)
    bm = bm16.astype(F32)

    dt_all = _softplus(dt_raw + dtbias)
    a_mat = dt_all * a_all
    ri = lax.broadcasted_iota(I32, (n, n), 0)
    ci = lax.broadcasted_iota(I32, (n, n), 1)
    mask = (ci <= ri) if direction == 0 else (ci >= ri)
    tri = jnp.where(mask, 1.0, 0.0).astype(BF16)
    a1, a2, a3 = _split3(a_mat)
    cs = _dot(tri, a1) + _dot(tri, a2) + _dot(tri, a3)
    end = n - 1 if direction == 0 else 0
    cs_end = cs[end:end + 1, :]
    w_state = dt_all * jnp.exp(cs_end - cs)
    e_off = jnp.exp(cs)
    c_dec = jnp.broadcast_to(jnp.exp(cs_end), (8, LANES))
    stack = jnp.concatenate([w_state, e_off, c_dec], axis=0).astype(BF16)
    expd = _dot(stack, e_mat)
    ws_x = expd[0:n]
    eo_x = expd[n:2 * n]
    cd_x = expd[2 * n:2 * n + 1]

    cs_t = cs.T
    dt_t = dt_all.T
    bm_t = bm.T.astype(BF16)

    pieces = []
    for g in range(SSD_GROUPS):
        gsl = slice(SSD_STATE * g, SSD_STATE * (g + 1))
        cg = cm16[:, gsl]
        gmat = _dot_nt(cg, bm16[:, gsl])
        for r in range(SSD_HEADS_PER_GROUP):
            hh = g * SSD_HEADS_PER_GROUP + r
            c = direction * SSD_HEADS + hh
            seg = cs[:, c:c + 1] - cs_t[c:c + 1, :]
            lm = jnp.where(mask, jnp.exp(jnp.where(mask, seg, 0.0)), 0.0) * dt_t[c:c + 1, :]
            mh = (gmat * lm).astype(BF16)
            pieces.append(_dot(mh, xs16[:, SSD_HEAD_DIM * hh:SSD_HEAD_DIM * (hh + 1)]))
    y = jnp.concatenate(pieces, axis=1)

    w = SSD_HEADS_PER_GROUP * SSD_HEAD_DIM
    offs = []
    for g in range(SSD_GROUPS):
        lsl = slice(w * g, w * (g + 1))
        gsl = slice(SSD_STATE * g, SSD_STATE * (g + 1))
        h_g = h_ref[:, lsl]
        offs.append(_dot(cm16[:, gsl], h_g.astype(BF16)) * eo_x[:, lsl])
        xd = (xs[:, lsl] * ws_x[:, lsl]).astype(BF16)
        h_ref[:, lsl] = h_g * cd_x[:, lsl] + _dot(bm_t[gsl, :], xd)
    y = y + jnp.concatenate(offs, axis=1)
    if skip is not None:
        y = y + xs * skip
    return y


def _ssd_kernel(xf_ref, dtf_ref, xb_ref, dtb_ref, dtbias_ref, alog_ref, skip_ref, e_ref,
                yf_ref, yb_ref, hf_ref, hb_ref):
    i = pl.program_id(1)

    @pl.when(i == 0)
    def _():
        hf_ref[...] = jnp.zeros_like(hf_ref)
        hb_ref[...] = jnp.zeros_like(hb_ref)

    lane = lax.broadcasted_iota(I32, (1, LANES), 1)
    a_all = jnp.where(lane < SSD_DIRECTIONS * SSD_HEADS, -jnp.exp(alog_ref[...]), 0.0)
    dtbias = dtbias_ref[...]
    skip = skip_ref[...]
    nch = xf_ref.shape[0] // SSD_CHUNK

    for c in range(nch):
        rows = slice(SSD_CHUNK * c, SSD_CHUNK * (c + 1))
        y = _ssd_chunk(xf_ref[rows, :], dtf_ref[rows, :], 0, hf_ref, dtbias, a_all, e_ref[0], skip)
        yf_ref[rows, :] = y.astype(yf_ref.dtype)

    for c in reversed(range(nch)):
        rows = slice(SSD_CHUNK * c, SSD_CHUNK * (c + 1))
        y = _ssd_chunk(xb_ref[rows, :], dtb_ref[rows, :], 1, hb_ref, dtbias, a_all, e_ref[1], None)
        yb_ref[rows, :] = y.astype(yb_ref.dtype)


def _ssd(act, dt_raw, dtbias, alog, skip, e_mat, b, s):
    r = min(SSD_BLOCK, s)
    nb = s // r

    def cur(rev):
        return (lambda bi, i: (bi * nb + (nb - 1 - i), 0)) if rev else (lambda bi, i: (bi * nb + i, 0))

    def role(rev):
        return [pl.BlockSpec((r, SSD_XBC), cur(rev)), pl.BlockSpec((r, LANES), cur(rev))]

    consts = (dtbias, alog, skip, e_mat)
    return pl.pallas_call(
        _ssd_kernel,
        grid=(b, nb),
        in_specs=role(False) + role(True) + [_full(c.shape) for c in consts],
        out_specs=[pl.BlockSpec((r, SSD_INNER), cur(False)), pl.BlockSpec((r, SSD_INNER), cur(True))],
        out_shape=[jax.ShapeDtypeStruct((b * s, SSD_INNER), F32)] * 2,
        scratch_shapes=[pltpu.VMEM((SSD_STATE, SSD_INNER), F32),
                        pltpu.VMEM((SSD_STATE, SSD_INNER), F32)],
        compiler_params=pltpu.CompilerParams(dimension_semantics=("parallel", "arbitrary"),
                                             vmem_limit_bytes=VMEM_LIMIT),
        name="ssd",
    )(act, dt_raw, act, dt_raw, *consts)


def _mixout_kernel(x_ref, attn_ref, yf_ref, yb_ref, z_ref, anw_ref, snw_ref, wo_ref, fnw_ref,
                   wr_ref, br_ref, x1_out, route_out):
    attn = _rms(attn_ref[...].astype(F32), anw_ref[...])
    z = z_ref[...].astype(F32)
    y = (yf_ref[...] + yb_ref[...]) * (z * _sigmoid(z))
    y = _rms(y, snw_ref[...])
    mix = jnp.concatenate([attn, y], axis=1).astype(BF16)
    x1 = x_ref[...] + _dot(mix, wo_ref[...])
    _store_token_tiles(x1_out, x1)

    h = _rms(x1, fnw_ref[...])
    h1, h2, _ = _split3(h)
    two = _dot_nt(wr_ref[...], h1) + _dot_nt(wr_ref[...], h2)
    lt = two[:ROUTE_ROWS] + two[ROUTE_ROWS:] + br_ref[...]

    epg = EXPERTS_PER_GROUP
    sub = lax.broadcasted_iota(I32, (epg, lt.shape[1]), 0)
    ninf = -jnp.inf

    def argmax_first(vals):
        vmax = jnp.max(vals, axis=0, keepdims=True)
        return jnp.min(jnp.where(vals == vmax, sub, epg), axis=0, keepdims=True)

    gidx = argmax_first(jnp.where(sub < N_EXPERT_GROUPS, lt[:epg], ninf))
    sel = lt[epg:2 * epg]
    for g in range(1, N_EXPERT_GROUPS):
        sel = jnp.where(gidx == g, lt[epg * (g + 1):epg * (g + 2)], sel)
    i1 = argmax_first(sel)
    i2 = argmax_first(jnp.where(sub == i1, ninf, sel))
    elo = jnp.minimum(i1, i2)
    ehi = jnp.maximum(i1, i2)
    pair = (elo * (2 * epg - 1 - elo)) // 2 + (ehi - elo - 1)
    cls = gidx * PAIRS_PER_GROUP + pair
    route_out[...] = jnp.broadcast_to(cls.astype(F32), route_out.shape)


def _mixout(x2, attn, yf, yb, z, anw, snw, wo, fnw, wr, br):
    t = x2.shape[0]
    tm = min(TOKEN_TILE, t)
    row = lambda w: pl.BlockSpec((tm, w), lambda i: (i, 0))
    br = jnp.broadcast_to(br, (ROUTE_ROWS, tm))
    weights = (anw, snw, wo, fnw, wr, br)
    return pl.pallas_call(
        _mixout_kernel,
        grid=(t // tm,),
        in_specs=[row(D_MODEL), row(MLA_OUT), row(SSD_INNER), row(SSD_INNER), row(SSD_INNER)]
        + [_full(w.shape) for w in weights],
        out_specs=[_token_tile_spec(tm), pl.BlockSpec((8, tm), lambda i: (i, 0))],
        out_shape=[jax.ShapeDtypeStruct((t * ROW_TILES, LANES), F32),
                   jax.ShapeDtypeStruct((t // tm * 8, tm), F32)],
        compiler_params=pltpu.CompilerParams(dimension_semantics=("parallel",),
                                             vmem_limit_bytes=VMEM_LIMIT),
        name="mixout",
    )(x2, attn, yf, yb, z, *weights)


def _moe_kernel(tile_lo_ref, tile_hi_ref, tile_j_ref, tile_n_ref, order_ref,
                x1_hbm, fnw_ref, wr_ref, br_ref, *rest):
    weight_refs = rest[:4]
    y_hbm, xbuf, obuf, sem_in, sem_out = rest[4:]
    tile = xbuf.shape[1] // ROW_TILES
    n_tok = y_hbm.shape[0] - 2 * tile

    def token_rows(r):
        return pl.ds(pl.multiple_of(r * ROW_TILES, ROW_TILES), ROW_TILES)

    def gather_start(tt, sl, inline):
        j0 = tile_j_ref[tt]

        def one(r, priority=0):
            tok = order_ref[j0 + r]
            pltpu.make_async_copy(x1_hbm.at[tok], xbuf.at[sl, token_rows(r)],
                                  sem_in.at[sl]).start(priority=priority)

        if inline:
            for r in range(tile):
                one(r, 1)
        else:
            def body(r, c):
                one(r)
                return c
            lax.fori_loop(0, tile, body, 0, unroll=8)

    def gather_wait(sl):
        pltpu.make_async_copy(xbuf.at[sl], xbuf.at[sl], sem_in.at[sl]).wait()

    def scatter_start(j0, n, sl, inline):
        def one(r, priority=0):
            tok = jnp.where(r < n, order_ref[j0 + r], n_tok + sl * tile + r)
            pltpu.make_async_copy(obuf.at[sl, token_rows(r)], y_hbm.at[tok],
                                  sem_out.at[sl]).start(priority=priority)

        if inline:
            for r in range(tile):
                one(r, r % 2)
        else:
            def body(r, c):
                one(r)
                return c
            lax.fori_loop(0, tile, body, 0, unroll=8)

    def scatter_wait(sl):
        pltpu.make_async_copy(obuf.at[sl], obuf.at[sl], sem_out.at[sl]).wait()

    def one_tile(t, slot, wgu_lo_ref, wdn_lo_ref, wgu_hi_ref, wdn_hi_ref):
        prev = jnp.maximum(t - 1, 0)
        valid = tile_n_ref[t] > 0
        prev_valid = (t > 0) & (tile_n_ref[prev] > 0)

        @pl.when(t == 0)
        def _():
            obuf[...] = jnp.zeros_like(obuf)
            scatter_start(0, 0, 0, inline=False)
            scatter_wait(0)

            @pl.when(valid)
            def _():
                gather_start(t, slot, inline=False)

        @pl.when(jnp.logical_not(valid) & prev_valid)
        def _():
            gather_wait(slot)
            scatter_wait(slot)
            scatter_start(tile_j_ref[prev], tile_n_ref[prev], 1 - slot, inline=False)
            scatter_wait(1 - slot)

        @pl.when(valid)
        def _():
            gather_wait(slot)
            gather_start(t + 1, 1 - slot, inline=True)
            scatter_start(tile_j_ref[prev], jnp.where(t > 0, tile_n_ref[prev], 0), 1 - slot,
                          inline=True)

            h = _rms(_load_token_tiles(xbuf.at[slot]), fnw_ref[...]).astype(BF16)
            logits = _dot(h, wr_ref[...]) + br_ref[...]
            lane = lax.broadcasted_iota(I32, logits.shape, 1)
            gl = jnp.where(lane < N_EXPERT_GROUPS, logits, -jnp.inf)
            gweight = 1.0 / jnp.sum(jnp.exp(gl - jnp.max(gl, axis=-1, keepdims=True)),
                                    axis=-1, keepdims=True)
            pick = lambda e: jnp.sum(jnp.where(lane == N_EXPERT_GROUPS + e, logits, 0.0),
                                     axis=-1, keepdims=True)
            l_lo = pick(tile_lo_ref[t])
            l_hi = pick(tile_hi_ref[t])
            wts = (gweight / (1.0 + jnp.exp(l_hi - l_lo)), gweight / (1.0 + jnp.exp(l_lo - l_hi)))
            acc = None
            for half, wgu_ref, wdn_ref in ((0, wgu_lo_ref, wdn_lo_ref), (1, wgu_hi_ref, wdn_hi_ref)):
                gu = _dot(h, wgu_ref[...])
                g = gu[:, :D_EXPERT]
                he = (g * _sigmoid(g) * gu[:, D_EXPERT:] * wts[half]).astype(BF16)
                d = _dot(he, wdn_ref[...])
                acc = d if acc is None else acc + d

            @pl.when(t > 0)
            def _():
                scatter_wait(slot)
            _store_token_tiles(obuf.at[slot], acc)

    one_tile(pl.program_id(0), pl.program_id(0) % 2, *weight_refs)


def _moe(x1t, order, tile_lo, tile_hi, tile_j, tile_n, fnw, wr, br, wgu, wdn):
    t = x1t.shape[0]
    n_tiles = tile_lo.shape[0]
    tile = MOE_TILE
    wspec = lambda shape, which: pl.BlockSpec(
        (None,) + shape, lambda i, lo, hi, tj, tn, od: ((lo, hi)[which][i], 0, 0))
    grid_spec = pltpu.PrefetchScalarGridSpec(
        num_scalar_prefetch=5,
        grid=(n_tiles,),
        in_specs=[pl.BlockSpec(memory_space=pl.ANY),
                  pl.BlockSpec((1, D_MODEL), lambda i, *_: (0, 0)),
                  pl.BlockSpec((D_MODEL, LANES), lambda i, *_: (0, 0)),
                  pl.BlockSpec((1, LANES), lambda i, *_: (0, 0)),
                  wspec((D_MODEL, 2 * D_EXPERT), 0), wspec((D_EXPERT, D_MODEL), 0),
                  wspec((D_MODEL, 2 * D_EXPERT), 1), wspec((D_EXPERT, D_MODEL), 1)],
        out_specs=pl.BlockSpec(memory_space=pl.ANY),
        scratch_shapes=[pltpu.VMEM((2, tile * ROW_TILES, LANES), F32),
                        pltpu.VMEM((2, tile * ROW_TILES, LANES), F32),
                        pltpu.SemaphoreType.DMA((2,)), pltpu.SemaphoreType.DMA((2,))],
    )
    return pl.pallas_call(
        _moe_kernel,
        grid_spec=grid_spec,
        out_shape=jax.ShapeDtypeStruct((t + 2 * tile, ROW_TILES, LANES), F32),
        compiler_params=pltpu.CompilerParams(dimension_semantics=("arbitrary",),
                                             vmem_limit_bytes=VMEM_LIMIT),
        name="moe",
    )(tile_lo, tile_hi, tile_j, tile_n, order, x1t, fnw, wr, br, wgu, wdn, wgu, wdn)


def _moe_plan(route, t):
    tile = MOE_TILE
    n_tiles = t // tile + N_CLASSES + 1
    cls = route.reshape(-1, 8, route.shape[1])[:, 0, :].reshape(t).astype(I32)
    _, order = lax.sort((cls, jnp.arange(t, dtype=I32)), num_keys=1, is_stable=True)
    order = jnp.concatenate([order, jnp.zeros((tile,), I32)])
    class_ids = jnp.arange(N_CLASSES, dtype=I32)
    counts = jnp.sum((cls[:, None] == class_ids[None, :]).astype(I32), axis=0)
    tiles_per = (counts + tile - 1) // tile
    tile_end = jnp.cumsum(tiles_per)
    tile_begin = tile_end - tiles_per
    starts = jnp.cumsum(counts) - counts
    t_idx = jnp.arange(n_tiles, dtype=I32)
    tile_cls = jnp.sum((tile_end[None, :] <= t_idx[:, None]).astype(I32), axis=1)
    valid = t_idx < tile_end[-1]
    last_cls = jnp.max(jnp.where(counts > 0, class_ids, 0))
    tile_cls = jnp.where(valid, tile_cls, last_cls)
    onehot = (tile_cls[:, None] == class_ids[None, :]).astype(I32)
    pick = lambda v: jnp.sum(onehot * v[None, :], axis=1)
    k = t_idx - pick(tile_begin)
    tile_j = jnp.where(valid, pick(starts) + k * tile, 0)
    tile_n = jnp.where(valid, jnp.clip(pick(counts) - k * tile, 0, tile), 0)
    lo_of_pair, hi_of_pair = [], []
    for lo in range(EXPERTS_PER_GROUP):
        for hi in range(lo + 1, EXPERTS_PER_GROUP):
            lo_of_pair.append(lo)
            hi_of_pair.append(hi)
    grp = class_ids // PAIRS_PER_GROUP
    class_lo = grp * EXPERTS_PER_GROUP + jnp.asarray(lo_of_pair * N_EXPERT_GROUPS, I32)
    class_hi = grp * EXPERTS_PER_GROUP + jnp.asarray(hi_of_pair * N_EXPERT_GROUPS, I32)
    return order, pick(class_lo), pick(class_hi), tile_j.astype(I32), tile_n.astype(I32)


def _final_kernel(x1_ref, ym_ref, p_ref, pnw_ref, wg_ref, bg_ref, wp_ref, ppnw_ref, fnw_ref, o_ref):
    x2 = _load_token_tiles(x1_ref) + _load_token_tiles(ym_ref)
    gate = _sigmoid(_dot(_rms(x2, pnw_ref[...]).astype(BF16), wg_ref[...]) + bg_ref[...])
    ple = _rms(_dot(p_ref[...].astype(BF16), wp_ref[...]), ppnw_ref[...])
    o_ref[...] = _rms(x2 + gate * ple, fnw_ref[...])


def _final(x1, ym, p2, pnw, wg, bg, wp, ppnw, fnw):
    t = p2.shape[0]
    tm = min(TOKEN_TILE, t)
    row = lambda w: pl.BlockSpec((tm, w), lambda i: (i, 0))
    weights = (pnw, wg, bg, wp, ppnw, fnw)
    return pl.pallas_call(
        _final_kernel,
        grid=(t // tm,),
        in_specs=[_token_tile_spec(tm), _token_tile_spec(tm), row(PLE_DIM)]
        + [_full(w.shape) for w in weights],
        out_specs=row(D_MODEL),
        out_shape=jax.ShapeDtypeStruct((t, D_MODEL), F32),
        compiler_params=pltpu.CompilerParams(dimension_semantics=("parallel",),
                                             vmem_limit_bytes=VMEM_LIMIT),
        name="final",
    )(x1, ym, p2, *weights)


def _pad_cols(w, width, offset=0):
    out = jnp.zeros((w.shape[0], width), w.dtype)
    return out.at[:, offset:offset + w.shape[1]].set(w)


def _rope_swap(w):
    half = MLA_ROPE_DIM // 2
    return jnp.concatenate([-w[..., half:], w[..., :half]], axis=-1)


def _layer(x2, p2, cos_t, sin_t, b, s, attn_norm_w, w_in, q_norm_w, w_uq, kv_norm_w, w_ukv,
           attn_out_norm_w, conv_w, conv_b, dt_bias, a_log, ssd_d, ssd_norm_w, w_o, ffn_norm_w,
           w_router_group, b_router_group, w_router_expert, b_router_expert, w_exp_gate,
           w_exp_up, w_exp_down, ple_norm_w, w_ple_gate, b_ple_gate, w_ple_proj, ple_post_norm_w):
    t = x2.shape[0]
    r1 = lambda v: v.reshape(1, -1).astype(F32)
    o1 = MLA_Q_RANK
    o2 = o1 + MLA_KV_RANK
    o3 = o2 + MLA_ROPE_DIM
    o4 = o3 + SSD_INNER
    o5 = o4 + SSD_XBC
    w_kr = w_in[:, o2:o3]
    wa = jnp.concatenate([w_in[:, :o2], _pad_cols(w_kr, HEAD_PAD, MLA_NOPE_DIM),
                          _pad_cols(_rope_swap(w_kr), HEAD_PAD, MLA_NOPE_DIM),
                          _pad_cols(w_in[:, o5:], LANES)], axis=1).astype(BF16)
    wz = w_in[:, o3:o4].astype(BF16)
    wxbc = w_in[:, o4:o5].astype(BF16)
    uq = w_uq.reshape(MLA_Q_RANK, MLA_HEADS, MLA_NOPE_DIM + MLA_ROPE_DIM)
    zq = jnp.zeros((MLA_Q_RANK, MLA_HEADS, HEAD_PAD - MLA_NOPE_DIM - MLA_ROPE_DIM), F32)
    wuq = jnp.concatenate([uq, zq], axis=-1).reshape(MLA_Q_RANK, -1).astype(BF16)
    wuqs = jnp.concatenate([jnp.zeros_like(uq[..., :MLA_NOPE_DIM]), _rope_swap(uq[..., MLA_NOPE_DIM:]),
                            zq], axis=-1).reshape(MLA_Q_RANK, -1).astype(BF16)
    ukv = w_ukv.reshape(MLA_KV_RANK, MLA_HEADS, MLA_NOPE_DIM + MLA_V_DIM)
    zk = jnp.zeros((MLA_KV_RANK, MLA_HEADS, HEAD_PAD - MLA_NOPE_DIM), F32)
    wuk = jnp.concatenate([ukv[..., :MLA_NOPE_DIM], zk], axis=-1).reshape(MLA_KV_RANK, -1).astype(BF16)
    wuv = ukv[..., MLA_NOPE_DIM:].reshape(MLA_KV_RANK, -1).T.astype(BF16)

    q, k, v, z, act, dt_raw = _inproj(x2, cos_t, sin_t, s, r1(attn_norm_w), wa, wz, wxbc,
                                      conv_w.astype(F32), r1(conv_b), r1(q_norm_w), wuq, wuqs,
                                      r1(kv_norm_w), wuk, wuv)
    attn = _attention(q, k, v, b, s)

    head_of_lane = jnp.arange(SSD_INNER) // SSD_HEAD_DIM
    rows = jnp.arange(LANES)[:, None]
    e_mat = jnp.stack([(rows == d * SSD_HEADS + head_of_lane[None, :]) for d in range(SSD_DIRECTIONS)]
                      ).astype(BF16)
    skip = jnp.repeat(ssd_d.astype(F32), SSD_HEAD_DIM).reshape(1, -1)
    yf, yb = _ssd(act, dt_raw, _pad_cols(r1(dt_bias), LANES), _pad_cols(r1(a_log), LANES), skip,
                  e_mat, b, s)

    wr = _pad_cols(jnp.concatenate([w_router_group, w_router_expert], axis=1).astype(F32), LANES)
    wr_hi = wr.astype(BF16)
    br = _pad_cols(jnp.concatenate([r1(b_router_group), r1(b_router_expert)], axis=1), LANES)
    gpad = jnp.zeros((EXPERTS_PER_GROUP - N_EXPERT_GROUPS, D_MODEL), F32)
    wrt = jnp.concatenate([w_router_group.T.astype(F32), gpad, w_router_expert.T.astype(F32),
                           jnp.zeros((ROUTE_ROWS - EXPERTS_PER_GROUP - N_EXPERTS, D_MODEL), F32)])
    wrt_hi, wrt_lo, _ = _split3(wrt)
    brt = jnp.concatenate([b_router_group.astype(F32), gpad[:, 0], b_router_expert.astype(F32),
                           jnp.zeros((ROUTE_ROWS - EXPERTS_PER_GROUP - N_EXPERTS,), F32)])
    x1, route = _mixout(x2, attn, yf, yb, z, r1(attn_out_norm_w), r1(ssd_norm_w), w_o.astype(BF16),
                        r1(ffn_norm_w), jnp.concatenate([wrt_hi, wrt_lo]), brt.reshape(-1, 1))

    order, tile_lo, tile_hi, tile_j, tile_n = _moe_plan(route, t)
    wgu = jnp.concatenate([w_exp_gate, w_exp_up], axis=-1).astype(BF16)
    ym = _moe(x1.reshape(t, ROW_TILES, LANES), order, tile_lo, tile_hi, tile_j, tile_n,
              r1(ffn_norm_w), wr_hi, br, wgu, w_exp_down.astype(BF16))
    ym = ym.reshape(ym.shape[0] * ROW_TILES, LANES)
    return x1, ym, (r1(ple_norm_w), w_ple_gate.astype(BF16), r1(b_ple_gate), w_ple_proj.astype(BF16),
                    r1(ple_post_norm_w))


def kernel(x, p, positions, attn_norm_w, w_in, q_norm_w, w_uq, kv_norm_w, w_ukv, attn_out_norm_w, conv_w, conv_b, dt_bias, a_log, ssd_d, ssd_norm_w, w_o, ffn_norm_w, w_router_group, b_router_group, w_router_expert, b_router_expert, w_exp_gate, w_exp_up, w_exp_down, ple_norm_w, w_ple_gate, b_ple_gate, w_ple_proj, ple_post_norm_w, final_norm_w):
    b, s, d = x.shape
    depth = p.shape[0]
    assert depth == 1, "the fused final stage assumes a single layer"
    t = b * s
    inv_freq = 1.0 / (ROPE_THETA ** (jnp.arange(0, MLA_ROPE_DIM, 2, dtype=F32) / MLA_ROPE_DIM))
    half = MLA_ROPE_DIM // 2
    ang = (positions.astype(F32).reshape(t, 1) * inv_freq).reshape(t * half // LANES, LANES)
    cos, sin = lax.optimization_barrier((jnp.cos(ang), jnp.sin(ang)))
    cos = cos.reshape(t, half)
    sin = sin.reshape(t, half)
    ones = jnp.ones((t, MLA_NOPE_DIM), F32)
    zeros = jnp.zeros((t, HEAD_PAD - MLA_NOPE_DIM - MLA_ROPE_DIM), F32)
    cos_t = jnp.concatenate([ones, cos, cos, zeros], axis=1)
    sin_t = jnp.concatenate([0.0 * ones, sin, sin, zeros], axis=1)

    x2 = x.reshape(t, d)
    i = 0
    x1, ym, (pnw, wg, bg, wp, ppnw) = _layer(
        x2, p[i].reshape(t, -1), cos_t, sin_t, b, s, attn_norm_w[i], w_in[i], q_norm_w[i], w_uq[i],
        kv_norm_w[i], w_ukv[i], attn_out_norm_w[i], conv_w[i], conv_b[i], dt_bias[i], a_log[i],
        ssd_d[i], ssd_norm_w[i], w_o[i], ffn_norm_w[i], w_router_group[i], b_router_group[i],
        w_router_expert[i], b_router_expert[i], w_exp_gate[i], w_exp_up[i], w_exp_down[i],
        ple_norm_w[i], w_ple_gate[i], b_ple_gate[i], w_ple_proj[i], ple_post_norm_w[i])
    out = _final(x1, ym, p[i].reshape(t, -1), pnw, wg, bg, wp, ppnw, final_norm_w.reshape(1, -1).astype(F32))
    return out.reshape(b, s, d)
```
